```python
import math
import jax, jax.numpy as jnp
from jax import lax
import numpy as np


D_MODEL = 1024
BATCH = 8
SEQ = 2048
DEPTH = 1
DEC_BATCH = 32
DEC_SEQ = 8
PAST_LEN = 16384
PAGE_SIZE = 128

HG_HEADS = 8
HG_DK = D_MODEL // HG_HEADS
HG_DV = HG_DK
HG_WIDTH = HG_HEADS * HG_DK
HG_CHUNK = 32

NSA_HEADS = 16
NSA_HD = D_MODEL // NSA_HEADS
NSA_KV_HEADS = 4
NSA_GROUP = NSA_HEADS // NSA_KV_HEADS
NSA_WIDTH = NSA_HEADS * NSA_HD
KV_WIDTH = NSA_KV_HEADS * NSA_HD
N_NSA_BRANCH = 3
CMP_BLOCK = 32
SEL_BLOCK = 64
SEL_TOPK = 8
WINDOW = 512
Q_BLOCK = 128
SEL_Q_BLOCK = 64
FORCE_BONUS = 1.0e4
ATT_SCALE = NSA_HD ** -0.5

N_BRANCH = 2
D_FF = 2816
CONV_W = 3

EPS = 1e-6
NEG = -1e30

IN_SPLITS = (HG_WIDTH,) * 4 + (NSA_WIDTH,) + (KV_WIDTH,) * 6 + (NSA_HEADS * N_NSA_BRANCH, N_BRANCH * D_MODEL)
IN_COLS = sum(IN_SPLITS)
IN_OFFSETS = [int(o) for o in np.cumsum(IN_SPLITS)[:-1]]

kernel_name = 'hybrid_hgrn2_nsa_convffn_step'


def rmsnorm(x, g):
    x32 = x.astype(jnp.float32)
    y = x32 * lax.rsqrt(jnp.mean(x32 * x32, axis=-1, keepdims=True) + EPS)
    return (y * g.astype(jnp.float32)).astype(x.dtype)


def masked_softmax(s, mask):
    s = jnp.where(mask, s.astype(jnp.float32), NEG)
    e = jnp.where(mask, jnp.exp(s - jnp.max(s, axis=-1, keepdims=True)), 0.0)
    return e / jnp.maximum(jnp.sum(e, axis=-1, keepdims=True), 1e-30)


def mixer_inputs(x, g_pre, w_in):
    h = rmsnorm(x, g_pre)
    return jnp.split(h @ w_in, IN_OFFSETS, axis=-1)


def stack_kv(k, v):
    B, T = k.shape[:2]
    return jnp.stack([k.reshape(B, T, NSA_KV_HEADS, NSA_HD), v.reshape(B, T, NSA_KV_HEADS, NSA_HD)], axis=2)


def hgrn2_scan(q, logf, k, v, s0):
    B, T, H, _ = q.shape
    C = math.gcd(T, HG_CHUNK)
    N = T // C
    f32 = jnp.float32
    rs = lambda a: a.astype(f32).reshape(B, N, C, H, a.shape[-1]).transpose(1, 0, 3, 2, 4)
    qc, gc, kc, vc = rs(q), rs(logf), rs(k), rs(v)
    b = jnp.cumsum(gc, axis=3)
    b_last = b[:, :, :, -1:, :]
    q_in = qc * jnp.exp(b)
    k_out = kc * jnp.exp(b_last - b)
    q_rel = qc * jnp.exp(b - b_last)
    causal = jnp.tril(jnp.ones((C, C), dtype=bool))
    attn = jnp.where(causal, jnp.einsum('nbhtk,nbhsk->nbhts', q_rel, k_out), 0.0)
    o_intra = jnp.einsum('nbhts,nbhsv->nbhtv', attn, vc)
    chunk_update = jnp.einsum('nbhsk,nbhsv->nbhkv', k_out, vc)
    decay = jnp.exp(b_last[:, :, :, 0, :])

    def step(S, inp):
        d, u = inp
        return d[..., None] * S + u, S

    s_final, s_prev = lax.scan(step, s0.astype(f32), (decay, chunk_update))
    o_inter = jnp.einsum('nbhtk,nbhkv->nbhtv', q_in, s_prev)
    o = (o_intra + o_inter).transpose(1, 0, 3, 2, 4).reshape(B, T, H, v.shape[-1])
    return o, s_final


def hgrn2_branch(hq, hf, hi, hg, lb, norm_g, s0):
    B, T = hq.shape[:2]
    f32 = jnp.float32
    heads = lambda a: a.reshape(B, T, HG_HEADS, -1)
    lbh = lb.reshape(HG_HEADS, HG_DK)
    z = heads(hf).astype(f32)
    logf = jnp.log(lbh + (1.0 - lbh) * jax.nn.sigmoid(z))
    k = (1.0 - lbh) * jax.nn.sigmoid(-z)
    o, s_new = hgrn2_scan(jax.nn.silu(heads(hq)), logf, k, heads(hi), s0)
    o = rmsnorm(o, norm_g) * jax.nn.silu(heads(hg).astype(f32))
    return o.reshape(B, T, HG_WIDTH).astype(hq.dtype), s_new


def gqa_attend(q, k, v, mask):
    s = jnp.einsum('bntgrd,bnsgd->bngrts', q, k) * ATT_SCALE
    p = masked_softmax(s, mask)
    return jnp.einsum('bngrts,bnsgd->bntgrd', p.astype(v.dtype), v), p


def block_means(rows):
    B, L = rows.shape[:2]
    nc = L // CMP_BLOCK
    blk = rows[:, :nc * CMP_BLOCK].astype(jnp.float32).reshape((B, nc, CMP_BLOCK) + rows.shape[2:])
    return blk.mean(axis=2).astype(rows.dtype)


def compressed_branch(qg, kc, q_pos):
    nc = kc.shape[1]
    blk_end = (jnp.arange(nc) + 1) * CMP_BLOCK - 1
    mask = blk_end[None, :] <= q_pos[:, None]
    o, p = gqa_attend(qg[:, None], kc[:, None, :, 0], kc[:, None, :, 1], mask)
    return o[:, 0], p[:, 0].sum(axis=2)


def select_blocks(imp, q_pos, n_sel):
    B, G, T, nc = imp.shape
    ratio = SEL_BLOCK // CMP_BLOCK
    imp = jnp.pad(imp, ((0, 0), (0, 0), (0, 0), (0, n_sel * ratio - nc))).reshape(B, G, T, n_sel, ratio).sum(-1)
    blk = jnp.arange(n_sel)[None, :]
    cur = (q_pos // SEL_BLOCK)[:, None]
    forced = (blk == 0) | (blk == cur)
    score = jnp.where(blk <= cur, imp + jnp.where(forced, FORCE_BONUS, 0.0), NEG)
    top, idx = lax.top_k(score, min(SEL_TOPK, n_sel))
    return idx, top > NEG / 2


def sel_attend(qg, kv, idx, valid, q_pos):
    B, G, T, K = idx.shape
    k = kv[..., 0, :].reshape(B, G, T, K * SEL_BLOCK, NSA_HD)
    v = kv[..., 1, :].reshape(B, G, T, K * SEL_BLOCK, NSA_HD)
    key_pos = (idx[..., None] * SEL_BLOCK + jnp.arange(SEL_BLOCK)).reshape(B, G, T, K * SEL_BLOCK)
    mask = jnp.repeat(valid, SEL_BLOCK, axis=-1) & (key_pos <= q_pos[:, None])
    s = jnp.einsum('btgrd,bgtsd->bgrts', qg, k) * ATT_SCALE
    p = masked_softmax(s, mask[:, :, None])
    return jnp.einsum('bgrts,bgtsd->btgrd', p.astype(v.dtype), v)


def window_prompt(qg, kvw):
    B, T = qg.shape[:2]
    qb = math.gcd(T, Q_BLOCK)
    nqb = T // qb
    span = WINDOW + qb
    kv_pad = jnp.pad(kvw, ((0, 0), (WINDOW, 0), (0, 0), (0, 0), (0, 0)))
    rows = jnp.arange(nqb)[:, None] * qb + jnp.arange(span)[None, :]
    kvb = kv_pad[:, rows]
    key_pos = rows - WINDOW
    q_pos = jnp.arange(T).reshape(nqb, qb)
    d = q_pos[:, :, None] - key_pos[:, None, :]
    mask = (key_pos[:, None, :] >= 0) & (d >= 0) & (d < WINDOW)
    o, _ = gqa_attend(qg.reshape(B, nqb, qb, NSA_KV_HEADS, NSA_GROUP, NSA_HD),
                      kvb[:, :, :, 0], kvb[:, :, :, 1], mask[None, :, None, None])
    return o.reshape(B, T, NSA_KV_HEADS, NSA_GROUP, NSA_HD)


def window_sample(qg, win_buf, kvw_new, q_pos):
    wb = win_buf.shape[1]
    kv = jnp.concatenate([win_buf.astype(kvw_new.dtype), kvw_new], axis=1)
    key_pos = PAST_LEN - wb + jnp.arange(kv.shape[1])
    d = q_pos[:, None] - key_pos[None, :]
    mask = (d >= 0) & (d < WINDOW)
    o, _ = gqa_attend(qg[:, None], kv[:, None, :, 0], kv[:, None, :, 1], mask)
    return o[:, 0], kv[:, -wb:]


def nsa_combine(o_c, o_s, o_w, ngate):
    B, T = o_c.shape[:2]
    g = jax.nn.sigmoid(ngate.astype(jnp.float32)).reshape(B, T, NSA_KV_HEADS, NSA_GROUP, N_NSA_BRANCH)
    o = g[..., 0:1] * o_c + g[..., 1:2] * o_s + g[..., 2:3] * o_w
    return o.reshape(B, T, NSA_WIDTH).astype(o_c.dtype)


def nsa_prompt(nq, kvc, kvs, kvw, ngate):
    B, T = nq.shape[:2]
    qg = nq.reshape(B, T, NSA_KV_HEADS, NSA_GROUP, NSA_HD)
    q_pos = jnp.arange(T)
    o_c, imp = compressed_branch(qg, block_means(kvc), q_pos)
    n_sel = -(-T // SEL_BLOCK)
    idx, valid = select_blocks(imp, q_pos, n_sel)
    k_top = idx.shape[-1]
    sel_blocks = jnp.pad(kvs, ((0, 0), (0, n_sel * SEL_BLOCK - T), (0, 0), (0, 0), (0, 0))).reshape(
        B, n_sel, SEL_BLOCK, 2, NSA_KV_HEADS, NSA_HD)
    b_idx = jnp.arange(B)[:, None, None, None]
    g_idx = jnp.arange(NSA_KV_HEADS)[None, :, None, None]
    qb = math.gcd(T, SEL_Q_BLOCK)
    nqb = T // qb

    def sweep(args):
        q_blk, idx_blk, valid_blk, pos_blk = args
        kv = sel_blocks[b_idx, idx_blk, :, :, g_idx, :]
        return sel_attend(q_blk, kv, idx_blk, valid_blk, pos_blk)

    o_s = lax.map(sweep, (qg.reshape(B, nqb, qb, NSA_KV_HEADS, NSA_GROUP, NSA_HD).swapaxes(0, 1),
                          idx.reshape(B, NSA_KV_HEADS, nqb, qb, k_top).transpose(2, 0, 1, 3, 4),
                          valid.reshape(B, NSA_KV_HEADS, nqb, qb, k_top).transpose(2, 0, 1, 3, 4),
                          q_pos.reshape(nqb, qb)))
    o_s = o_s.swapaxes(0, 1).reshape(B, T, NSA_KV_HEADS, NSA_GROUP, NSA_HD)
    o_w = window_prompt(qg, kvw)
    return nsa_combine(o_c, o_s, o_w, ngate)


def nsa_sample(nq, kvc, kvs, kvw, ngate, pool_c, pool_s, win_buf, page_table):
    B, T = nq.shape[:2]
    qg = nq.reshape(B, T, NSA_KV_HEADS, NSA_GROUP, NSA_HD)
    q_pos = PAST_LEN + jnp.arange(T)
    past_c = pool_c[page_table].reshape(B, PAST_LEN, 2, NSA_KV_HEADS, NSA_HD)
    kc = jnp.concatenate([block_means(past_c).astype(kvc.dtype), block_means(kvc)], axis=1)
    o_c, imp = compressed_branch(qg, kc, q_pos)
    n_sel = -(-(PAST_LEN + T) // SEL_BLOCK)
    idx, valid = select_blocks(imp, q_pos, n_sel)
    bpp = PAGE_SIZE // SEL_BLOCK
    n_past_blk = PAST_LEN // SEL_BLOCK
    pool_blk = pool_s.reshape(-1, SEL_BLOCK, 2, NSA_KV_HEADS, NSA_HD)
    b_idx = jnp.arange(B)[:, None, None, None]
    g_idx = jnp.arange(NSA_KV_HEADS)[None, :, None, None]
    pidx = jnp.minimum(idx, n_past_blk - 1)
    phys = page_table[b_idx, pidx // bpp] * bpp + pidx % bpp
    n_new_blk = -(-T // SEL_BLOCK)
    new_blk = jnp.pad(kvs, ((0, 0), (0, n_new_blk * SEL_BLOCK - T), (0, 0), (0, 0), (0, 0))).reshape(
        B, n_new_blk, SEL_BLOCK, 2, NSA_KV_HEADS, NSA_HD)
    nidx = jnp.clip(idx - n_past_blk, 0, n_new_blk - 1)
    kv_sel = jnp.where((idx < n_past_blk)[..., None, None, None],
                       pool_blk[phys, :, :, g_idx, :].astype(kvs.dtype),
                       new_blk[b_idx, nidx, :, :, g_idx, :])
    o_s = sel_attend(qg, kv_sel, idx, valid, q_pos)
    o_w, win_new = window_sample(qg, win_buf, kvw, q_pos)
    return nsa_combine(o_c, o_s, o_w, ngate), win_new


def merge_out(o_hg, o_nsa, mgate, w_out):
    B, T = o_hg.shape[:2]
    g = jax.nn.sigmoid(mgate.astype(jnp.float32)).reshape(B, T, N_BRANCH, D_MODEL)
    u = g[:, :, 0] * o_hg + g[:, :, 1] * o_nsa
    return u.astype(o_hg.dtype) @ w_out


def conv_ffn(h, w_up, conv_w, conv_b, w_down, buf):
    T = h.shape[1]
    up = h @ w_up
    ext = jnp.concatenate([buf.astype(up.dtype), up], axis=1)
    c = conv_b + ext[:, 0:T] * conv_w[0]
    for j in range(1, CONV_W):
        c = c + ext[:, j:j + T] * conv_w[j]
    a, b = jnp.split(c, 2, axis=-1)
    return (jax.nn.silu(a) * b) @ w_down, ext[:, T:]


def setup_inputs(seed: int = 0) -> dict:
    key = jax.random.key(seed)
    ks = jax.random.split(key, 24)
    f32 = jnp.float32
    n_pages = PAST_LEN // PAGE_SIZE
    n_used = DEC_BATCH * n_pages
    n_pool = (5 * n_used + 3) // 4
    wb = min(WINDOW, PAST_LEN)
    nrm = lambda k, shape, s: jax.random.normal(k, shape, f32) * s
    page_table = jax.random.permutation(ks[0], n_pool)[:n_used].reshape(DEC_BATCH, n_pages).astype(jnp.int32)
    return {
        'x_prompt': nrm(ks[1], (BATCH, SEQ, D_MODEL), 1.0),
        'x_sample': nrm(ks[2], (DEC_BATCH, DEC_SEQ, D_MODEL), 1.0),
        'cache_cmp_kv': nrm(ks[3], (DEPTH, n_pool, PAGE_SIZE, 2, NSA_KV_HEADS, NSA_HD), 1.0),
        'cache_sel_kv': nrm(ks[4], (DEPTH, n_pool, PAGE_SIZE, 2, NSA_KV_HEADS, NSA_HD), 1.0),
        'page_table': page_table,
        'state_win_kv': nrm(ks[5], (DEPTH, DEC_BATCH, wb, 2, NSA_KV_HEADS, NSA_HD), 1.0),
        'state_hgrn': nrm(ks[6], (DEPTH, DEC_BATCH, HG_HEADS, HG_DK, HG_DV), 0.5),
        'state_conv': nrm(ks[7], (DEPTH, DEC_BATCH, CONV_W - 1, 2 * D_FF), 1.0),
        'w_in': nrm(ks[8], (DEPTH, D_MODEL, IN_COLS), D_MODEL ** -0.5),
        'hg_lb_logits': nrm(ks[9], (DEPTH + 1, HG_WIDTH), 0.1),
        'hg_norm_g': 1.0 + nrm(ks[10], (DEPTH, HG_DV), 0.05),
        'w_out': nrm(ks[11], (DEPTH, D_MODEL, D_MODEL), D_MODEL ** -0.5),
        'w_up': nrm(ks[12], (DEPTH, D_MODEL, 2 * D_FF), D_MODEL ** -0.5),
        'conv_w': nrm(ks[13], (DEPTH, CONV_W, 2 * D_FF), 0.5),
        'conv_b': nrm(ks[14], (DEPTH, 2 * D_FF), 0.01),
        'w_down': nrm(ks[15], (DEPTH, D_FF, D_MODEL), D_FF ** -0.5),
        'g_pre_mix': 1.0 + nrm(ks[16], (DEPTH, D_MODEL), 0.05),
        'g_post_mix': 1.0 + nrm(ks[17], (DEPTH, D_MODEL), 0.05),
        'g_pre_ffn': 1.0 + nrm(ks[18], (DEPTH, D_MODEL), 0.05),
        'g_post_ffn': 1.0 + nrm(ks[19], (DEPTH, D_MODEL), 0.05),
    }


def reference(x_prompt, x_sample, cache_cmp_kv, cache_sel_kv, page_table, state_win_kv, state_hgrn, state_conv,
              w_in, hg_lb_logits, hg_norm_g, w_out, w_up, conv_w, conv_b, w_down,
              g_pre_mix, g_post_mix, g_pre_ffn, g_post_ffn):
    B, T = x_prompt.shape[:2]
    DB = x_sample.shape[0]
    lbs = jnp.cumsum(jax.nn.softmax(hg_lb_logits.astype(jnp.float32), axis=0), axis=0)
    xp, xs = x_prompt, x_sample
    cmp_p, cmp_s, sel_p, sel_s, win_p, win_s, hg_p, hg_s, cv_p, cv_s = ([] for _ in range(10))
    for l in range(DEPTH):
        hq, hf, hi, hg, nq, kc, vc, ksl, vsl, kw, vw, ng, mg = mixer_inputs(xp, g_pre_mix[l], w_in[l])
        o_hg, s_hg = hgrn2_branch(hq, hf, hi, hg, lbs[l], hg_norm_g[l],
                                  jnp.zeros((B, HG_HEADS, HG_DK, HG_DV), jnp.float32))
        kvc, kvs, kvw = stack_kv(kc, vc), stack_kv(ksl, vsl), stack_kv(kw, vw)
        o_nsa = nsa_prompt(nq, kvc, kvs, kvw, ng)
        xp = xp + rmsnorm(merge_out(o_hg, o_nsa, mg, w_out[l]), g_post_mix[l])
        f_out, cbuf = conv_ffn(rmsnorm(xp, g_pre_ffn[l]), w_up[l], conv_w[l], conv_b[l], w_down[l],
                               jnp.zeros((B, CONV_W - 1, 2 * D_FF), xp.dtype))
        xp = xp + rmsnorm(f_out, g_post_ffn[l])
        cmp_p.append(kvc)
        sel_p.append(kvs)
        win_p.append(kvw[:, -min(WINDOW, T):])
        hg_p.append(s_hg)
        cv_p.append(cbuf)
        hq, hf, hi, hg, nq, kc, vc, ksl, vsl, kw, vw, ng, mg = mixer_inputs(xs, g_pre_mix[l], w_in[l])
        o_hg, s_hg = hgrn2_branch(hq, hf, hi, hg, lbs[l], hg_norm_g[l], state_hgrn[l])
        kvc, kvs, kvw = stack_kv(kc, vc), stack_kv(ksl, vsl), stack_kv(kw, vw)
        o_nsa, wbuf = nsa_sample(nq, kvc, kvs, kvw, ng, cache_cmp_kv[l], cache_sel_kv[l], state_win_kv[l], page_table)
        xs = xs + rmsnorm(merge_out(o_hg, o_nsa, mg, w_out[l]), g_post_mix[l])
        f_out, cbuf = conv_ffn(rmsnorm(xs, g_pre_ffn[l]), w_up[l], conv_w[l], conv_b[l], w_down[l], state_conv[l])
        xs = xs + rmsnorm(f_out, g_post_ffn[l])
        cmp_s.append(kvc)
        sel_s.append(kvs)
        win_s.append(wbuf)
        hg_s.append(s_hg)
        cv_s.append(cbuf)
    return (xp, xs,
            jnp.stack(cmp_p), jnp.stack(cmp_s),
            jnp.stack(sel_p), jnp.stack(sel_s),
            jnp.stack(win_p), jnp.stack(win_s),
            jnp.stack(hg_p), jnp.stack(hg_s),
            jnp.stack(cv_p), jnp.stack(cv_s))
```

```python
import functools
import math

import jax
import jax.numpy as jnp
from jax import lax
from jax.experimental import pallas as pl
from jax.experimental.pallas import tpu as pltpu

F32 = jnp.float32
BF16 = jnp.bfloat16
I32 = jnp.int32

D_MODEL = 1024
HG_HEADS = 8
HG_DK = 128
HG_CHUNK = 32
NSA_HEADS = 16
NSA_HD = 64
KVH = 4
GROUP = 4
KV_W = KVH * NSA_HD
KV_ROW = 2 * KV_W
CMP_BLOCK = 32
SEL_BLOCK = 64
SEL_TOPK = 8
WINDOW = 512
FORCE_BONUS = 1.0e4
ATT_SCALE = NSA_HD ** -0.5
D_FF = 2816
CONV_W = 3
PAGE = 128
EPS = 1e-6
NEG = -1e30
M_INIT = -1e38

VMEM_LIMIT = 56 * 1024 * 1024

NT = (((1,), (1,)), ((), ()))
TN = (((0,), (0,)), ((), ()))


def _cparams(n_axes):
    return pltpu.CompilerParams(dimension_semantics=("arbitrary",) * n_axes,
                                vmem_limit_bytes=VMEM_LIMIT)


def _dot(a, b, dims=None):
    if dims is None:
        return jnp.dot(a, b, preferred_element_type=F32)
    return lax.dot_general(a, b, dims, preferred_element_type=F32)


def _rms(x, g):
    return x * lax.rsqrt(jnp.mean(x * x, axis=-1, keepdims=True) + EPS) * g


def _rmsnorm_kernel(x_ref, g_ref, o_ref):
    o_ref[...] = _rms(x_ref[...], g_ref[...]).astype(o_ref.dtype)


def _rmsnorm_bf16(x, g, tm=512):
    R, D = x.shape
    tm = min(tm, R)
    return pl.pallas_call(
        _rmsnorm_kernel, grid=(R // tm,),
        in_specs=[pl.BlockSpec((tm, D), lambda i: (i, 0)), pl.BlockSpec((1, D), lambda i: (0, 0))],
        out_specs=pl.BlockSpec((tm, D), lambda i: (i, 0)),
        out_shape=jax.ShapeDtypeStruct((R, D), BF16),
        compiler_params=_cparams(1), name="rmsnorm",
    )(x, g.reshape(1, D))


def _mm_kernel(a_ref, w_ref, o_ref):
    o_ref[...] = _dot(a_ref[...], w_ref[...]).astype(o_ref.dtype)


def _mm(a, w, out_dtype, tm=1024, tn=512, name="proj"):
    R, K = a.shape
    N = w.shape[1]
    tm, tn = min(tm, R), min(tn, N)
    return pl.pallas_call(
        _mm_kernel, grid=(R // tm, N // tn),
        in_specs=[pl.BlockSpec((tm, K), lambda i, j: (i, 0)), pl.BlockSpec((K, tn), lambda i, j: (0, j))],
        out_specs=pl.BlockSpec((tm, tn), lambda i, j: (i, j)),
        out_shape=jax.ShapeDtypeStruct((R, N), out_dtype),
        compiler_params=_cparams(2), name=name,
    )(a, w)


def _mm_nt_kernel(wt_ref, a_ref, o_ref):
    o_ref[...] = _dot(wt_ref[...], a_ref[...], NT).astype(o_ref.dtype)


def _mm_nt(wt, a, out_dtype, tm=1024, tn=256, name="proj_t"):
    N, K = wt.shape
    R = a.shape[0]
    tm, tn = min(tm, R), min(tn, N)
    return pl.pallas_call(
        _mm_nt_kernel, grid=(R // tm, N // tn),
        in_specs=[pl.BlockSpec((tn, K), lambda i, j: (j, 0)), pl.BlockSpec((tm, K), lambda i, j: (i, 0))],
        out_specs=pl.BlockSpec((tn, tm), lambda i, j: (j, i)),
        out_shape=jax.ShapeDtypeStruct((N, R), out_dtype),
        compiler_params=_cparams(2), name=name,
    )(wt, a)


def _merge_kernel(mg_ref, ohg_ref, onsa_ref, x_ref, w_ref, g_ref, o_ref):
    mg = mg_ref[...]
    u = jax.nn.sigmoid(mg[:, :D_MODEL]) * ohg_ref[...] + jax.nn.sigmoid(mg[:, D_MODEL:]) * onsa_ref[...]
    y = _dot(u.astype(BF16), w_ref[...])
    o_ref[...] = x_ref[...] + _rms(y, g_ref[...])


def _merge(mg, ohg, onsa, x, w_out, g_post, tm=512):
    R, D = x.shape
    tm = min(tm, R)
    row = lambda w: pl.BlockSpec((tm, w), lambda i: (i, 0))
    return pl.pallas_call(
        _merge_kernel, grid=(R // tm,),
        in_specs=[row(2 * D), row(D), row(D), row(D),
                  pl.BlockSpec((D, D), lambda i: (0, 0)), pl.BlockSpec((1, D), lambda i: (0, 0))],
        out_specs=row(D),
        out_shape=jax.ShapeDtypeStruct((R, D), F32),
        compiler_params=_cparams(1), name="merge",
    )(mg, ohg, onsa, x, w_out, g_post.reshape(1, D))


def _ffn_kernel(x_ref, gpre_ref, wup_ref, cw_ref, cb_ref, wdn_ref, gpost_ref, buf_ref,
                o_ref, cbuf_ref, acc_scr, carry_scr, *, ns, tn):
    tm = x_ref.shape[0]
    ls = tm // ns
    x = x_ref[...]
    h = _rms(x, gpre_ref[...]).astype(BF16)
    pos = lax.broadcasted_iota(I32, (tm, 1), 0) % ls
    if ns == 1:
        @pl.when(pl.program_id(1) == 0)
        def _():
            carry_scr[...] = buf_ref[0]
    acc_scr[...] = jnp.zeros_like(acc_scr)
    for j in range(D_FF // tn):
        halves = []
        for base in (j * tn, D_FF + j * tn):
            cols = slice(base, base + tn)
            up = _dot(h, wup_ref[:, cols])
            if ns == 1:
                b0 = carry_scr[0:1, cols]
                b1 = carry_scr[1:2, cols]
                carry_scr[:, cols] = up[tm - 2:tm, :]
                cbuf_ref[0, :, cols] = up[tm - 2:tm, :]
            else:
                bufv = buf_ref[:, :, cols]
                b0 = jnp.broadcast_to(bufv[:, 0:1, :], (ns, ls, tn)).reshape(tm, tn)
                b1 = jnp.broadcast_to(bufv[:, 1:2, :], (ns, ls, tn)).reshape(tm, tn)
                cbuf_ref[:, :, cols] = up.reshape(ns, ls, tn)[:, ls - 2:ls, :]
            prev1 = jnp.where(pos == 0, b1, pltpu.roll(up, 1, 0))
            prev2 = jnp.where(pos == 0, b0, jnp.where(pos == 1, b1, pltpu.roll(up, 2, 0)))
            c = cb_ref[:, cols] + prev2 * cw_ref[0:1, cols]
            c = c + prev1 * cw_ref[1:2, cols]
            c = c + up * cw_ref[2:3, cols]
            halves.append(c)
        act = (jax.nn.silu(halves[0]) * halves[1]).astype(BF16)
        acc_scr[...] += _dot(act, wdn_ref[j * tn:(j + 1) * tn, :])
    o_ref[...] = x + _rms(acc_scr[...], gpost_ref[...])


def _ffn(x, n_seq, g_pre, w_up, conv_w, conv_b, w_down, g_post, buf, tm):
    R, D = x.shape
    L = R // n_seq
    C2 = 2 * D_FF
    if L >= tm:
        ns, nt = 1, L // tm
        grid = (n_seq, nt)
        buf_spec = pl.BlockSpec((1, 2, C2), lambda b, t: (b, 0, 0))
    else:
        ns, nt, tm = n_seq, 1, R
        grid = (1, 1)
        buf_spec = pl.BlockSpec((ns, 2, C2), lambda b, t: (0, 0, 0))
    const = lambda shape: pl.BlockSpec(shape, lambda b, t: (0,) * len(shape), pipeline_mode=pl.Buffered(1))
    return pl.pallas_call(
        functools.partial(_ffn_kernel, ns=ns, tn=256), grid=grid,
        in_specs=[pl.BlockSpec((tm, D), lambda b, t: (b * nt + t, 0)), const((1, D)), const((D, C2)),
                  const((CONV_W, C2)), const((1, C2)), const((D_FF, D)), const((1, D)), buf_spec],
        out_specs=[pl.BlockSpec((tm, D), lambda b, t: (b * nt + t, 0)), buf_spec],
        out_shape=[jax.ShapeDtypeStruct((R, D), F32), jax.ShapeDtypeStruct((n_seq, 2, C2), F32)],
        scratch_shapes=[pltpu.VMEM((tm, D), F32), pltpu.VMEM((2, C2), F32)],
        compiler_params=_cparams(2), name="conv_ffn",
    )(x, g_pre.reshape(1, D), w_up, conv_w, conv_b.reshape(1, C2), w_down, g_post.reshape(1, D), buf)


def _pad_rows(x, n):
    if x.shape[0] >= n:
        return x
    return jnp.concatenate([x, jnp.zeros((n - x.shape[0],) + x.shape[1:], x.dtype)], axis=0)


def _hgrn_kernel(q_ref, f_ref, i_ref, g_ref, lb_ref, ng_ref, s0_ref, o_ref, so_ref, st_scr, *, chunk):
    t = pl.program_id(2)
    tc = q_ref.shape[0]
    C = chunk
    cp = max(C, 16)

    @pl.when(t == 0)
    def _():
        st_scr[...] = s0_ref[0, 0].T

    lb = lb_ref[0]
    z = f_ref[...]
    logf = jnp.log(lb + (1.0 - lb) * jax.nn.sigmoid(z))
    k = (1.0 - lb) * jax.nn.sigmoid(-z)
    q = jax.nn.silu(q_ref[...])
    v = i_ref[...]

    tp = max(tc, 16)
    r = lax.broadcasted_iota(I32, (tp, tp), 0)
    c = lax.broadcasted_iota(I32, (tp, tp), 1)
    lmat = jnp.where((c <= r) & ((r // C) == (c // C)), 1.0, 0.0).astype(BF16)
    lf = _pad_rows(logf, tp)
    hi = lf.astype(BF16)
    r1 = lf - hi.astype(F32)
    mid = r1.astype(BF16)
    lo = (r1 - mid.astype(F32)).astype(BF16)
    b = (_dot(lmat, hi) + _dot(lmat, mid) + _dot(lmat, lo))[:tc]

    rr = lax.broadcasted_iota(I32, (cp, cp), 0)
    cc = lax.broadcasted_iota(I32, (cp, cp), 1)
    tril = cc <= rr
    st = st_scr[...]
    outs = []
    for ci in range(tc // C):
        sl = slice(ci * C, (ci + 1) * C)
        bc = b[sl]
        bl = bc[C - 1:C, :]
        qc, kc = q[sl], k[sl]
        q_in = _pad_rows(qc * jnp.exp(bc), cp).astype(BF16)
        k_out = _pad_rows(kc * jnp.exp(bl - bc), cp).astype(BF16)
        q_rel = _pad_rows(qc * jnp.exp(bc - bl), cp).astype(BF16)
        vb = _pad_rows(v[sl], cp).astype(BF16)
        attn = jnp.where(tril, _dot(q_rel, k_out, NT), 0.0)
        o_c = _dot(attn.astype(BF16), vb) + _dot(q_in, st.astype(BF16), NT)
        st = jnp.exp(bl) * st + _dot(vb, k_out, TN)
        outs.append(o_c[:C])
    o = outs[0] if len(outs) == 1 else jnp.concatenate(outs, axis=0)
    st_scr[...] = st
    o_ref[...] = _rms(o, ng_ref[...]) * jax.nn.silu(g_ref[...])

    @pl.when(t == pl.num_programs(2) - 1)
    def _():
        so_ref[0, 0] = st.T


def _hgrn(zh, lbs, norm_g, s0, B, T):
    C = math.gcd(T, HG_CHUNK)
    tc = min(T, 256)
    nt = T // tc
    sec = lambda s: pl.BlockSpec((tc, HG_DK), lambda b, h, t: (b * nt + t, s * HG_HEADS + h))
    st_spec = pl.BlockSpec((1, 1, HG_DK, HG_DK), lambda b, h, t: (b, h, 0, 0))
    return pl.pallas_call(
        functools.partial(_hgrn_kernel, chunk=C), grid=(B, HG_HEADS, nt),
        in_specs=[sec(0), sec(1), sec(2), sec(3),
                  pl.BlockSpec((1, 1, HG_DK), lambda b, h, t: (h, 0, 0)),
                  pl.BlockSpec((1, HG_DK), lambda b, h, t: (0, 0)), st_spec],
        out_specs=[pl.BlockSpec((tc, HG_DK), lambda b, h, t: (b * nt + t, h)), st_spec],
        out_shape=[jax.ShapeDtypeStruct((B * T, HG_HEADS * HG_DK), F32),
                   jax.ShapeDtypeStruct((B, HG_HEADS, HG_DK, HG_DK), F32)],
        scratch_shapes=[pltpu.VMEM((HG_DK, HG_DK), F32)],
        compiler_params=_cparams(3), name="hgrn2",
    )(zh, zh, zh, zh, lbs.reshape(HG_HEADS, 1, HG_DK), norm_g.reshape(1, HG_DK), s0)


def _topk_rows(score, k):
    n = score.shape[0]
    rid = lax.broadcasted_iota(I32, score.shape, 0)
    sel = jnp.zeros(score.shape, F32)
    s = score
    for _ in range(k):
        m = jnp.max(s, axis=0, keepdims=True)
        first = jnp.min(jnp.where(s == m, rid, n), axis=0, keepdims=True)
        hit = rid == first
        sel = jnp.where(hit, 1.0, sel)
        s = jnp.where(hit, -jnp.inf, s)
    return sel


def _softmax_rows(s, mask):
    s = jnp.where(mask, s, NEG)
    e = jnp.where(mask, jnp.exp(s - jnp.max(s, axis=0, keepdims=True)), 0.0)
    return e / jnp.maximum(jnp.sum(e, axis=0, keepdims=True), 1e-30)


def _online(carry, s, v_t):
    m, l, acc = carry
    m_new = jnp.maximum(m, jnp.max(s, axis=0, keepdims=True))
    alpha = jnp.exp(m - m_new)
    p = jnp.exp(s - m_new)
    l = alpha * l + jnp.sum(p, axis=0, keepdims=True)
    acc = alpha * acc + _dot(v_t, p.astype(BF16))
    return m_new, l, acc


def _block_means_even_odd(x):
    s = 1.0 / CMP_BLOCK
    return (x[:, 0:CMP_BLOCK, :].sum(axis=1) * s, x[:, CMP_BLOCK:SEL_BLOCK, :].sum(axis=1) * s)


def _cmp_prep_kernel(kvc_ref, vct_ref, km_ref, vmt_ref):
    hb = kvc_ref.shape[0]
    nb = 2 * hb
    T = hb * SEL_BLOCK
    ev, od = _block_means_even_odd(kvc_ref[:, :, 0:KV_W])
    km_ref[0:hb, :] = ev
    km_ref[hb:nb, :] = od
    ti = lax.broadcasted_iota(I32, (T, nb), 0)
    ci = lax.broadcasted_iota(I32, (T, nb), 1)
    blk = jnp.where(ci < hb, 2 * ci, 2 * (ci - hb) + 1)
    pool = jnp.where((ti // CMP_BLOCK) == blk, 1.0 / CMP_BLOCK, 0.0).astype(BF16)
    vmt_ref[...] = _dot(vct_ref[...], pool)


def _cmp_prep(kvc, vt, B, T):
    nb = T // CMP_BLOCK
    hb = T // SEL_BLOCK
    return pl.pallas_call(
        _cmp_prep_kernel, grid=(B,),
        in_specs=[pl.BlockSpec((hb, SEL_BLOCK, KV_ROW), lambda b: (b, 0, 0)), pl.BlockSpec((KV_W, T), lambda b: (0, b))],
        out_specs=[pl.BlockSpec((nb, KV_W), lambda b: (b, 0)), pl.BlockSpec((KV_W, nb), lambda b: (b, 0))],
        out_shape=[jax.ShapeDtypeStruct((B * nb, KV_W), F32), jax.ShapeDtypeStruct((B * KV_W, nb), F32)],
        compiler_params=_cparams(1), name="cmp_prep",
    )(kvc.reshape(B * hb, SEL_BLOCK, KV_ROW), vt)


def _nsa_prompt_kernel(qt_ref, km_ref, vmt_ref, ks_ref, vst_ref, kw_ref, vwt_ref, ngt_ref, o_ref, bias_scr,
                       *, tq):
    g = pl.program_id(1)
    qi = pl.program_id(2)
    par = g % 2
    L = GROUP * tq
    nblk = km_ref.shape[0]
    nsel = nblk // 2
    spt = tq // SEL_BLOCK

    qt = qt_ref[...]
    qs = jnp.concatenate([qt[r * NSA_HD:(r + 1) * NSA_HD, :] for r in range(GROUP)], axis=1)
    zq = jnp.zeros_like(qs)
    qp = jnp.concatenate([jnp.where(par == 0, qs, zq), jnp.where(par == 1, qs, zq)], axis=0)

    lane = lax.broadcasted_iota(I32, (1, L), 1)
    qpos = qi * tq + (lane % tq)

    row = lax.broadcasted_iota(I32, (nblk, 1), 0)
    cblk = jnp.where(row < nsel, 2 * row, 2 * (row - nsel) + 1)
    vis = ((cblk + 1) * CMP_BLOCK - 1) <= qpos
    p_c = _softmax_rows(_dot(km_ref[...].astype(BF16), qp), vis)
    o_c = _dot(vmt_ref[...].astype(BF16), p_c.astype(BF16))
    imp = p_c[:, 0:tq]
    for r in range(1, GROUP):
        imp = imp + p_c[:, r * tq:(r + 1) * tq]
    imp = imp[0:nsel] + imp[nsel:nblk]
    sblk = lax.broadcasted_iota(I32, (nsel, 1), 0)
    cur = qpos[:, 0:tq] // SEL_BLOCK
    forced = (sblk == 0) | (sblk == cur)
    score = jnp.where(sblk <= cur, imp + jnp.where(forced, FORCE_BONUS, 0.0), NEG)
    keep = (_topk_rows(score, min(SEL_TOPK, nsel)) > 0.5) & (score > NEG / 2)
    bias = jnp.where(keep, 0.0, NEG)
    bias = jnp.concatenate([bias] * GROUP, axis=1)
    for j in range(nsel):
        bias_scr[j] = bias[j:j + 1, :]

    init = (jnp.full((1, L), M_INIT, F32), jnp.zeros((1, L), F32), jnp.zeros((NSA_HD, L), F32))
    krow = lax.broadcasted_iota(I32, (tq, 1), 0)

    def tile(k_ref, vt_ref, kt):
        start = pl.multiple_of(kt * tq, tq)
        s = _dot(k_ref[pl.ds(start, tq), :].astype(BF16), qp)
        return s, vt_ref[:, pl.ds(start, tq)]

    def sel_body(kt, carry):
        s, v_t = tile(ks_ref, vst_ref, kt)
        b = bias_scr[pl.ds(kt * spt, spt)]
        s = (s.reshape(spt, SEL_BLOCK, L) + b).reshape(tq, L)
        return _online(carry, s, v_t)

    s, v_t = tile(ks_ref, vst_ref, qi)
    b = bias_scr[pl.ds(qi * spt, spt)]
    s = (s.reshape(spt, SEL_BLOCK, L) + b).reshape(tq, L)
    s = jnp.where((qi * tq + krow) <= qpos, s, NEG)
    m_s, l_s, a_s = lax.fori_loop(0, qi, sel_body, _online(init, s, v_t))
    o_s = a_s / jnp.maximum(l_s, 1e-30)

    def win_body(kt, carry):
        s, v_t = tile(kw_ref, vwt_ref, kt)
        d = qpos - (kt * tq + krow)
        return _online(carry, jnp.where((d >= 0) & (d < WINDOW), s, NEG), v_t)

    m_w, l_w, a_w = lax.fori_loop(jnp.maximum(qi - WINDOW // tq, 0), qi, win_body, win_body(qi, init))
    o_w = a_w / jnp.maximum(l_w, 1e-30)

    gts = jax.nn.sigmoid(ngt_ref[...])
    grow = lambda i: jnp.concatenate([gts[i * GROUP + r:i * GROUP + r + 1, :] for r in range(GROUP)], axis=1)
    o = grow(0) * o_c + grow(1) * o_s + grow(2) * o_w
    for pr in range(GROUP // 2):
        two = jnp.concatenate([o[:, (2 * pr) * tq:(2 * pr + 1) * tq], o[:, (2 * pr + 1) * tq:(2 * pr + 2) * tq]], axis=0)
        o_ref[:, pr * 128:(pr + 1) * 128] = two.T


def _nsa_prompt(qt, kmean, vmt, kvs, kvw, vt, ngt, B, T, tq=256):
    nq = T // tq
    nblk = T // CMP_BLOCK
    nsel = nblk // 2
    L = GROUP * tq
    GW = GROUP * NSA_HD
    vrow = lambda base: pl.BlockSpec((NSA_HD, T), lambda b, g, q: (base + g, b))
    kslab = pl.BlockSpec((T, 128), lambda b, g, q: (b, g // 2))
    return pl.pallas_call(
        functools.partial(_nsa_prompt_kernel, tq=tq), grid=(B, KVH, nq),
        in_specs=[pl.BlockSpec((GW, tq), lambda b, g, q: (g, b * nq + q)),
                  pl.BlockSpec((nblk, 128), lambda b, g, q: (b, g // 2)),
                  pl.BlockSpec((NSA_HD, nblk), lambda b, g, q: (b * KVH + g, 0)),
                  kslab, vrow(KVH), kslab, vrow(2 * KVH),
                  pl.BlockSpec((16, tq), lambda b, g, q: (g, b * nq + q))],
        out_specs=pl.BlockSpec((tq, GW), lambda b, g, q: (b * nq + q, g)),
        out_shape=jax.ShapeDtypeStruct((B * T, NSA_HEADS * NSA_HD), F32),
        scratch_shapes=[pltpu.VMEM((nsel, 1, L), F32)],
        compiler_params=_cparams(3), name="nsa_prompt",
    )(qt, kmean, vmt, kvs, vt, kvw, vt, ngt)


def _q_rows(qx_ref):
    T = qx_ref.shape[0]
    parts = [qx_ref[:, (g * GROUP + r) * KV_W:(g * GROUP + r + 1) * KV_W] for r in range(GROUP) for g in range(KVH)]
    return jnp.concatenate(parts, axis=0).astype(BF16)


def _heads_to_row(o, T):
    parts = []
    for g in range(KVH):
        for r in range(GROUP):
            r0 = (r * KVH + g) * T
            parts.append(o[r0:r0 + T, g * NSA_HD:(g + 1) * NSA_HD])
    return jnp.concatenate(parts, axis=1)


def _s_cmp_kernel(pt_ref, *refs, n_pg, past):
    pages = refs[:n_pg]
    qx_ref, oc_ref, bias_ref, km_scr = refs[n_pg:]
    c = pl.program_id(1)
    T = qx_ref.shape[0]
    spp = PAGE // SEL_BLOCK
    npast = km_scr.shape[0] // 2
    for i in range(0, n_pg, 8 // spp):
        rows = jnp.concatenate([pages[i + k][0] for k in range(8 // spp)], axis=0)
        ev, od = _block_means_even_odd(rows.reshape(8, SEL_BLOCK, KV_ROW))
        at = pl.multiple_of((c * n_pg + i) * spp, 8)
        km_scr[pl.ds(at, 8), :] = ev
        km_scr[pl.ds(npast + at, 8), :] = od

    @pl.when(c == pl.num_programs(1) - 1)
    def _():
        nblk = km_scr.shape[0]
        L = NSA_HEADS * T
        qr = _q_rows(qx_ref)
        lane = lax.broadcasted_iota(I32, (1, L), 1)
        qpos = past + (lane % T)
        row = lax.broadcasted_iota(I32, (nblk, 1), 0)
        cblk = jnp.where(row < npast, 2 * row, 2 * (row - npast) + 1)
        vis = ((cblk + 1) * CMP_BLOCK - 1) <= qpos
        p = _softmax_rows(_dot(km_scr[:, 0:KV_W].astype(BF16), qr, NT), vis)
        o_c = _dot(p.astype(BF16), km_scr[:, KV_W:KV_ROW].astype(BF16), TN)
        oc_ref[...] = _heads_to_row(o_c, T)
        gl = KVH * T
        imp = p + pltpu.roll(p, gl, 1) + pltpu.roll(p, 2 * gl, 1) + pltpu.roll(p, 3 * gl, 1)
        imp = imp[0:npast] + imp[npast:nblk]
        nrow = bias_ref.shape[1]
        imp = jnp.concatenate([imp, jnp.zeros((nrow - npast, L), F32)], axis=0)
        sblk = lax.broadcasted_iota(I32, (nrow, 1), 0)
        cur = qpos // SEL_BLOCK
        forced = (sblk == 0) | (sblk == cur)
        score = jnp.where(sblk <= cur, imp + jnp.where(forced, FORCE_BONUS, 0.0), NEG)
        keep = (_topk_rows(score, SEL_TOPK) > 0.5) & (score > NEG / 2)
        bias = jnp.where(keep & (lane < gl), 0.0, NEG)
        bias = jnp.where(lane < gl, bias, 0.0)
        bias = bias + pltpu.roll(bias, gl, 1) + pltpu.roll(bias, 2 * gl, 1) + pltpu.roll(bias, 3 * gl, 1)
        bias_ref[0] = bias


def _s_cmp(page_table, pool_c, qx, DB, T, n_pg=16):
    n_pages = page_table.shape[1]
    past = n_pages * PAGE
    nblk = past // CMP_BLOCK
    nrow = past // SEL_BLOCK + 8
    n_pg = min(n_pg, n_pages)
    nch = n_pages // n_pg
    page_spec = lambda i: pl.BlockSpec((1, PAGE, KV_ROW), lambda b, c, pt: (pt[b, c * n_pg + i], 0, 0))
    gs = pltpu.PrefetchScalarGridSpec(
        num_scalar_prefetch=1, grid=(DB, nch),
        in_specs=[page_spec(i) for i in range(n_pg)] + [pl.BlockSpec((T, NSA_HEADS * KV_W), lambda b, c, pt: (b, 0))],
        out_specs=[pl.BlockSpec((T, NSA_HEADS * NSA_HD), lambda b, c, pt: (b, 0)),
                   pl.BlockSpec((1, nrow, NSA_HEADS * T), lambda b, c, pt: (b, 0, 0))],
        scratch_shapes=[pltpu.VMEM((nblk, KV_ROW), F32)])
    return pl.pallas_call(
        functools.partial(_s_cmp_kernel, n_pg=n_pg, past=past), grid_spec=gs,
        out_shape=[jax.ShapeDtypeStruct((DB * T, NSA_HEADS * NSA_HD), F32),
                   jax.ShapeDtypeStruct((DB, nrow, NSA_HEADS * T), F32)],
        compiler_params=_cparams(2), name="sample_cmp",
    )(page_table, *([pool_c] * n_pg), qx)


def _s_attn_kernel(pt_ref, *refs, n_pg, past):
    pages = refs[:n_pg]
    (qx_ref, bias_ref, win_ref, ksn_ref, kwn_ref, oc_ref, gr_ref,
     o_ref, wout_ref, m_scr, l_scr, acc_scr) = refs[n_pg:]
    c = pl.program_id(1)
    T = qx_ref.shape[0]
    L = NSA_HEADS * T
    spp = PAGE // SEL_BLOCK
    nb = n_pg * spp
    qr = _q_rows(qx_ref)

    @pl.when(c == 0)
    def _():
        m_scr[...] = jnp.full(m_scr.shape, M_INIT, F32)
        l_scr[...] = jnp.zeros_like(l_scr)
        acc_scr[...] = jnp.zeros_like(acc_scr)
        wb = win_ref.shape[1]
        wout_ref[0, 0:wb - T, :] = win_ref[0, T:wb, :]
        wout_ref[0, wb - T:wb, :] = kwn_ref[...]

    def kv_t(rows_k, rows_v, s_mask):
        s = _dot(rows_k.astype(BF16), qr, NT)
        return s if s_mask is None else jnp.where(s_mask, s, NEG), rows_v.astype(BF16)

    def upd(carry, s, v):
        m, l, acc = carry
        m_new = jnp.maximum(m, jnp.max(s, axis=0, keepdims=True))
        alpha = jnp.exp(m - m_new)
        p = jnp.exp(s - m_new)
        return m_new, alpha * l + jnp.sum(p, axis=0, keepdims=True), alpha * acc + _dot(v, p.astype(BF16), TN)

    kv = jnp.concatenate([pg[0] for pg in pages], axis=0)
    s, v = kv_t(kv[:, 0:KV_W], kv[:, KV_W:KV_ROW], None)
    brow = bias_ref[0, pl.ds(pl.multiple_of(c * nb, 8), nb), :]
    s = jnp.concatenate([s[j * SEL_BLOCK:(j + 1) * SEL_BLOCK, :] + brow[j:j + 1, :] for j in range(nb)], axis=0)
    m, l, acc = upd((m_scr[...], l_scr[...], acc_scr[...]), s, v)
    m_scr[...] = m
    l_scr[...] = l
    acc_scr[...] = acc

    @pl.when(c == pl.num_programs(1) - 1)
    def _():
        npast = past // SEL_BLOCK
        lane = lax.broadcasted_iota(I32, (1, L), 1)
        tq = lane % T
        pad = 16
        jrow = lax.broadcasted_iota(I32, (pad, 1), 0)
        new_ok = (jrow < T) & (jrow <= tq)
        zpad = jnp.zeros((pad - T, KV_ROW), F32)
        ksn = jnp.concatenate([ksn_ref[...], zpad], axis=0)
        s, v = kv_t(ksn[:, 0:KV_W], ksn[:, KV_W:KV_ROW], new_ok)
        s = s + bias_ref[0, npast:npast + 1, :]
        m_s, l_s, a_s = upd((m_scr[...], l_scr[...], acc_scr[...]), s, v)
        o_s = (a_s / jnp.maximum(l_s, 1e-30)).T
        wb = win_ref.shape[1]
        win = win_ref[0]
        irow = lax.broadcasted_iota(I32, (wb, 1), 0)
        d = (past + tq) - (past - wb + irow)
        init = (jnp.full((1, L), M_INIT, F32), jnp.zeros((1, L), F32), jnp.zeros((KV_W, L), F32))
        kwn = jnp.concatenate([kwn_ref[...], zpad], axis=0)
        s, v = kv_t(kwn[:, 0:KV_W], kwn[:, KV_W:KV_ROW], new_ok)
        carry = upd(init, s, v)
        s, v = kv_t(win[:, 0:KV_W], win[:, KV_W:KV_ROW], (d >= 0) & (d < WINDOW))
        m_w, l_w, a_w = upd(carry, s, v)
        o_w = (a_w / jnp.maximum(l_w, 1e-30)).T
        W = NSA_HEADS * NSA_HD
        gts = jax.nn.sigmoid(gr_ref[...])
        o_ref[...] = (gts[:, 0:W] * oc_ref[...] + gts[:, W:2 * W] * _heads_to_row(o_s, T)
                      + gts[:, 2 * W:3 * W] * _heads_to_row(o_w, T))


def _s_attn(page_table, pool_s, qx, bias, win, kvs_new, kvw_new, o_c, g_rep, DB, T, n_pg=16):
    n_pages = page_table.shape[1]
    past = n_pages * PAGE
    n_pg = min(n_pg, n_pages)
    nch = n_pages // n_pg
    wb = win.shape[1]
    L = NSA_HEADS * T
    W = NSA_HEADS * NSA_HD
    page_spec = lambda i: pl.BlockSpec((1, PAGE, KV_ROW), lambda b, c, pt: (pt[b, c * n_pg + i], 0, 0))
    rowb = lambda w: pl.BlockSpec((T, w), lambda b, c, pt: (b, 0))
    full3 = lambda a: pl.BlockSpec((1,) + a.shape[1:], lambda b, c, pt: (b, 0, 0))
    gs = pltpu.PrefetchScalarGridSpec(
        num_scalar_prefetch=1, grid=(DB, nch),
        in_specs=[page_spec(i) for i in range(n_pg)] + [
            rowb(NSA_HEADS * KV_W), full3(bias), full3(win), rowb(KV_ROW), rowb(KV_ROW), rowb(W), rowb(3 * W)],
        out_specs=[rowb(W), full3(win)],
        scratch_shapes=[pltpu.VMEM((1, L), F32), pltpu.VMEM((1, L), F32), pltpu.VMEM((KV_W, L), F32)])
    return pl.pallas_call(
        functools.partial(_s_attn_kernel, n_pg=n_pg, past=past), grid_spec=gs,
        out_shape=[jax.ShapeDtypeStruct((DB * T, W), F32), jax.ShapeDtypeStruct(win.shape, F32)],
        compiler_params=_cparams(2), name="sample_attn",
    )(page_table, *([pool_s] * n_pg), qx, bias, win, kvs_new, kvw_new, o_c, g_rep)


def _prep_weights(w_in):
    o_q = 4 * HG_HEADS * HG_DK
    o_kv = o_q + NSA_HEADS * NSA_HD
    o_ng = o_kv + 6 * KV_W
    o_mg = o_ng + NSA_HEADS * 3
    w_hg = w_in[:, :o_q].astype(BF16)
    w_q = w_in[:, o_q:o_kv] * ATT_SCALE
    w_kv = w_in[:, o_kv:o_ng].astype(BF16)
    w_ng = w_in[:, o_ng:o_mg]
    w_mg = w_in[:, o_mg:].astype(BF16)
    ng4 = w_ng.reshape(D_MODEL, KVH, GROUP, 3).transpose(0, 1, 3, 2)
    ng4 = jnp.pad(ng4.reshape(D_MODEL, KVH, 3 * GROUP), ((0, 0), (0, 0), (0, 16 - 3 * GROUP)))
    vcols = jnp.concatenate([w_in[:, o_kv + (2 * i + 1) * KV_W:o_kv + (2 * i + 2) * KV_W] for i in range(3)], axis=1)
    w_t = jnp.concatenate([w_q, vcols], axis=1).T.astype(BF16)
    w_ngt = ng4.reshape(D_MODEL, KVH * 16).T.astype(BF16)
    wq4 = w_q.reshape(D_MODEL, KVH, GROUP, NSA_HD)
    eye = jnp.eye(KVH, dtype=w_q.dtype)
    w_qx = jnp.einsum('dgrh,gk->dgrkh', wq4, eye).reshape(D_MODEL, NSA_HEADS * KV_W).astype(BF16)
    w_gr = jnp.repeat(w_ng.reshape(D_MODEL, NSA_HEADS, 3).transpose(0, 2, 1), NSA_HD, axis=2)
    w_gr = w_gr.reshape(D_MODEL, 3 * NSA_HEADS * NSA_HD).astype(BF16)
    return w_hg, w_kv, w_mg, w_t, w_ngt, w_qx, w_gr


def kernel(x_prompt, x_sample, cache_cmp_kv, cache_sel_kv, page_table, state_win_kv, state_hgrn, state_conv, w_in, hg_lb_logits, hg_norm_g, w_out, w_up, conv_w, conv_b, w_down, g_pre_mix, g_post_mix, g_pre_ffn, g_post_ffn):
    B, T, D = x_prompt.shape
    DB, TS, _ = x_sample.shape
    depth = w_in.shape[0]
    assert depth == 1 and D == D_MODEL and T % 256 == 0 and TS == 8 and TS < CMP_BLOCK
    n_pool = cache_cmp_kv.shape[1]
    wb = state_win_kv.shape[2]
    C2 = 2 * D_FF
    l = 0

    lbs = jnp.cumsum(jax.nn.softmax(hg_lb_logits.astype(F32), axis=0), axis=0)[l]
    w_hg, w_kv, w_mg, w_t, w_ngt, w_qx, w_gr = _prep_weights(w_in[l])
    w_out_b, w_up_b, w_dn_b = w_out[l].astype(BF16), w_up[l].astype(BF16), w_down[l].astype(BF16)

    outs = {}
    for name, x, nb, tl in (("p", x_prompt, B, T), ("s", x_sample, DB, TS)):
        R = nb * tl
        x2 = x.reshape(R, D)
        h = _rmsnorm_bf16(x2, g_pre_mix[l])
        zh = _mm(h, w_hg, F32, name="proj_hgrn")
        kv = [_mm(h, w_kv[:, i * KV_ROW:(i + 1) * KV_ROW], F32, name="proj_kv") for i in range(3)]
        mg = _mm(h, w_mg, F32, name="proj_merge_gate")
        s0 = jnp.zeros((nb, HG_HEADS, HG_DK, HG_DK), F32) if name == "p" else state_hgrn[l]
        o_hg, s_hg = _hgrn(zh, lbs, hg_norm_g[l], s0, nb, tl)
        if name == "p":
            qvt = _mm_nt(w_t, h, BF16, name="proj_qv_t")
            ngt = _mm_nt(w_ngt, h, F32, tn=64, name="proj_gate_t")
            kmean, vmt = _cmp_prep(kv[0], qvt[NSA_HEADS * NSA_HD:], nb, tl)
            o_nsa = _nsa_prompt(qvt, kmean, vmt, kv[1], kv[2], qvt[NSA_HEADS * NSA_HD:], ngt, nb, tl)
            win_new = kv[2].reshape(nb, tl, KV_ROW)[:, tl - min(WINDOW, tl):]
            cbuf0 = jnp.zeros((nb, CONV_W - 1, C2), F32)
        else:
            qx = _mm(h, w_qx, F32, name="proj_q_pad")
            g_rep = _mm(h, w_gr, F32, name="proj_gate_rep")
            pool_c = cache_cmp_kv[l].reshape(n_pool, PAGE, KV_ROW)
            pool_s = cache_sel_kv[l].reshape(n_pool, PAGE, KV_ROW)
            o_c, bias = _s_cmp(page_table, pool_c, qx, nb, tl)
            o_nsa, win_new = _s_attn(page_table, pool_s, qx, bias, state_win_kv[l].reshape(nb, wb, KV_ROW),
                                     kv[1], kv[2], o_c, g_rep, nb, tl)
            cbuf0 = state_conv[l]
        x1 = _merge(mg, o_hg, o_nsa, x2, w_out_b, g_post_mix[l])
        x3, cbuf = _ffn(x1, nb, g_pre_ffn[l], w_up_b, conv_w[l], conv_b[l], w_dn_b, g_post_ffn[l], cbuf0, tm=512)
        kv6 = lambda a, n: a.reshape(1, nb, n, 2, KVH, NSA_HD)
        outs[name] = (x3.reshape(nb, tl, D), kv6(kv[0], tl), kv6(kv[1], tl), kv6(win_new, win_new.shape[1]),
                      s_hg[None], cbuf[None])
    p, s = outs["p"], outs["s"]
    return (p[0], s[0], p[1], s[1], p[2], s[2], p[3], s[3], p[4], s[4], p[5], s[5])
```

```python
import functools
import math

import jax
import jax.numpy as jnp
from jax import lax
from jax.experimental import pallas as pl
from jax.experimental.pallas import tpu as pltpu

F32 = jnp.float32
BF16 = jnp.bfloat16
I32 = jnp.int32

D_MODEL = 1024
HG_HEADS = 8
HG_DK = 128
HG_CHUNK = 32
NSA_HEADS = 16
NSA_HD = 64
KVH = 4
GROUP = 4
KV_W = KVH * NSA_HD
KV_ROW = 2 * KV_W
CMP_BLOCK = 32
SEL_BLOCK = 64
SEL_TOPK = 8
WINDOW = 512
FORCE_BONUS = 1.0e4
ATT_SCALE = NSA_HD ** -0.5
D_FF = 2816
CONV_W = 3
PAGE = 128
EPS = 1e-6
NEG = -1e30
M_INIT = -1e38

VMEM_LIMIT = 56 * 1024 * 1024

NT = (((1,), (1,)), ((), ()))
TN = (((0,), (0,)), ((), ()))


def _cparams(n_axes):
    return pltpu.CompilerParams(dimension_semantics=("arbitrary",) * n_axes,
                                vmem_limit_bytes=VMEM_LIMIT)


def _dot(a, b, dims=None):
    if dims is None:
        return jnp.dot(a, b, preferred_element_type=F32)
    return lax.dot_general(a, b, dims, preferred_element_type=F32)


def _rms(x, g):
    return x * lax.rsqrt(jnp.mean(x * x, axis=-1, keepdims=True) + EPS) * g


def _rmsnorm_kernel(x_ref, g_ref, o_ref):
    o_ref[...] = _rms(x_ref[...], g_ref[...]).astype(o_ref.dtype)


def _rmsnorm_bf16(x, g, tm=512):
    R, D = x.shape
    tm = min(tm, R)
    return pl.pallas_call(
        _rmsnorm_kernel, grid=(R // tm,),
        in_specs=[pl.BlockSpec((tm, D), lambda i: (i, 0)), pl.BlockSpec((1, D), lambda i: (0, 0))],
        out_specs=pl.BlockSpec((tm, D), lambda i: (i, 0)),
        out_shape=jax.ShapeDtypeStruct((R, D), BF16),
        compiler_params=_cparams(1), name="rmsnorm",
    )(x, g.reshape(1, D))


def _mm_kernel(a_ref, w_ref, o_ref):
    o_ref[...] = _dot(a_ref[...], w_ref[...]).astype(o_ref.dtype)


def _mm(a, w, out_dtype, tm=1024, tn=512, name="proj"):
    R, K = a.shape
    N = w.shape[1]
    tm, tn = min(tm, R), min(tn, N)
    return pl.pallas_call(
        _mm_kernel, grid=(R // tm, N // tn),
        in_specs=[pl.BlockSpec((tm, K), lambda i, j: (i, 0)), pl.BlockSpec((K, tn), lambda i, j: (0, j))],
        out_specs=pl.BlockSpec((tm, tn), lambda i, j: (i, j)),
        out_shape=jax.ShapeDtypeStruct((R, N), out_dtype),
        compiler_params=_cparams(2), name=name,
    )(a, w)


def _mm_nt_kernel(wt_ref, a_ref, o_ref):
    o_ref[...] = _dot(wt_ref[...], a_ref[...], NT).astype(o_ref.dtype)


def _mm_nt(wt, a, out_dtype, tm=1024, tn=256, name="proj_t"):
    N, K = wt.shape
    R = a.shape[0]
    tm, tn = min(tm, R), min(tn, N)
    return pl.pallas_call(
        _mm_nt_kernel, grid=(R // tm, N // tn),
        in_specs=[pl.BlockSpec((tn, K), lambda i, j: (j, 0)), pl.BlockSpec((tm, K), lambda i, j: (i, 0))],
        out_specs=pl.BlockSpec((tn, tm), lambda i, j: (j, i)),
        out_shape=jax.ShapeDtypeStruct((N, R), out_dtype),
        compiler_params=_cparams(2), name=name,
    )(wt, a)


def _merge_kernel(mg_ref, ohg_ref, onsa_ref, x_ref, w_ref, g_ref, o_ref):
    mg = mg_ref[...]
    u = jax.nn.sigmoid(mg[:, :D_MODEL]) * ohg_ref[...] + jax.nn.sigmoid(mg[:, D_MODEL:]) * onsa_ref[...]
    y = _dot(u.astype(BF16), w_ref[...])
    o_ref[...] = x_ref[...] + _rms(y, g_ref[...])


def _merge(mg, ohg, onsa, x, w_out, g_post, tm=512):
    R, D = x.shape
    tm = min(tm, R)
    row = lambda w: pl.BlockSpec((tm, w), lambda i: (i, 0))
    return pl.pallas_call(
        _merge_kernel, grid=(R // tm,),
        in_specs=[row(2 * D), row(D), row(D), row(D),
                  pl.BlockSpec((D, D), lambda i: (0, 0)), pl.BlockSpec((1, D), lambda i: (0, 0))],
        out_specs=row(D),
        out_shape=jax.ShapeDtypeStruct((R, D), F32),
        compiler_params=_cparams(1), name="merge",
    )(mg, ohg, onsa, x, w_out, g_post.reshape(1, D))


def _ffn_kernel(x_ref, gpre_ref, wup_ref, cw_ref, cb_ref, wdn_ref, gpost_ref, buf_ref,
                o_ref, cbuf_ref, acc_scr, carry_scr, *, ns, tn):
    tm = x_ref.shape[0]
    ls = tm // ns
    x = x_ref[...]
    h = _rms(x, gpre_ref[...]).astype(BF16)
    pos = lax.broadcasted_iota(I32, (tm, 1), 0) % ls
    if ns == 1:
        @pl.when(pl.program_id(1) == 0)
        def _():
            carry_scr[...] = buf_ref[0]
    acc_scr[...] = jnp.zeros_like(acc_scr)
    for j in range(D_FF // tn):
        halves = []
        for base in (j * tn, D_FF + j * tn):
            cols = slice(base, base + tn)
            up = _dot(h, wup_ref[:, cols])
            if ns == 1:
                b0 = carry_scr[0:1, cols]
                b1 = carry_scr[1:2, cols]
                carry_scr[:, cols] = up[tm - 2:tm, :]
                cbuf_ref[0, :, cols] = up[tm - 2:tm, :]
            else:
                bufv = buf_ref[:, :, cols]
                b0 = jnp.broadcast_to(bufv[:, 0:1, :], (ns, ls, tn)).reshape(tm, tn)
                b1 = jnp.broadcast_to(bufv[:, 1:2, :], (ns, ls, tn)).reshape(tm, tn)
                cbuf_ref[:, :, cols] = up.reshape(ns, ls, tn)[:, ls - 2:ls, :]
            prev1 = jnp.where(pos == 0, b1, pltpu.roll(up, 1, 0))
            prev2 = jnp.where(pos == 0, b0, jnp.where(pos == 1, b1, pltpu.roll(up, 2, 0)))
            c = cb_ref[:, cols] + prev2 * cw_ref[0:1, cols]
            c = c + prev1 * cw_ref[1:2, cols]
            c = c + up * cw_ref[2:3, cols]
            halves.append(c)
        act = (jax.nn.silu(halves[0]) * halves[1]).astype(BF16)
        acc_scr[...] += _dot(act, wdn_ref[j * tn:(j + 1) * tn, :])
    o_ref[...] = x + _rms(acc_scr[...], gpost_ref[...])


def _ffn(x, n_seq, g_pre, w_up, conv_w, conv_b, w_down, g_post, buf, tm):
    R, D = x.shape
    L = R // n_seq
    C2 = 2 * D_FF
    if L >= tm:
        ns, nt = 1, L // tm
        grid = (n_seq, nt)
        buf_spec = pl.BlockSpec((1, 2, C2), lambda b, t: (b, 0, 0))
    else:
        ns, nt, tm = n_seq, 1, R
        grid = (1, 1)
        buf_spec = pl.BlockSpec((ns, 2, C2), lambda b, t: (0, 0, 0))
    const = lambda shape: pl.BlockSpec(shape, lambda b, t: (0,) * len(shape), pipeline_mode=pl.Buffered(1))
    return pl.pallas_call(
        functools.partial(_ffn_kernel, ns=ns, tn=256), grid=grid,
        in_specs=[pl.BlockSpec((tm, D), lambda b, t: (b * nt + t, 0)), const((1, D)), const((D, C2)),
                  const((CONV_W, C2)), const((1, C2)), const((D_FF, D)), const((1, D)), buf_spec],
        out_specs=[pl.BlockSpec((tm, D), lambda b, t: (b * nt + t, 0)), buf_spec],
        out_shape=[jax.ShapeDtypeStruct((R, D), F32), jax.ShapeDtypeStruct((n_seq, 2, C2), F32)],
        scratch_shapes=[pltpu.VMEM((tm, D), F32), pltpu.VMEM((2, C2), F32)],
        compiler_params=_cparams(2), name="conv_ffn",
    )(x, g_pre.reshape(1, D), w_up, conv_w, conv_b.reshape(1, C2), w_down, g_post.reshape(1, D), buf)


def _pad_rows(x, n):
    if x.shape[0] >= n:
        return x
    return jnp.concatenate([x, jnp.zeros((n - x.shape[0],) + x.shape[1:], x.dtype)], axis=0)


def _hgrn_kernel(q_ref, f_ref, i_ref, g_ref, lb_ref, ng_ref, s0_ref, o_ref, so_ref, st_scr, *, chunk):
    t = pl.program_id(2)
    tc = q_ref.shape[0]
    C = chunk
    cp = max(C, 16)

    @pl.when(t == 0)
    def _():
        st_scr[...] = s0_ref[0, 0].T

    lb = lb_ref[0]
    z = f_ref[...]
    logf = jnp.log(lb + (1.0 - lb) * jax.nn.sigmoid(z))
    k = (1.0 - lb) * jax.nn.sigmoid(-z)
    q = jax.nn.silu(q_ref[...])
    v = i_ref[...]

    tp = max(tc, 16)
    r = lax.broadcasted_iota(I32, (tp, tp), 0)
    c = lax.broadcasted_iota(I32, (tp, tp), 1)
    lmat = jnp.where((c <= r) & ((r // C) == (c // C)), 1.0, 0.0).astype(BF16)
    lf = _pad_rows(logf, tp)
    hi = lf.astype(BF16)
    r1 = lf - hi.astype(F32)
    mid = r1.astype(BF16)
    lo = (r1 - mid.astype(F32)).astype(BF16)
    b = (_dot(lmat, hi) + _dot(lmat, mid) + _dot(lmat, lo))[:tc]

    rr = lax.broadcasted_iota(I32, (cp, cp), 0)
    cc = lax.broadcasted_iota(I32, (cp, cp), 1)
    tril = cc <= rr
    st = st_scr[...]
    outs = []
    for ci in range(tc // C):
        sl = slice(ci * C, (ci + 1) * C)
        bc = b[sl]
        bl = bc[C - 1:C, :]
        qc, kc = q[sl], k[sl]
        q_in = _pad_rows(qc * jnp.exp(bc), cp).astype(BF16)
        k_out = _pad_rows(kc * jnp.exp(bl - bc), cp).astype(BF16)
        q_rel = _pad_rows(qc * jnp.exp(bc - bl), cp).astype(BF16)
        vb = _pad_rows(v[sl], cp).astype(BF16)
        attn = jnp.where(tril, _dot(q_rel, k_out, NT), 0.0)
        o_c = _dot(attn.astype(BF16), vb) + _dot(q_in, st.astype(BF16), NT)
        st = jnp.exp(bl) * st + _dot(vb, k_out, TN)
        outs.append(o_c[:C])
    o = outs[0] if len(outs) == 1 else jnp.concatenate(outs, axis=0)
    st_scr[...] = st
    o_ref[...] = _rms(o, ng_ref[...]) * jax.nn.silu(g_ref[...])

    @pl.when(t == pl.num_programs(2) - 1)
    def _():
        so_ref[0, 0] = st.T


def _hgrn(zh, lbs, norm_g, s0, B, T):
    C = math.gcd(T, HG_CHUNK)
    tc = min(T, 256)
    nt = T // tc
    sec = lambda s: pl.BlockSpec((tc, HG_DK), lambda b, h, t: (b * nt + t, s * HG_HEADS + h))
    st_spec = pl.BlockSpec((1, 1, HG_DK, HG_DK), lambda b, h, t: (b, h, 0, 0))
    return pl.pallas_call(
        functools.partial(_hgrn_kernel, chunk=C), grid=(B, HG_HEADS, nt),
        in_specs=[sec(0), sec(1), sec(2), sec(3),
                  pl.BlockSpec((1, 1, HG_DK), lambda b, h, t: (h, 0, 0)),
                  pl.BlockSpec((1, HG_DK), lambda b, h, t: (0, 0)), st_spec],
        out_specs=[pl.BlockSpec((tc, HG_DK), lambda b, h, t: (b * nt + t, h)), st_spec],
        out_shape=[jax.ShapeDtypeStruct((B * T, HG_HEADS * HG_DK), F32),
                   jax.ShapeDtypeStruct((B, HG_HEADS, HG_DK, HG_DK), F32)],
        scratch_shapes=[pltpu.VMEM((HG_DK, HG_DK), F32)],
        compiler_params=_cparams(3), name="hgrn2",
    )(zh, zh, zh, zh, lbs.reshape(HG_HEADS, 1, HG_DK), norm_g.reshape(1, HG_DK), s0)


def _topk_rows(score, k):
    n = score.shape[0]
    rid = lax.broadcasted_iota(I32, score.shape, 0)
    sel = jnp.zeros(score.shape, F32)
    s = score
    for _ in range(k):
        m = jnp.max(s, axis=0, keepdims=True)
        first = jnp.min(jnp.where(s == m, rid, n), axis=0, keepdims=True)
        hit = rid == first
        sel = jnp.where(hit, 1.0, sel)
        s = jnp.where(hit, -jnp.inf, s)
    return sel


def _softmax_rows(s, mask):
    s = jnp.where(mask, s, NEG)
    e = jnp.where(mask, jnp.exp(s - jnp.max(s, axis=0, keepdims=True)), 0.0)
    return e / jnp.maximum(jnp.sum(e, axis=0, keepdims=True), 1e-30)


def _online(carry, s, v_t):
    m, l, acc = carry
    m_new = jnp.maximum(m, jnp.max(s, axis=0, keepdims=True))
    alpha = jnp.exp(m - m_new)
    p = jnp.exp(s - m_new)
    l = alpha * l + jnp.sum(p, axis=0, keepdims=True)
    acc = alpha * acc + _dot(v_t, p.astype(BF16))
    return m_new, l, acc


def _block_means_even_odd(x):
    s = 1.0 / CMP_BLOCK
    return (x[:, 0:CMP_BLOCK, :].sum(axis=1) * s, x[:, CMP_BLOCK:SEL_BLOCK, :].sum(axis=1) * s)


def _cmp_prep_kernel(kvc_ref, vct_ref, km_ref, vmt_ref):
    hb = kvc_ref.shape[0]
    nb = 2 * hb
    T = hb * SEL_BLOCK
    ev, od = _block_means_even_odd(kvc_ref[:, :, 0:KV_W])
    km_ref[0:hb, :] = ev
    km_ref[hb:nb, :] = od
    ti = lax.broadcasted_iota(I32, (T, nb), 0)
    ci = lax.broadcasted_iota(I32, (T, nb), 1)
    blk = jnp.where(ci < hb, 2 * ci, 2 * (ci - hb) + 1)
    pool = jnp.where((ti // CMP_BLOCK) == blk, 1.0 / CMP_BLOCK, 0.0).astype(BF16)
    vmt_ref[...] = _dot(vct_ref[...], pool)


def _cmp_prep(kvc, vt, B, T):
    nb = T // CMP_BLOCK
    hb = T // SEL_BLOCK
    return pl.pallas_call(
        _cmp_prep_kernel, grid=(B,),
        in_specs=[pl.BlockSpec((hb, SEL_BLOCK, KV_ROW), lambda b: (b, 0, 0)), pl.BlockSpec((KV_W, T), lambda b: (0, b))],
        out_specs=[pl.BlockSpec((nb, KV_W), lambda b: (b, 0)), pl.BlockSpec((KV_W, nb), lambda b: (b, 0))],
        out_shape=[jax.ShapeDtypeStruct((B * nb, KV_W), F32), jax.ShapeDtypeStruct((B * KV_W, nb), F32)],
        compiler_params=_cparams(1), name="cmp_prep",
    )(kvc.reshape(B * hb, SEL_BLOCK, KV_ROW), vt)


def _nsa_prompt_kernel(qt_ref, km_ref, vmt_ref, ks_ref, vst_ref, kw_ref, vwt_ref, ngt_ref, o_ref, bias_scr,
                       *, tq):
    g = pl.program_id(1)
    qi = pl.program_id(2)
    par = g % 2
    L = GROUP * tq
    nblk = km_ref.shape[0]
    nsel = nblk // 2
    spt = tq // SEL_BLOCK

    qt = qt_ref[...]
    qs = jnp.concatenate([qt[r * NSA_HD:(r + 1) * NSA_HD, :] for r in range(GROUP)], axis=1)
    zq = jnp.zeros_like(qs)
    qp = jnp.concatenate([jnp.where(par == 0, qs, zq), jnp.where(par == 1, qs, zq)], axis=0)

    lane = lax.broadcasted_iota(I32, (1, L), 1)
    qpos = qi * tq + (lane % tq)

    row = lax.broadcasted_iota(I32, (nblk, 1), 0)
    cblk = jnp.where(row < nsel, 2 * row, 2 * (row - nsel) + 1)
    vis = ((cblk + 1) * CMP_BLOCK - 1) <= qpos
    p_c = _softmax_rows(_dot(km_ref[...].astype(BF16), qp), vis)
    o_c = _dot(vmt_ref[...].astype(BF16), p_c.astype(BF16))
    imp = p_c[:, 0:tq]
    for r in range(1, GROUP):
        imp = imp + p_c[:, r * tq:(r + 1) * tq]
    imp = imp[0:nsel] + imp[nsel:nblk]
    sblk = lax.broadcasted_iota(I32, (nsel, 1), 0)
    cur = qpos[:, 0:tq] // SEL_BLOCK
    forced = (sblk == 0) | (sblk == cur)
    score = jnp.where(sblk <= cur, imp + jnp.where(forced, FORCE_BONUS, 0.0), NEG)
    keep = (_topk_rows(score, min(SEL_TOPK, nsel)) > 0.5) & (score > NEG / 2)
    bias = jnp.where(keep, 0.0, NEG)
    bias = jnp.concatenate([bias] * GROUP, axis=1)
    for j in range(nsel):
        bias_scr[j] = bias[j:j + 1, :]

    init = (jnp.full((1, L), M_INIT, F32), jnp.zeros((1, L), F32), jnp.zeros((NSA_HD, L), F32))
    krow = lax.broadcasted_iota(I32, (tq, 1), 0)

    def tile(k_ref, vt_ref, kt):
        start = pl.multiple_of(kt * tq, tq)
        s = _dot(k_ref[pl.ds(start, tq), :].astype(BF16), qp)
        return s, vt_ref[:, pl.ds(start, tq)]

    def sel_body(kt, carry):
        s, v_t = tile(ks_ref, vst_ref, kt)
        b = bias_scr[pl.ds(kt * spt, spt)]
        s = (s.reshape(spt, SEL_BLOCK, L) + b).reshape(tq, L)
        return _online(carry, s, v_t)

    s, v_t = tile(ks_ref, vst_ref, qi)
    b = bias_scr[pl.ds(qi * spt, spt)]
    s = (s.reshape(spt, SEL_BLOCK, L) + b).reshape(tq, L)
    s = jnp.where((qi * tq + krow) <= qpos, s, NEG)
    m_s, l_s, a_s = lax.fori_loop(0, qi, sel_body, _online(init, s, v_t))
    o_s = a_s / jnp.maximum(l_s, 1e-30)

    def win_body(kt, carry):
        s, v_t = tile(kw_ref, vwt_ref, kt)
        d = qpos - (kt * tq + krow)
        return _online(carry, jnp.where((d >= 0) & (d < WINDOW), s, NEG), v_t)

    m_w, l_w, a_w = lax.fori_loop(jnp.maximum(qi - WINDOW // tq, 0), qi, win_body, win_body(qi, init))
    o_w = a_w / jnp.maximum(l_w, 1e-30)

    gts = jax.nn.sigmoid(ngt_ref[...])
    grow = lambda i: jnp.concatenate([gts[i * GROUP + r:i * GROUP + r + 1, :] for r in range(GROUP)], axis=1)
    o = grow(0) * o_c + grow(1) * o_s + grow(2) * o_w
    for pr in range(GROUP // 2):
        two = jnp.concatenate([o[:, (2 * pr) * tq:(2 * pr + 1) * tq], o[:, (2 * pr + 1) * tq:(2 * pr + 2) * tq]], axis=0)
        o_ref[:, pr * 128:(pr + 1) * 128] = two.T


def _nsa_prompt(qt, kmean, vmt, kvs, kvw, vt, ngt, B, T, tq=256):
    nq = T // tq
    nblk = T // CMP_BLOCK
    nsel = nblk // 2
    L = GROUP * tq
    GW = GROUP * NSA_HD
    vrow = lambda base: pl.BlockSpec((NSA_HD, T), lambda b, g, q: (base + g, b))
    kslab = pl.BlockSpec((T, 128), lambda b, g, q: (b, g // 2))
    return pl.pallas_call(
        functools.partial(_nsa_prompt_kernel, tq=tq), grid=(B, KVH, nq),
        in_specs=[pl.BlockSpec((GW, tq), lambda b, g, q: (g, b * nq + q)),
                  pl.BlockSpec((nblk, 128), lambda b, g, q: (b, g // 2)),
                  pl.BlockSpec((NSA_HD, nblk), lambda b, g, q: (b * KVH + g, 0)),
                  kslab, vrow(KVH), kslab, vrow(2 * KVH),
                  pl.BlockSpec((16, tq), lambda b, g, q: (g, b * nq + q))],
        out_specs=pl.BlockSpec((tq, GW), lambda b, g, q: (b * nq + q, g)),
        out_shape=jax.ShapeDtypeStruct((B * T, NSA_HEADS * NSA_HD), F32),
        scratch_shapes=[pltpu.VMEM((nsel, 1, L), F32)],
        compiler_params=_cparams(3), name="nsa_prompt",
    )(qt, kmean, vmt, kvs, vt, kvw, vt, ngt)


def _q_rows(qx_ref):
    T = qx_ref.shape[0]
    parts = [qx_ref[:, (g * GROUP + r) * KV_W:(g * GROUP + r + 1) * KV_W] for r in range(GROUP) for g in range(KVH)]
    return jnp.concatenate(parts, axis=0).astype(BF16)


def _heads_to_row(o, T):
    parts = []
    for g in range(KVH):
        for r in range(GROUP):
            r0 = (r * KVH + g) * T
            parts.append(o[r0:r0 + T, g * NSA_HD:(g + 1) * NSA_HD])
    return jnp.concatenate(parts, axis=1)


def _softmax_lanes(s, mask):
    s = jnp.where(mask, s, NEG)
    e = jnp.where(mask, jnp.exp(s - jnp.max(s, axis=1, keepdims=True)), 0.0)
    return e / jnp.maximum(jnp.sum(e, axis=1, keepdims=True), 1e-30)


def _topk_lanes(score, jid, cand, k):
    sel = jnp.zeros(score.shape, F32)
    s = score
    for _ in range(k):
        m = jnp.max(s, axis=1, keepdims=True)
        first = jnp.min(jnp.where((s == m) & cand, jid, 3.0e38), axis=1, keepdims=True)
        hit = (jid == first) & cand
        sel = jnp.where(hit, 1.0, sel)
        s = jnp.where(hit, -jnp.inf, s)
    return sel


def _online_lanes(carry, s, v_t):
    m, l, acc = carry
    m_new = jnp.maximum(m, jnp.max(s, axis=1, keepdims=True))
    alpha = jnp.exp(m - m_new)
    p = jnp.exp(s - m_new)
    return (m_new, alpha * l + jnp.sum(p, axis=1, keepdims=True),
            alpha * acc + _dot(p.astype(BF16), v_t, NT))


def _s_cmp_kernel(pt_ref, *refs, n_pg, past):
    pages = refs[:n_pg]
    qx_ref, pool_ref, oc_ref, bias_ref, km_scr = refs[n_pg:]
    c = pl.program_id(1)
    nch = pl.num_programs(1)
    T = qx_ref.shape[0]
    per = n_pg * (PAGE // CMP_BLOCK)
    hs = per // 2
    x = jnp.concatenate([pg[0] for pg in pages], axis=1)
    hi = x.astype(BF16)
    lo = (x - hi.astype(F32)).astype(BF16)
    km_scr[c] = _dot(hi, pool_ref[...]) + _dot(lo, pool_ref[...])

    @pl.when(c == nch - 1)
    def _():
        n_steps = km_scr.shape[0]
        nblk = n_steps * per
        npast = nblk // 2
        gl = KVH * T
        kvm = jnp.concatenate([km_scr[i] for i in range(n_steps)], axis=1)
        qr = _q_rows(qx_ref)
        row = lax.broadcasted_iota(I32, (NSA_HEADS * T, 1), 0)
        qpos = past + (row % T)
        lane = lax.broadcasted_iota(I32, (1, nblk), 1)
        jsel = (lane // per) * hs + (lane % hs)
        odd = (lane // hs) % 2
        vis = ((2 * jsel + odd + 1) * CMP_BLOCK - 1) <= qpos
        p = _softmax_lanes(_dot(qr, kvm[0:KV_W].astype(BF16)), vis)
        o_c = _dot(p.astype(BF16), kvm[KV_W:KV_ROW].astype(BF16), NT)
        oc_ref[...] = _heads_to_row(o_c, T)
        imp = p[0:gl] + p[gl:2 * gl] + p[2 * gl:3 * gl] + p[3 * gl:4 * gl]
        imp = imp + pltpu.roll(imp, nblk - hs, 1)
        imp = jnp.concatenate([imp, jnp.zeros((gl, 128), F32)], axis=1)
        lane2 = lax.broadcasted_iota(I32, (1, nblk + 128), 1)
        jid = jnp.where(lane2 < nblk, (lane2 // per) * hs + (lane2 % hs), npast + lane2 - nblk)
        cand = ((lane2 < nblk) & (((lane2 // hs) % 2) == 0)) | (lane2 == nblk)
        cur = (past + (lax.broadcasted_iota(I32, (gl, 1), 0) % T)) // SEL_BLOCK
        forced = (jid == 0) | (jid == cur)
        score = jnp.where(cand, jnp.where(jid <= cur, imp + jnp.where(forced, FORCE_BONUS, 0.0), NEG), -jnp.inf)
        keep = (_topk_lanes(score, jid.astype(F32), cand, SEL_TOPK) > 0.5) & (score > NEG / 2)
        bias = jnp.where(keep, 0.0, NEG)
        bias = jnp.concatenate([bias] * GROUP, axis=0)
        zpad = jnp.zeros((NSA_HEADS * T, 128 - hs), F32)
        for i in range(n_steps):
            bias_ref[0, i] = jnp.concatenate([bias[:, i * per:i * per + hs], zpad], axis=1)
        bias_ref[0, n_steps] = bias[:, nblk:nblk + 128]


def _pool_matrix(n_pg):
    rows = n_pg * PAGE
    per = rows // CMP_BLOCK
    blk = jnp.arange(rows) // CMP_BLOCK
    col = (blk % 2) * (per // 2) + blk // 2
    return jnp.where(col[:, None] == jnp.arange(per)[None, :], 1.0 / CMP_BLOCK, 0.0).astype(BF16)


def _s_cmp(page_table, pool_t, qx, DB, T, n_pg=16):
    n_pages = page_table.shape[1]
    past = n_pages * PAGE
    n_pg = min(n_pg, n_pages)
    nch = n_pages // n_pg
    per = n_pg * (PAGE // CMP_BLOCK)
    L = NSA_HEADS * T
    page_spec = lambda i: pl.BlockSpec((1, KV_ROW, PAGE), lambda b, c, pt: (pt[b, c * n_pg + i], 0, 0))
    gs = pltpu.PrefetchScalarGridSpec(
        num_scalar_prefetch=1, grid=(DB, nch),
        in_specs=[page_spec(i) for i in range(n_pg)] + [
            pl.BlockSpec((T, NSA_HEADS * KV_W), lambda b, c, pt: (b, 0)),
            pl.BlockSpec((n_pg * PAGE, per), lambda b, c, pt: (0, 0))],
        out_specs=[pl.BlockSpec((T, NSA_HEADS * NSA_HD), lambda b, c, pt: (b, 0)),
                   pl.BlockSpec((1, nch + 1, L, 128), lambda b, c, pt: (b, 0, 0, 0))],
        scratch_shapes=[pltpu.VMEM((nch, KV_ROW, per), F32)])
    return pl.pallas_call(
        functools.partial(_s_cmp_kernel, n_pg=n_pg, past=past), grid_spec=gs,
        out_shape=[jax.ShapeDtypeStruct((DB * T, NSA_HEADS * NSA_HD), F32),
                   jax.ShapeDtypeStruct((DB, nch + 1, L, 128), F32)],
        compiler_params=_cparams(2), name="sample_cmp",
    )(page_table, *([pool_t] * n_pg), qx, _pool_matrix(n_pg))


def _s_attn_kernel(pt_ref, *refs, n_pg, past):
    pages = refs[:n_pg]
    (qx_ref, bias_ref, exp_ref, win_ref, ksn_ref, kwn_ref, oc_ref, gr_ref,
     o_ref, wout_ref, m_scr, l_scr, acc_scr) = refs[n_pg:]
    c = pl.program_id(1)
    n_steps = pl.num_programs(1)
    T = qx_ref.shape[0]
    L = NSA_HEADS * T
    qr = _q_rows(qx_ref)

    @pl.when(c == 0)
    def _():
        m_scr[...] = jnp.full(m_scr.shape, M_INIT, F32)
        l_scr[...] = jnp.zeros_like(l_scr)
        acc_scr[...] = jnp.zeros_like(acc_scr)

    kv = jnp.concatenate([pg[0] for pg in pages], axis=1)
    s = _dot(qr, kv[0:KV_W].astype(BF16)) + _dot(bias_ref[0, c].astype(BF16), exp_ref[...])
    m, l, acc = _online_lanes((m_scr[...], l_scr[...], acc_scr[...]), s, kv[KV_W:KV_ROW].astype(BF16))
    m_scr[...] = m
    l_scr[...] = l
    acc_scr[...] = acc

    @pl.when(c == n_steps - 1)
    def _():
        row = lax.broadcasted_iota(I32, (L, 1), 0)
        tq = row % T
        pad = 128
        jl = lax.broadcasted_iota(I32, (1, pad), 1)
        new_ok = (jl < T) & (jl <= tq)
        zpad = jnp.zeros((pad - T, KV_ROW), F32)

        def new_rows(ref):
            t = jnp.concatenate([ref[...], zpad], axis=0).T
            return t[0:KV_W].astype(BF16), t[KV_W:KV_ROW].astype(BF16), t

        k_t, v_t, _ = new_rows(ksn_ref)
        s = jnp.where(new_ok, _dot(qr, k_t), NEG) + bias_ref[0, n_steps][:, 0:1]
        m_s, l_s, a_s = _online_lanes((m_scr[...], l_scr[...], acc_scr[...]), s, v_t)
        o_s = a_s / jnp.maximum(l_s, 1e-30)
        wb = win_ref.shape[2]
        win = win_ref[0]
        il = lax.broadcasted_iota(I32, (1, wb), 1)
        d = (past + tq) - (past - wb + il)
        init = (jnp.full((L, 1), M_INIT, F32), jnp.zeros((L, 1), F32), jnp.zeros((L, KV_W), F32))
        k_t, v_t, new_t = new_rows(kwn_ref)
        carry = _online_lanes(init, jnp.where(new_ok, _dot(qr, k_t), NEG), v_t)
        s = jnp.where((d >= 0) & (d < WINDOW), _dot(qr, win[0:KV_W].astype(BF16)), NEG)
        m_w, l_w, a_w = _online_lanes(carry, s, win[KV_W:KV_ROW].astype(BF16))
        o_w = a_w / jnp.maximum(l_w, 1e-30)
        W = NSA_HEADS * NSA_HD
        gts = jax.nn.sigmoid(gr_ref[...])
        o_ref[...] = (gts[:, 0:W] * oc_ref[...] + gts[:, W:2 * W] * _heads_to_row(o_s, T)
                      + gts[:, 2 * W:3 * W] * _heads_to_row(o_w, T))
        shifted = pltpu.roll(win, wb - T, 1)
        wout_ref[0, :, 0:wb - 128] = shifted[:, 0:wb - 128]
        tail = jnp.where(jl >= 128 - T, pltpu.roll(new_t, 128 - T, 1), shifted[:, wb - 128:wb])
        wout_ref[0, :, wb - 128:wb] = tail


def _s_attn(page_table, pool_t, qx, bias, win_t, kvs_new, kvw_new, o_c, g_rep, DB, T, n_pg=16):
    n_pages = page_table.shape[1]
    past = n_pages * PAGE
    n_pg = min(n_pg, n_pages)
    nch = n_pages // n_pg
    L = NSA_HEADS * T
    W = NSA_HEADS * NSA_HD
    keys = n_pg * PAGE
    expand = jnp.where(jnp.arange(128)[:, None] == (jnp.arange(keys) // SEL_BLOCK)[None, :], 1.0, 0.0).astype(BF16)
    page_spec = lambda i: pl.BlockSpec((1, KV_ROW, PAGE), lambda b, c, pt: (pt[b, c * n_pg + i], 0, 0))
    rowb = lambda w: pl.BlockSpec((T, w), lambda b, c, pt: (b, 0))
    per_b = lambda a: pl.BlockSpec((1,) + a.shape[1:], lambda b, c, pt: (b,) + (0,) * (a.ndim - 1))
    gs = pltpu.PrefetchScalarGridSpec(
        num_scalar_prefetch=1, grid=(DB, nch),
        in_specs=[page_spec(i) for i in range(n_pg)] + [
            rowb(NSA_HEADS * KV_W), per_b(bias), pl.BlockSpec((128, keys), lambda b, c, pt: (0, 0)),
            per_b(win_t), rowb(KV_ROW), rowb(KV_ROW), rowb(W), rowb(3 * W)],
        out_specs=[rowb(W), per_b(win_t)],
        scratch_shapes=[pltpu.VMEM((L, 1), F32), pltpu.VMEM((L, 1), F32), pltpu.VMEM((L, KV_W), F32)])
    return pl.pallas_call(
        functools.partial(_s_attn_kernel, n_pg=n_pg, past=past), grid_spec=gs,
        out_shape=[jax.ShapeDtypeStruct((DB * T, W), F32), jax.ShapeDtypeStruct(win_t.shape, F32)],
        compiler_params=_cparams(2), name="sample_attn",
    )(page_table, *([pool_t] * n_pg), qx, bias, expand, win_t, kvs_new, kvw_new, o_c, g_rep)


def _prep_weights(w_in):
    o_q = 4 * HG_HEADS * HG_DK
    o_kv = o_q + NSA_HEADS * NSA_HD
    o_ng = o_kv + 6 * KV_W
    o_mg = o_ng + NSA_HEADS * 3
    w_hg = w_in[:, :o_q].astype(BF16)
    w_q = w_in[:, o_q:o_kv] * ATT_SCALE
    w_kv = w_in[:, o_kv:o_ng].astype(BF16)
    w_ng = w_in[:, o_ng:o_mg]
    w_mg = w_in[:, o_mg:].astype(BF16)
    ng4 = w_ng.reshape(D_MODEL, KVH, GROUP, 3).transpose(0, 1, 3, 2)
    ng4 = jnp.pad(ng4.reshape(D_MODEL, KVH, 3 * GROUP), ((0, 0), (0, 0), (0, 16 - 3 * GROUP)))
    vcols = jnp.concatenate([w_in[:, o_kv + (2 * i + 1) * KV_W:o_kv + (2 * i + 2) * KV_W] for i in range(3)], axis=1)
    w_t = jnp.concatenate([w_q, vcols], axis=1).T.astype(BF16)
    w_ngt = ng4.reshape(D_MODEL, KVH * 16).T.astype(BF16)
    wq4 = w_q.reshape(D_MODEL, KVH, GROUP, NSA_HD)
    eye = jnp.eye(KVH, dtype=w_q.dtype)
    w_qx = jnp.einsum('dgrh,gk->dgrkh', wq4, eye).reshape(D_MODEL, NSA_HEADS * KV_W).astype(BF16)
    w_gr = jnp.repeat(w_ng.reshape(D_MODEL, NSA_HEADS, 3).transpose(0, 2, 1), NSA_HD, axis=2)
    w_gr = w_gr.reshape(D_MODEL, 3 * NSA_HEADS * NSA_HD).astype(BF16)
    return w_hg, w_kv, w_mg, w_t, w_ngt, w_qx, w_gr


def kernel(x_prompt, x_sample, cache_cmp_kv, cache_sel_kv, page_table, state_win_kv, state_hgrn, state_conv, w_in, hg_lb_logits, hg_norm_g, w_out, w_up, conv_w, conv_b, w_down, g_pre_mix, g_post_mix, g_pre_ffn, g_post_ffn):
    B, T, D = x_prompt.shape
    DB, TS, _ = x_sample.shape
    depth = w_in.shape[0]
    assert depth == 1 and D == D_MODEL and T % 256 == 0 and TS == 8 and TS < CMP_BLOCK
    n_pool = cache_cmp_kv.shape[1]
    wb = state_win_kv.shape[2]
    C2 = 2 * D_FF
    l = 0

    lbs = jnp.cumsum(jax.nn.softmax(hg_lb_logits.astype(F32), axis=0), axis=0)[l]
    w_hg, w_kv, w_mg, w_t, w_ngt, w_qx, w_gr = _prep_weights(w_in[l])
    w_out_b, w_up_b, w_dn_b = w_out[l].astype(BF16), w_up[l].astype(BF16), w_down[l].astype(BF16)

    outs = {}
    for name, x, nb, tl in (("p", x_prompt, B, T), ("s", x_sample, DB, TS)):
        R = nb * tl
        x2 = x.reshape(R, D)
        h = _rmsnorm_bf16(x2, g_pre_mix[l])
        zh = _mm(h, w_hg, F32, name="proj_hgrn")
        kv = [_mm(h, w_kv[:, i * KV_ROW:(i + 1) * KV_ROW], F32, name="proj_kv") for i in range(3)]
        mg = _mm(h, w_mg, F32, name="proj_merge_gate")
        s0 = jnp.zeros((nb, HG_HEADS, HG_DK, HG_DK), F32) if name == "p" else state_hgrn[l]
        o_hg, s_hg = _hgrn(zh, lbs, hg_norm_g[l], s0, nb, tl)
        if name == "p":
            qvt = _mm_nt(w_t, h, BF16, name="proj_qv_t")
            ngt = _mm_nt(w_ngt, h, F32, tn=64, name="proj_gate_t")
            kmean, vmt = _cmp_prep(kv[0], qvt[NSA_HEADS * NSA_HD:], nb, tl)
            o_nsa = _nsa_prompt(qvt, kmean, vmt, kv[1], kv[2], qvt[NSA_HEADS * NSA_HD:], ngt, nb, tl)
            win_new = kv[2].reshape(nb, tl, KV_ROW)[:, tl - min(WINDOW, tl):]
            cbuf0 = jnp.zeros((nb, CONV_W - 1, C2), F32)
        else:
            qx = _mm(h, w_qx, F32, name="proj_q_pad")
            g_rep = _mm(h, w_gr, F32, name="proj_gate_rep")
            fmaj = lambda a: a.transpose(0, 2, 3, 4, 1).reshape(a.shape[0], KV_ROW, a.shape[1])
            o_c, bias = _s_cmp(page_table, fmaj(cache_cmp_kv[l]), qx, nb, tl)
            o_nsa, win_t = _s_attn(page_table, fmaj(cache_sel_kv[l]), qx, bias, fmaj(state_win_kv[l]),
                                   kv[1], kv[2], o_c, g_rep, nb, tl)
            win_new = win_t.reshape(nb, 2, KVH, NSA_HD, wb).transpose(0, 4, 1, 2, 3).reshape(nb, wb, KV_ROW)
            cbuf0 = state_conv[l]
        x1 = _merge(mg, o_hg, o_nsa, x2, w_out_b, g_post_mix[l])
        x3, cbuf = _ffn(x1, nb, g_pre_ffn[l], w_up_b, conv_w[l], conv_b[l], w_dn_b, g_post_ffn[l], cbuf0, tm=512)
        kv6 = lambda a, n: a.reshape(1, nb, n, 2, KVH, NSA_HD)
        outs[name] = (x3.reshape(nb, tl, D), kv6(kv[0], tl), kv6(kv[1], tl), kv6(win_new, win_new.shape[1]),
                      s_hg[None], cbuf[None])
    p, s = outs["p"], outs["s"]
    return (p[0], s[0], p[1], s[1], p[2], s[2], p[3], s[3], p[4], s[4], p[5], s[5])
```

```python
import functools
import math

import jax
import jax.numpy as jnp
from jax import lax
from jax.experimental import pallas as pl
from jax.experimental.pallas import tpu as pltpu

F32 = jnp.float32
BF16 = jnp.bfloat16
I32 = jnp.int32

D_MODEL = 1024
HG_HEADS = 8
HG_DK = 128
HG_CHUNK = 32
NSA_HEADS = 16
NSA_HD = 64
KVH = 4
GROUP = 4
KV_W = KVH * NSA_HD
KV_ROW = 2 * KV_W
CMP_BLOCK = 32
SEL_BLOCK = 64
SEL_TOPK = 8
WINDOW = 512
FORCE_BONUS = 1.0e4
ATT_SCALE = NSA_HD ** -0.5
LOG2E = math.log2(math.e)
D_FF = 2816
CONV_W = 3
PAGE = 128
EPS = 1e-6
NEG = -1e30
M_INIT = -1e38

VMEM_LIMIT = 56 * 1024 * 1024

NT = (((1,), (1,)), ((), ()))
TN = (((0,), (0,)), ((), ()))


def _cparams(n_axes):
    return pltpu.CompilerParams(dimension_semantics=("arbitrary",) * n_axes,
                                vmem_limit_bytes=VMEM_LIMIT)


def _dot(a, b, dims=None):
    if dims is None:
        return jnp.dot(a, b, preferred_element_type=F32)
    return lax.dot_general(a, b, dims, preferred_element_type=F32)


def _rms(x, g):
    return x * lax.rsqrt(jnp.mean(x * x, axis=-1, keepdims=True) + EPS) * g


def _rmsnorm_kernel(x_ref, g_ref, o_ref):
    o_ref[...] = _rms(x_ref[...], g_ref[...]).astype(o_ref.dtype)


def _rmsnorm_bf16(x, g, tm=512):
    R, D = x.shape
    tm = min(tm, R)
    return pl.pallas_call(
        _rmsnorm_kernel, grid=(R // tm,),
        in_specs=[pl.BlockSpec((tm, D), lambda i: (i, 0)), pl.BlockSpec((1, D), lambda i: (0, 0))],
        out_specs=pl.BlockSpec((tm, D), lambda i: (i, 0)),
        out_shape=jax.ShapeDtypeStruct((R, D), BF16),
        compiler_params=_cparams(1), name="rmsnorm",
    )(x, g.reshape(1, D))


def _mm_kernel(a_ref, w_ref, o_ref):
    o_ref[...] = _dot(a_ref[...], w_ref[...]).astype(o_ref.dtype)


def _mm(a, w, out_dtype, tm=1024, tn=512, name="proj"):
    R, K = a.shape
    N = w.shape[1]
    tm, tn = min(tm, R), min(tn, N)
    return pl.pallas_call(
        _mm_kernel, grid=(R // tm, N // tn),
        in_specs=[pl.BlockSpec((tm, K), lambda i, j: (i, 0)), pl.BlockSpec((K, tn), lambda i, j: (0, j))],
        out_specs=pl.BlockSpec((tm, tn), lambda i, j: (i, j)),
        out_shape=jax.ShapeDtypeStruct((R, N), out_dtype),
        compiler_params=_cparams(2), name=name,
    )(a, w)


def _mm_nt_kernel(wt_ref, a_ref, o_ref):
    o_ref[...] = _dot(wt_ref[...], a_ref[...], NT).astype(o_ref.dtype)


def _mm_nt(wt, a, out_dtype, tm=1024, tn=256, name="proj_t"):
    N, K = wt.shape
    R = a.shape[0]
    tm, tn = min(tm, R), min(tn, N)
    return pl.pallas_call(
        _mm_nt_kernel, grid=(R // tm, N // tn),
        in_specs=[pl.BlockSpec((tn, K), lambda i, j: (j, 0)), pl.BlockSpec((tm, K), lambda i, j: (i, 0))],
        out_specs=pl.BlockSpec((tn, tm), lambda i, j: (j, i)),
        out_shape=jax.ShapeDtypeStruct((N, R), out_dtype),
        compiler_params=_cparams(2), name=name,
    )(wt, a)


def _merge_kernel(mg_ref, ohg_ref, onsa_ref, x_ref, w_ref, g_ref, o_ref):
    mg = mg_ref[...]
    u = jax.nn.sigmoid(mg[:, :D_MODEL]) * ohg_ref[...] + jax.nn.sigmoid(mg[:, D_MODEL:]) * onsa_ref[...]
    y = _dot(u.astype(BF16), w_ref[...])
    o_ref[...] = x_ref[...] + _rms(y, g_ref[...])


def _merge(mg, ohg, onsa, x, w_out, g_post, tm=512):
    R, D = x.shape
    tm = min(tm, R)
    row = lambda w: pl.BlockSpec((tm, w), lambda i: (i, 0))
    return pl.pallas_call(
        _merge_kernel, grid=(R // tm,),
        in_specs=[row(2 * D), row(D), row(D), row(D),
                  pl.BlockSpec((D, D), lambda i: (0, 0)), pl.BlockSpec((1, D), lambda i: (0, 0))],
        out_specs=row(D),
        out_shape=jax.ShapeDtypeStruct((R, D), F32),
        compiler_params=_cparams(1), name="merge",
    )(mg, ohg, onsa, x, w_out, g_post.reshape(1, D))


def _ffn_kernel(x_ref, gpre_ref, wup_ref, cw_ref, cb_ref, wdn_ref, gpost_ref, buf_ref,
                o_ref, cbuf_ref, acc_scr, carry_scr, *, ns, tn):
    tm = x_ref.shape[0]
    ls = tm // ns
    x = x_ref[...]
    h = _rms(x, gpre_ref[...]).astype(BF16)
    pos = lax.broadcasted_iota(I32, (tm, 1), 0) % ls
    if ns == 1:
        @pl.when(pl.program_id(1) == 0)
        def _():
            carry_scr[...] = buf_ref[0]
    acc_scr[...] = jnp.zeros_like(acc_scr)
    for j in range(D_FF // tn):
        halves = []
        for base in (j * tn, D_FF + j * tn):
            cols = slice(base, base + tn)
            up = _dot(h, wup_ref[:, cols])
            if ns == 1:
                b0 = carry_scr[0:1, cols]
                b1 = carry_scr[1:2, cols]
                carry_scr[:, cols] = up[tm - 2:tm, :]
                cbuf_ref[0, :, cols] = up[tm - 2:tm, :]
            else:
                bufv = buf_ref[:, :, cols]
                b0 = jnp.broadcast_to(bufv[:, 0:1, :], (ns, ls, tn)).reshape(tm, tn)
                b1 = jnp.broadcast_to(bufv[:, 1:2, :], (ns, ls, tn)).reshape(tm, tn)
                cbuf_ref[:, :, cols] = up.reshape(ns, ls, tn)[:, ls - 2:ls, :]
            def conv(x, x1, x2):
                c = cb_ref[:, cols] + x2 * cw_ref[0:1, cols]
                c = c + x1 * cw_ref[1:2, cols]
                return c + x * cw_ref[2:3, cols]

            def fix(p, x1, x2):
                return jnp.where(p == 0, b1, x1), jnp.where(p == 0, b0, jnp.where(p == 1, b1, x2))

            r1, r2 = pltpu.roll(up, 1, 0), pltpu.roll(up, 2, 0)
            if ns == 1:
                head = conv(up[0:8], *fix(pos[0:8], r1[0:8], r2[0:8]))
                c = jnp.concatenate([head, conv(up, r1, r2)[8:]], axis=0)
            else:
                c = conv(up, *fix(pos, r1, r2))
            halves.append(c)
        act = (jax.nn.silu(halves[0]) * halves[1]).astype(BF16)
        acc_scr[...] += _dot(act, wdn_ref[j * tn:(j + 1) * tn, :])
    o_ref[...] = x + _rms(acc_scr[...], gpost_ref[...])


def _ffn(x, n_seq, g_pre, w_up, conv_w, conv_b, w_down, g_post, buf, tm):
    R, D = x.shape
    L = R // n_seq
    C2 = 2 * D_FF
    if L >= tm:
        ns, nt = 1, L // tm
        grid = (n_seq, nt)
        buf_spec = pl.BlockSpec((1, 2, C2), lambda b, t: (b, 0, 0))
    else:
        ns, nt, tm = n_seq, 1, R
        grid = (1, 1)
        buf_spec = pl.BlockSpec((ns, 2, C2), lambda b, t: (0, 0, 0))
    const = lambda shape: pl.BlockSpec(shape, lambda b, t: (0,) * len(shape), pipeline_mode=pl.Buffered(1))
    return pl.pallas_call(
        functools.partial(_ffn_kernel, ns=ns, tn=256), grid=grid,
        in_specs=[pl.BlockSpec((tm, D), lambda b, t: (b * nt + t, 0)), const((1, D)), const((D, C2)),
                  const((CONV_W, C2)), const((1, C2)), const((D_FF, D)), const((1, D)), buf_spec],
        out_specs=[pl.BlockSpec((tm, D), lambda b, t: (b * nt + t, 0)), buf_spec],
        out_shape=[jax.ShapeDtypeStruct((R, D), F32), jax.ShapeDtypeStruct((n_seq, 2, C2), F32)],
        scratch_shapes=[pltpu.VMEM((tm, D), F32), pltpu.VMEM((2, C2), F32)],
        compiler_params=_cparams(2), name="conv_ffn",
    )(x, g_pre.reshape(1, D), w_up, conv_w, conv_b.reshape(1, C2), w_down, g_post.reshape(1, D), buf)


def _pad_rows(x, n):
    if x.shape[0] >= n:
        return x
    return jnp.concatenate([x, jnp.zeros((n - x.shape[0],) + x.shape[1:], x.dtype)], axis=0)


def _hgrn_kernel(q_ref, f_ref, i_ref, g_ref, lb_ref, ng_ref, s0_ref, o_ref, so_ref, st_scr, *, chunk):
    t = pl.program_id(2)
    tc = q_ref.shape[0]
    hp = q_ref.shape[1] // HG_DK
    C = chunk

    @pl.when(t == 0)
    def _():
        for hh in range(hp):
            st_scr[hh] = s0_ref[0, hh].T

    tp = max(tc, 16)
    n_ch = tc // C
    r = lax.broadcasted_iota(I32, (tp, tp), 0)
    c = lax.broadcasted_iota(I32, (tp, tp), 1)
    causal = (c <= r) & ((r // C) == (c // C))
    lmat = jnp.where(causal, 1.0, 0.0).astype(BF16)
    in_blk = ((lax.broadcasted_iota(I32, (tp, n_ch * HG_DK), 0) // C)
              == (lax.broadcasted_iota(I32, (tp, n_ch * HG_DK), 1) // HG_DK))

    lb = jnp.concatenate([lb_ref[hh] for hh in range(hp)], axis=1)
    z = _pad_rows(f_ref[...], tp)
    logf = jnp.log(lb + (1.0 - lb) * jax.nn.sigmoid(z))
    k = (1.0 - lb) * jax.nn.sigmoid(-z)
    q = jax.nn.silu(_pad_rows(q_ref[...], tp))
    hi = logf.astype(BF16)
    r1 = logf - hi.astype(F32)
    mid = r1.astype(BF16)
    lo = (r1 - mid.astype(F32)).astype(BF16)
    b = _dot(lmat, hi) + _dot(lmat, mid) + _dot(lmat, lo)
    last = [b[(ci + 1) * C - 1:(ci + 1) * C, :] for ci in range(tp // C)]
    bl = jnp.concatenate([jnp.broadcast_to(x, (C, x.shape[1])) for x in last], axis=0)
    q_in = q * jnp.exp(b)
    k_out = k * jnp.exp(bl - b)
    q_rel = (q * jnp.exp(b - bl)).astype(BF16)
    decay = [jnp.exp(x) for x in last[:n_ch]]
    q_in = q_in.astype(BF16)

    for hh in range(hp):
        lanes = slice(hh * HG_DK, (hh + 1) * HG_DK)
        vb = _pad_rows(i_ref[:, lanes], tp).astype(BF16)
        ko = k_out[:, lanes]
        attn = jnp.where(causal, _dot(q_rel[:, lanes], ko.astype(BF16), NT), 0.0)
        o = _dot(attn.astype(BF16), vb)
        kx = jnp.where(in_blk, jnp.concatenate([ko] * n_ch, axis=1), 0.0).astype(BF16)
        ut_all = _dot(vb, kx, TN)
        st = st_scr[hh]
        inter = []
        for ci in range(n_ch):
            rows = slice(ci * C, min((ci + 1) * C, tp) if C >= 16 else ci * C + 16)
            inter.append(_dot(q_in[rows, lanes], st.astype(BF16), NT)[0:C])
            st = decay[ci][:, lanes] * st + ut_all[:, ci * HG_DK:(ci + 1) * HG_DK]
        o = (o[0:tc] + (inter[0] if n_ch == 1 else jnp.concatenate(inter, axis=0)))
        st_scr[hh] = st
        o_ref[:, lanes] = _rms(o, ng_ref[...]) * jax.nn.silu(g_ref[:, lanes])

        @pl.when(t == pl.num_programs(2) - 1)
        def _():
            so_ref[0, hh] = st.T


def _hgrn(zh, lbs, norm_g, s0, B, T, hp=4):
    C = math.gcd(T, HG_CHUNK)
    tc = min(T, 256)
    nt = T // tc
    nh = HG_HEADS // hp
    sec = lambda s: pl.BlockSpec((tc, hp * HG_DK), lambda b, h, t: (b * nt + t, s * nh + h))
    st_spec = pl.BlockSpec((1, hp, HG_DK, HG_DK), lambda b, h, t: (b, h, 0, 0))
    return pl.pallas_call(
        functools.partial(_hgrn_kernel, chunk=C), grid=(B, nh, nt),
        in_specs=[sec(0), sec(1), sec(2), sec(3),
                  pl.BlockSpec((hp, 1, HG_DK), lambda b, h, t: (h, 0, 0)),
                  pl.BlockSpec((1, HG_DK), lambda b, h, t: (0, 0)), st_spec],
        out_specs=[pl.BlockSpec((tc, hp * HG_DK), lambda b, h, t: (b * nt + t, h)), st_spec],
        out_shape=[jax.ShapeDtypeStruct((B * T, HG_HEADS * HG_DK), F32),
                   jax.ShapeDtypeStruct((B, HG_HEADS, HG_DK, HG_DK), F32)],
        scratch_shapes=[pltpu.VMEM((hp, HG_DK, HG_DK), F32)],
        compiler_params=_cparams(3), name="hgrn2",
    )(zh, zh, zh, zh, lbs.reshape(HG_HEADS, 1, HG_DK), norm_g.reshape(1, HG_DK), s0)


def _topk_rows(score, k):
    n = score.shape[0]
    rid = lax.broadcasted_iota(I32, score.shape, 0)
    sel = jnp.zeros(score.shape, F32)
    s = score
    for _ in range(k):
        m = jnp.max(s, axis=0, keepdims=True)
        first = jnp.min(jnp.where(s == m, rid, n), axis=0, keepdims=True)
        hit = rid == first
        sel = jnp.where(hit, 1.0, sel)
        s = jnp.where(hit, -jnp.inf, s)
    return sel


def _softmax2_rows(s, mask):
    s = jnp.where(mask, s, NEG)
    e = jnp.where(mask, jnp.exp2(s - jnp.max(s, axis=0, keepdims=True)), 0.0)
    return e / jnp.maximum(jnp.sum(e, axis=0, keepdims=True), 1e-30)


def _online2(carry, s, v_aug):
    m, acc = carry
    m_new = jnp.maximum(m, jnp.max(s, axis=0, keepdims=True))
    p = jnp.exp2(s - m_new)
    return m_new, jnp.exp2(m - m_new) * acc + _dot(v_aug, p.astype(BF16))


def _block_means_even_odd(x):
    s = 1.0 / CMP_BLOCK
    return (x[:, 0:CMP_BLOCK, :].sum(axis=1) * s, x[:, CMP_BLOCK:SEL_BLOCK, :].sum(axis=1) * s)


def _cmp_prep_kernel(kvc_ref, vct_ref, km_ref, vmt_ref):
    hb = kvc_ref.shape[0]
    nb = 2 * hb
    T = hb * SEL_BLOCK
    ev, od = _block_means_even_odd(kvc_ref[:, :, 0:KV_W])
    km_ref[0:hb, :] = ev
    km_ref[hb:nb, :] = od
    ti = lax.broadcasted_iota(I32, (T, nb), 0)
    ci = lax.broadcasted_iota(I32, (T, nb), 1)
    blk = jnp.where(ci < hb, 2 * ci, 2 * (ci - hb) + 1)
    pool = jnp.where((ti // CMP_BLOCK) == blk, 1.0 / CMP_BLOCK, 0.0).astype(BF16)
    vmt_ref[...] = _dot(vct_ref[...], pool)


def _cmp_prep(kvc, vt, B, T):
    nb = T // CMP_BLOCK
    hb = T // SEL_BLOCK
    return pl.pallas_call(
        _cmp_prep_kernel, grid=(B,),
        in_specs=[pl.BlockSpec((hb, SEL_BLOCK, KV_ROW), lambda b: (b, 0, 0)), pl.BlockSpec((KV_W, T), lambda b: (0, b))],
        out_specs=[pl.BlockSpec((nb, KV_W), lambda b: (b, 0)), pl.BlockSpec((KV_W, nb), lambda b: (b, 0))],
        out_shape=[jax.ShapeDtypeStruct((B * nb, KV_W), F32), jax.ShapeDtypeStruct((B * KV_W, nb), F32)],
        compiler_params=_cparams(1), name="cmp_prep",
    )(kvc.reshape(B * hb, SEL_BLOCK, KV_ROW), vt)


def _nsa_prompt_kernel(qt_ref, km_ref, vmt_ref, ks_ref, vst_ref, kw_ref, vwt_ref, ngt_ref, tri_ref, o_ref,
                       *, tq):
    g = pl.program_id(1)
    qi = pl.program_id(2)
    par = g % 2
    L = GROUP * tq
    nblk = km_ref.shape[0]
    nsel = nblk // 2
    spt = tq // SEL_BLOCK

    qt = qt_ref[...]
    qs = jnp.concatenate([qt[r * NSA_HD:(r + 1) * NSA_HD, :] for r in range(GROUP)], axis=1)
    zq = jnp.zeros_like(qs)
    qp = jnp.concatenate([jnp.where(par == 0, qs, zq), jnp.where(par == 1, qs, zq)], axis=0)

    lane = lax.broadcasted_iota(I32, (1, L), 1)
    qpos = qi * tq + (lane % tq)

    row = lax.broadcasted_iota(I32, (nblk, 1), 0)
    cblk = jnp.where(row < nsel, 2 * row, 2 * (row - nsel) + 1)
    vis = ((cblk + 1) * CMP_BLOCK - 1) <= qpos
    p_c = _softmax2_rows(_dot(km_ref[...].astype(BF16), qp), vis)
    o_c = _dot(vmt_ref[...].astype(BF16), p_c.astype(BF16))
    imp = p_c[:, 0:tq]
    for r in range(1, GROUP):
        imp = imp + p_c[:, r * tq:(r + 1) * tq]
    imp = imp[0:nsel] + imp[nsel:nblk]
    sblk = lax.broadcasted_iota(I32, (nsel, 1), 0)
    cur = qpos[:, 0:tq] // SEL_BLOCK
    forced = (sblk == 0) | (sblk == cur)
    score = jnp.where(sblk <= cur, imp + jnp.where(forced, FORCE_BONUS, 0.0), NEG)
    keep = (_topk_rows(score, min(SEL_TOPK, nsel)) > 0.5) & (score > NEG / 2)
    bias = jnp.where(keep, 0.0, NEG)
    bias = jnp.concatenate([bias] * GROUP, axis=1).astype(BF16)
    qa = jnp.concatenate([qp, bias, jnp.zeros((128 - nsel, L), BF16)], axis=0)

    NR = NSA_HD + 16
    init = (jnp.full((1, L), M_INIT, F32), jnp.zeros((NR, L), F32))
    ones = jnp.ones((16, tq), BF16)
    kblk = lax.broadcasted_iota(I32, (tq, 1), 0) // SEL_BLOCK
    lane128 = lax.broadcasted_iota(I32, (1, 128), 1)

    def tile(k_ref, vt_ref, kt, with_bias):
        start = pl.multiple_of(kt * tq, tq)
        kb = k_ref[pl.ds(start, tq), :].astype(BF16)
        if with_bias:
            hot = jnp.where(lane128 == kt * spt + kblk, 1.0, 0.0).astype(BF16)
            s = _dot(jnp.concatenate([kb, hot], axis=1), qa)
        else:
            s = _dot(kb, qp)
        return s, jnp.concatenate([vt_ref[:, pl.ds(start, tq)], ones], axis=0)

    def finish(acc):
        return acc[0:NSA_HD] / jnp.maximum(acc[NSA_HD:NSA_HD + 1], 1e-30)

    def branch(k_ref, vt_ref, with_bias, first, n_prev, far_mask):
        s, v_a = tile(k_ref, vt_ref, qi, with_bias)
        carry = _online2(init, s + tri_ref[0], v_a)

        def one(kt, c):
            s, v_a = tile(k_ref, vt_ref, kt, with_bias)
            if far_mask:
                s = s + tri_ref[jnp.where(kt == qi - WINDOW // tq, 2, 1)]
            return _online2(c, s, v_a)

        carry = lax.fori_loop(0, n_prev // 2, lambda i, c: one(first + 2 * i + 1, one(first + 2 * i, c)), carry)
        carry = lax.cond(n_prev % 2 == 1, lambda c: one(first + n_prev - 1, c), lambda c: c, carry)
        return finish(carry[1])

    o_s = branch(ks_ref, vst_ref, True, 0, qi, False)
    n_w = jnp.minimum(qi, WINDOW // tq)
    o_w = branch(kw_ref, vwt_ref, False, qi - n_w, n_w, True)

    gts = jax.nn.sigmoid(ngt_ref[...])
    grow = lambda i: jnp.concatenate([gts[i * GROUP + r:i * GROUP + r + 1, :] for r in range(GROUP)], axis=1)
    o = grow(0) * o_c + grow(1) * o_s + grow(2) * o_w
    for pr in range(GROUP // 2):
        two = jnp.concatenate([o[:, (2 * pr) * tq:(2 * pr + 1) * tq], o[:, (2 * pr + 1) * tq:(2 * pr + 2) * tq]], axis=0)
        o_ref[:, pr * 128:(pr + 1) * 128] = two.T


def _nsa_prompt(qt, kmean, vmt, kvs, kvw, vt, ngt, B, T, tq=256):
    nq = T // tq
    nblk = T // CMP_BLOCK
    nsel = nblk // 2
    L = GROUP * tq
    GW = GROUP * NSA_HD
    assert WINDOW == 2 * tq and nsel <= 128
    vrow = lambda base: pl.BlockSpec((NSA_HD, T), lambda b, g, q: (base + g, b))
    kslab = pl.BlockSpec((T, 128), lambda b, g, q: (b, g // 2))
    kr = jnp.arange(tq)[:, None]
    qc = (jnp.arange(L) % tq)[None, :]
    tri = jnp.stack([jnp.where(kr <= qc, 0.0, NEG), jnp.zeros((tq, L)), jnp.where(kr > qc, 0.0, NEG)]).astype(F32)
    return pl.pallas_call(
        functools.partial(_nsa_prompt_kernel, tq=tq), grid=(B, KVH, nq),
        in_specs=[pl.BlockSpec((GW, tq), lambda b, g, q: (g, b * nq + q)),
                  pl.BlockSpec((nblk, 128), lambda b, g, q: (b, g // 2)),
                  pl.BlockSpec((NSA_HD, nblk), lambda b, g, q: (b * KVH + g, 0)),
                  kslab, vrow(KVH), kslab, vrow(2 * KVH),
                  pl.BlockSpec((16, tq), lambda b, g, q: (g, b * nq + q)),
                  pl.BlockSpec((3, tq, L), lambda b, g, q: (0, 0, 0))],
        out_specs=pl.BlockSpec((tq, GW), lambda b, g, q: (b * nq + q, g)),
        out_shape=jax.ShapeDtypeStruct((B * T, NSA_HEADS * NSA_HD), F32),
        compiler_params=_cparams(3), name="nsa_prompt",
    )(qt, kmean, vmt, kvs, vt, kvw, vt, ngt, tri)


def _q_rows(qx_ref):
    T = qx_ref.shape[0]
    parts = [qx_ref[:, (g * GROUP + r) * KV_W:(g * GROUP + r + 1) * KV_W] for r in range(GROUP) for g in range(KVH)]
    return jnp.concatenate(parts, axis=0).astype(BF16)


def _heads_to_row(o, T):
    parts = []
    for g in range(KVH):
        for r in range(GROUP):
            r0 = (r * KVH + g) * T
            parts.append(o[r0:r0 + T, g * NSA_HD:(g + 1) * NSA_HD])
    return jnp.concatenate(parts, axis=1)


def _softmax_lanes(s, mask):
    s = jnp.where(mask, s, NEG)
    e = jnp.where(mask, jnp.exp(s - jnp.max(s, axis=1, keepdims=True)), 0.0)
    return e / jnp.maximum(jnp.sum(e, axis=1, keepdims=True), 1e-30)


def _topk_lanes(score, jid, cand, k):
    sel = jnp.zeros(score.shape, F32)
    s = score
    for _ in range(k):
        m = jnp.max(s, axis=1, keepdims=True)
        first = jnp.min(jnp.where((s == m) & cand, jid, 3.0e38), axis=1, keepdims=True)
        hit = (jid == first) & cand
        sel = jnp.where(hit, 1.0, sel)
        s = jnp.where(hit, -jnp.inf, s)
    return sel


def _online_lanes(carry, s, v_t):
    m, l, acc = carry
    m_new = jnp.maximum(m, jnp.max(s, axis=1, keepdims=True))
    alpha = jnp.exp(m - m_new)
    p = jnp.exp(s - m_new)
    return (m_new, alpha * l + jnp.sum(p, axis=1, keepdims=True),
            alpha * acc + _dot(p.astype(BF16), v_t, NT))


def _s_cmp_kernel(pt_ref, *refs, n_pg, past):
    pages = refs[:n_pg]
    qx_ref, pool_ref, oc_ref, bias_ref, km_scr = refs[n_pg:]
    c = pl.program_id(1)
    nch = pl.num_programs(1)
    T = qx_ref.shape[0]
    per = n_pg * (PAGE // CMP_BLOCK)
    hs = per // 2
    x = jnp.concatenate([pg[0].astype(BF16) for pg in pages], axis=1)
    km_scr[c] = _dot(x, pool_ref[...])

    @pl.when(c == nch - 1)
    def _():
        n_steps = km_scr.shape[0]
        nblk = n_steps * per
        npast = nblk // 2
        gl = KVH * T
        kvm = jnp.concatenate([km_scr[i] for i in range(n_steps)], axis=1)
        qr = _q_rows(qx_ref)
        row = lax.broadcasted_iota(I32, (NSA_HEADS * T, 1), 0)
        qpos = past + (row % T)
        lane = lax.broadcasted_iota(I32, (1, nblk), 1)
        jsel = (lane // per) * hs + (lane % hs)
        odd = (lane // hs) % 2
        vis = ((2 * jsel + odd + 1) * CMP_BLOCK - 1) <= qpos
        p = _softmax_lanes(_dot(qr, kvm[0:KV_W].astype(BF16)), vis)
        o_c = _dot(p.astype(BF16), kvm[KV_W:KV_ROW].astype(BF16), NT)
        oc_ref[...] = _heads_to_row(o_c, T)
        imp = p[0:gl] + p[gl:2 * gl] + p[2 * gl:3 * gl] + p[3 * gl:4 * gl]
        imp = imp + pltpu.roll(imp, nblk - hs, 1)
        imp = jnp.concatenate([imp, jnp.zeros((gl, 128), F32)], axis=1)
        lane2 = lax.broadcasted_iota(I32, (1, nblk + 128), 1)
        jid = jnp.where(lane2 < nblk, (lane2 // per) * hs + (lane2 % hs), npast + lane2 - nblk)
        cand = ((lane2 < nblk) & (((lane2 // hs) % 2) == 0)) | (lane2 == nblk)
        cur = (past + (lax.broadcasted_iota(I32, (gl, 1), 0) % T)) // SEL_BLOCK
        forced = (jid == 0) | (jid == cur)
        score = jnp.where(cand, jnp.where(jid <= cur, imp + jnp.where(forced, FORCE_BONUS, 0.0), NEG), -jnp.inf)
        keep = (_topk_lanes(score, jid.astype(F32), cand, SEL_TOPK) > 0.5) & (score > NEG / 2)
        bias = jnp.where(keep, 0.0, NEG)
        bias = jnp.concatenate([bias] * GROUP, axis=0)
        zpad = jnp.zeros((NSA_HEADS * T, 128 - hs), F32)
        for i in range(n_steps):
            bias_ref[0, i] = jnp.concatenate([bias[:, i * per:i * per + hs], zpad], axis=1)
        bias_ref[0, n_steps] = bias[:, nblk:nblk + 128]


def _pool_matrix(n_pg):
    rows = n_pg * PAGE
    per = rows // CMP_BLOCK
    blk = jnp.arange(rows) // CMP_BLOCK
    col = (blk % 2) * (per // 2) + blk // 2
    return jnp.where(col[:, None] == jnp.arange(per)[None, :], 1.0 / CMP_BLOCK, 0.0).astype(BF16)


def _s_cmp(page_table, pool_t, qx, DB, T, n_pg=16):
    n_pages = page_table.shape[1]
    past = n_pages * PAGE
    n_pg = min(n_pg, n_pages)
    nch = n_pages // n_pg
    per = n_pg * (PAGE // CMP_BLOCK)
    L = NSA_HEADS * T
    page_spec = lambda i: pl.BlockSpec((1, KV_ROW, PAGE), lambda b, c, pt: (pt[b, c * n_pg + i], 0, 0))
    gs = pltpu.PrefetchScalarGridSpec(
        num_scalar_prefetch=1, grid=(DB, nch),
        in_specs=[page_spec(i) for i in range(n_pg)] + [
            pl.BlockSpec((T, NSA_HEADS * KV_W), lambda b, c, pt: (b, 0)),
            pl.BlockSpec((n_pg * PAGE, per), lambda b, c, pt: (0, 0))],
        out_specs=[pl.BlockSpec((T, NSA_HEADS * NSA_HD), lambda b, c, pt: (b, 0)),
                   pl.BlockSpec((1, nch + 1, L, 128), lambda b, c, pt: (b, 0, 0, 0))],
        scratch_shapes=[pltpu.VMEM((nch, KV_ROW, per), F32)])
    return pl.pallas_call(
        functools.partial(_s_cmp_kernel, n_pg=n_pg, past=past), grid_spec=gs,
        out_shape=[jax.ShapeDtypeStruct((DB * T, NSA_HEADS * NSA_HD), F32),
                   jax.ShapeDtypeStruct((DB, nch + 1, L, 128), F32)],
        compiler_params=_cparams(2), name="sample_cmp",
    )(page_table, *([pool_t] * n_pg), qx, _pool_matrix(n_pg))


def _s_attn_kernel(pt_ref, *refs, n_pg, past):
    pages = refs[:n_pg]
    (qx_ref, bias_ref, exp_ref, win_ref, ksn_ref, kwn_ref, oc_ref, gr_ref,
     o_ref, wout_ref, m_scr, l_scr, acc_scr) = refs[n_pg:]
    c = pl.program_id(1)
    n_steps = pl.num_programs(1)
    T = qx_ref.shape[0]
    L = NSA_HEADS * T
    qr = _q_rows(qx_ref)

    @pl.when(c == 0)
    def _():
        m_scr[...] = jnp.full(m_scr.shape, M_INIT, F32)
        l_scr[...] = jnp.zeros_like(l_scr)
        acc_scr[...] = jnp.zeros_like(acc_scr)

    kv = jnp.concatenate([pg[0] for pg in pages], axis=1)
    s = _dot(qr, kv[0:KV_W].astype(BF16)) + _dot(bias_ref[0, c].astype(BF16), exp_ref[...])
    m, l, acc = _online_lanes((m_scr[...], l_scr[...], acc_scr[...]), s, kv[KV_W:KV_ROW].astype(BF16))
    m_scr[...] = m
    l_scr[...] = l
    acc_scr[...] = acc

    @pl.when(c == n_steps - 1)
    def _():
        row = lax.broadcasted_iota(I32, (L, 1), 0)
        tq = row % T
        pad = 128
        jl = lax.broadcasted_iota(I32, (1, pad), 1)
        new_ok = (jl < T) & (jl <= tq)
        zpad = jnp.zeros((pad - T, KV_ROW), F32)

        def new_rows(ref):
            t = jnp.concatenate([ref[...], zpad], axis=0).T
            return t[0:KV_W].astype(BF16), t[KV_W:KV_ROW].astype(BF16), t

        k_t, v_t, _ = new_rows(ksn_ref)
        s = jnp.where(new_ok, _dot(qr, k_t), NEG) + bias_ref[0, n_steps][:, 0:1]
        m_s, l_s, a_s = _online_lanes((m_scr[...], l_scr[...], acc_scr[...]), s, v_t)
        o_s = a_s / jnp.maximum(l_s, 1e-30)
        wb = win_ref.shape[2]
        win = win_ref[0]
        il = lax.broadcasted_iota(I32, (1, wb), 1)
        d = (past + tq) - (past - wb + il)
        init = (jnp.full((L, 1), M_INIT, F32), jnp.zeros((L, 1), F32), jnp.zeros((L, KV_W), F32))
        k_t, v_t, new_t = new_rows(kwn_ref)
        carry = _online_lanes(init, jnp.where(new_ok, _dot(qr, k_t), NEG), v_t)
        s = jnp.where((d >= 0) & (d < WINDOW), _dot(qr, win[0:KV_W].astype(BF16)), NEG)
        m_w, l_w, a_w = _online_lanes(carry, s, win[KV_W:KV_ROW].astype(BF16))
        o_w = a_w / jnp.maximum(l_w, 1e-30)
        W = NSA_HEADS * NSA_HD
        gts = jax.nn.sigmoid(gr_ref[...])
        o_ref[...] = (gts[:, 0:W] * oc_ref[...] + gts[:, W:2 * W] * _heads_to_row(o_s, T)
                      + gts[:, 2 * W:3 * W] * _heads_to_row(o_w, T))
        shifted = pltpu.roll(win, wb - T, 1)
        wout_ref[0, :, 0:wb - 128] = shifted[:, 0:wb - 128]
        tail = jnp.where(jl >= 128 - T, pltpu.roll(new_t, 128 - T, 1), shifted[:, wb - 128:wb])
        wout_ref[0, :, wb - 128:wb] = tail


def _s_attn(page_table, pool_t, qx, bias, win_t, kvs_new, kvw_new, o_c, g_rep, DB, T, n_pg=16):
    n_pages = page_table.shape[1]
    past = n_pages * PAGE
    n_pg = min(n_pg, n_pages)
    nch = n_pages // n_pg
    L = NSA_HEADS * T
    W = NSA_HEADS * NSA_HD
    keys = n_pg * PAGE
    expand = jnp.where(jnp.arange(128)[:, None] == (jnp.arange(keys) // SEL_BLOCK)[None, :], 1.0, 0.0).astype(BF16)
    page_spec = lambda i: pl.BlockSpec((1, KV_ROW, PAGE), lambda b, c, pt: (pt[b, c * n_pg + i], 0, 0))
    rowb = lambda w: pl.BlockSpec((T, w), lambda b, c, pt: (b, 0))
    per_b = lambda a: pl.BlockSpec((1,) + a.shape[1:], lambda b, c, pt: (b,) + (0,) * (a.ndim - 1))
    gs = pltpu.PrefetchScalarGridSpec(
        num_scalar_prefetch=1, grid=(DB, nch),
        in_specs=[page_spec(i) for i in range(n_pg)] + [
            rowb(NSA_HEADS * KV_W), per_b(bias), pl.BlockSpec((128, keys), lambda b, c, pt: (0, 0)),
            per_b(win_t), rowb(KV_ROW), rowb(KV_ROW), rowb(W), rowb(3 * W)],
        out_specs=[rowb(W), per_b(win_t)],
        scratch_shapes=[pltpu.VMEM((L, 1), F32), pltpu.VMEM((L, 1), F32), pltpu.VMEM((L, KV_W), F32)])
    return pl.pallas_call(
        functools.partial(_s_attn_kernel, n_pg=n_pg, past=past), grid_spec=gs,
        out_shape=[jax.ShapeDtypeStruct((DB * T, W), F32), jax.ShapeDtypeStruct(win_t.shape, F32)],
        compiler_params=_cparams(2), name="sample_attn",
    )(page_table, *([pool_t] * n_pg), qx, bias, expand, win_t, kvs_new, kvw_new, o_c, g_rep)


def _prep_weights(w_in):
    o_q = 4 * HG_HEADS * HG_DK
    o_kv = o_q + NSA_HEADS * NSA_HD
    o_ng = o_kv + 6 * KV_W
    o_mg = o_ng + NSA_HEADS * 3
    w_hg = w_in[:, :o_q].astype(BF16)
    w_q = w_in[:, o_q:o_kv] * ATT_SCALE
    w_kv = w_in[:, o_kv:o_ng].astype(BF16)
    w_ng = w_in[:, o_ng:o_mg]
    w_mg = w_in[:, o_mg:].astype(BF16)
    ng4 = w_ng.reshape(D_MODEL, KVH, GROUP, 3).transpose(0, 1, 3, 2)
    ng4 = jnp.pad(ng4.reshape(D_MODEL, KVH, 3 * GROUP), ((0, 0), (0, 0), (0, 16 - 3 * GROUP)))
    vcols = jnp.concatenate([w_in[:, o_kv + (2 * i + 1) * KV_W:o_kv + (2 * i + 2) * KV_W] for i in range(3)], axis=1)
    w_t = jnp.concatenate([w_q * LOG2E, vcols], axis=1).T.astype(BF16)
    w_ngt = ng4.reshape(D_MODEL, KVH * 16).T.astype(BF16)
    wq4 = w_q.reshape(D_MODEL, KVH, GROUP, NSA_HD)
    eye = jnp.eye(KVH, dtype=w_q.dtype)
    w_qx = jnp.einsum('dgrh,gk->dgrkh', wq4, eye).reshape(D_MODEL, NSA_HEADS * KV_W).astype(BF16)
    w_gr = jnp.repeat(w_ng.reshape(D_MODEL, NSA_HEADS, 3).transpose(0, 2, 1), NSA_HD, axis=2)
    w_gr = w_gr.reshape(D_MODEL, 3 * NSA_HEADS * NSA_HD).astype(BF16)
    return w_hg, w_kv, w_mg, w_t, w_ngt, w_qx, w_gr


def kernel(x_prompt, x_sample, cache_cmp_kv, cache_sel_kv, page_table, state_win_kv, state_hgrn, state_conv, w_in, hg_lb_logits, hg_norm_g, w_out, w_up, conv_w, conv_b, w_down, g_pre_mix, g_post_mix, g_pre_ffn, g_post_ffn):
    B, T, D = x_prompt.shape
    DB, TS, _ = x_sample.shape
    depth = w_in.shape[0]
    assert depth == 1 and D == D_MODEL and T % 256 == 0 and TS == 8 and TS < CMP_BLOCK
    n_pool = cache_cmp_kv.shape[1]
    wb = state_win_kv.shape[2]
    C2 = 2 * D_FF
    l = 0

    lbs = jnp.cumsum(jax.nn.softmax(hg_lb_logits.astype(F32), axis=0), axis=0)[l]
    w_hg, w_kv, w_mg, w_t, w_ngt, w_qx, w_gr = _prep_weights(w_in[l])
    w_out_b, w_up_b, w_dn_b = w_out[l].astype(BF16), w_up[l].astype(BF16), w_down[l].astype(BF16)

    outs = {}
    for name, x, nb, tl in (("p", x_prompt, B, T), ("s", x_sample, DB, TS)):
        R = nb * tl
        x2 = x.reshape(R, D)
        h = _rmsnorm_bf16(x2, g_pre_mix[l])
        zh = _mm(h, w_hg, F32, name="proj_hgrn")
        kv = [_mm(h, w_kv[:, i * KV_ROW:(i + 1) * KV_ROW], F32, name="proj_kv") for i in range(3)]
        mg = _mm(h, w_mg, F32, name="proj_merge_gate")
        s0 = jnp.zeros((nb, HG_HEADS, HG_DK, HG_DK), F32) if name == "p" else state_hgrn[l]
        o_hg, s_hg = _hgrn(zh, lbs, hg_norm_g[l], s0, nb, tl)
        if name == "p":
            qvt = _mm_nt(w_t, h, BF16, name="proj_qv_t")
            ngt = _mm_nt(w_ngt, h, F32, tn=64, name="proj_gate_t")
            kmean, vmt = _cmp_prep(kv[0], qvt[NSA_HEADS * NSA_HD:], nb, tl)
            o_nsa = _nsa_prompt(qvt, kmean, vmt, kv[1], kv[2], qvt[NSA_HEADS * NSA_HD:], ngt, nb, tl)
            win_new = kv[2].reshape(nb, tl, KV_ROW)[:, tl - min(WINDOW, tl):]
            cbuf0 = jnp.zeros((nb, CONV_W - 1, C2), F32)
        else:
            qx = _mm(h, w_qx, F32, name="proj_q_pad")
            g_rep = _mm(h, w_gr, F32, name="proj_gate_rep")
            fmaj = lambda a: a.transpose(0, 2, 3, 4, 1).reshape(a.shape[0], KV_ROW, a.shape[1])
            o_c, bias = _s_cmp(page_table, fmaj(cache_cmp_kv[l]), qx, nb, tl)
            o_nsa, win_t = _s_attn(page_table, fmaj(cache_sel_kv[l]), qx, bias, fmaj(state_win_kv[l]),
                                   kv[1], kv[2], o_c, g_rep, nb, tl)
            win_new = win_t.reshape(nb, 2, KVH, NSA_HD, wb).transpose(0, 4, 1, 2, 3).reshape(nb, wb, KV_ROW)
            cbuf0 = state_conv[l]
        x1 = _merge(mg, o_hg, o_nsa, x2, w_out_b, g_post_mix[l])
        x3, cbuf = _ffn(x1, nb, g_pre_ffn[l], w_up_b, conv_w[l], conv_b[l], w_dn_b, g_post_ffn[l], cbuf0, tm=512)
        kv6 = lambda a, n: a.reshape(1, nb, n, 2, KVH, NSA_HD)
        outs[name] = (x3.reshape(nb, tl, D), kv6(kv[0], tl), kv6(kv[1], tl), kv6(win_new, win_new.shape[1]),
                      s_hg[None], cbuf[None])
    p, s = outs["p"], outs["s"]
    return (p[0], s[0], p[1], s[1], p[2], s[2], p[3], s[3], p[4], s[4], p[5], s[5])
```

```python
import functools
import math

import jax
import jax.numpy as jnp
from jax import lax
from jax.experimental import pallas as pl
from jax.experimental.pallas import tpu as pltpu

F32 = jnp.float32
BF16 = jnp.bfloat16
I32 = jnp.int32

D_MODEL = 1024
HG_HEADS = 8
HG_DK = 128
HG_CHUNK = 32
NSA_HEADS = 16
NSA_HD = 64
KVH = 4
GROUP = 4
KV_W = KVH * NSA_HD
KV_ROW = 2 * KV_W
CMP_BLOCK = 32
SEL_BLOCK = 64
SEL_TOPK = 8
WINDOW = 512
FORCE_BONUS = 1.0e4
ATT_SCALE = NSA_HD ** -0.5
LOG2E = math.log2(math.e)
D_FF = 2816
CONV_W = 3
PAGE = 128
EPS = 1e-6
NEG = -1e30
M_INIT = -1e38

VMEM_LIMIT = 56 * 1024 * 1024

NT = (((1,), (1,)), ((), ()))
TN = (((0,), (0,)), ((), ()))


def _cparams(n_axes):
    return pltpu.CompilerParams(dimension_semantics=("arbitrary",) * n_axes,
                                vmem_limit_bytes=VMEM_LIMIT)


def _dot(a, b, dims=None):
    if dims is None:
        return jnp.dot(a, b, preferred_element_type=F32)
    return lax.dot_general(a, b, dims, preferred_element_type=F32)


def _rms(x, g):
    return x * lax.rsqrt(jnp.mean(x * x, axis=-1, keepdims=True) + EPS) * g


def _rmsnorm_kernel(x_ref, g_ref, o_ref):
    o_ref[...] = _rms(x_ref[...], g_ref[...]).astype(o_ref.dtype)


def _rmsnorm_bf16(x, g, tm=512):
    R, D = x.shape
    tm = min(tm, R)
    return pl.pallas_call(
        _rmsnorm_kernel, grid=(R // tm,),
        in_specs=[pl.BlockSpec((tm, D), lambda i: (i, 0)), pl.BlockSpec((1, D), lambda i: (0, 0))],
        out_specs=pl.BlockSpec((tm, D), lambda i: (i, 0)),
        out_shape=jax.ShapeDtypeStruct((R, D), BF16),
        compiler_params=_cparams(1), name="rmsnorm",
    )(x, g.reshape(1, D))


def _mm_kernel(a_ref, w_ref, o_ref):
    o_ref[...] = _dot(a_ref[...], w_ref[...]).astype(o_ref.dtype)


def _mm(a, w, out_dtype, tm=1024, tn=512, name="proj"):
    R, K = a.shape
    N = w.shape[1]
    tm, tn = min(tm, R), min(tn, N)
    return pl.pallas_call(
        _mm_kernel, grid=(R // tm, N // tn),
        in_specs=[pl.BlockSpec((tm, K), lambda i, j: (i, 0)), pl.BlockSpec((K, tn), lambda i, j: (0, j))],
        out_specs=pl.BlockSpec((tm, tn), lambda i, j: (i, j)),
        out_shape=jax.ShapeDtypeStruct((R, N), out_dtype),
        compiler_params=_cparams(2), name=name,
    )(a, w)


def _mm_nt_kernel(wt_ref, a_ref, o_ref):
    o_ref[...] = _dot(wt_ref[...], a_ref[...], NT).astype(o_ref.dtype)


def _mm_nt(wt, a, out_dtype, tm=1024, tn=256, name="proj_t"):
    N, K = wt.shape
    R = a.shape[0]
    tm, tn = min(tm, R), min(tn, N)
    return pl.pallas_call(
        _mm_nt_kernel, grid=(R // tm, N // tn),
        in_specs=[pl.BlockSpec((tn, K), lambda i, j: (j, 0)), pl.BlockSpec((tm, K), lambda i, j: (i, 0))],
        out_specs=pl.BlockSpec((tn, tm), lambda i, j: (j, i)),
        out_shape=jax.ShapeDtypeStruct((N, R), out_dtype),
        compiler_params=_cparams(2), name=name,
    )(wt, a)


def _merge_kernel(x_ref, gpre_ref, wmg_ref, ohg_ref, onsa_ref, w_ref, g_ref, o_ref):
    x = x_ref[...]
    mg = _dot(_rms(x, gpre_ref[...]).astype(BF16), wmg_ref[...])
    u = jax.nn.sigmoid(mg[:, :D_MODEL]) * ohg_ref[...] + jax.nn.sigmoid(mg[:, D_MODEL:]) * onsa_ref[...]
    y = _dot(u.astype(BF16), w_ref[...])
    o_ref[...] = x + _rms(y, g_ref[...])


def _merge(x, g_pre, w_mg, ohg, onsa, w_out, g_post, tm=512):
    R, D = x.shape
    tm = min(tm, R)
    row = pl.BlockSpec((tm, D), lambda i: (i, 0))
    const = lambda shape: pl.BlockSpec(shape, lambda i: (0, 0), pipeline_mode=pl.Buffered(1))
    return pl.pallas_call(
        _merge_kernel, grid=(R // tm,),
        in_specs=[row, const((1, D)), const((D, 2 * D)), row, row, const((D, D)), const((1, D))],
        out_specs=row,
        out_shape=jax.ShapeDtypeStruct((R, D), F32),
        compiler_params=_cparams(1), name="merge",
    )(x, g_pre.reshape(1, D), w_mg, ohg, onsa, w_out, g_post.reshape(1, D))


def _ffn_kernel(x_ref, gpre_ref, wup_ref, cw_ref, cb_ref, wdn_ref, gpost_ref, buf_ref,
                o_ref, cbuf_ref, act_scr, carry_scr, *, ns, tn):
    tm = x_ref.shape[0]
    ls = tm // ns
    x = x_ref[...]
    h = _rms(x, gpre_ref[...]).astype(BF16)
    pos = lax.broadcasted_iota(I32, (tm, 1), 0) % ls
    if ns == 1:
        @pl.when(pl.program_id(1) == 0)
        def _():
            carry_scr[...] = buf_ref[0]
    for j in range(D_FF // tn):
        halves = []
        for base in (j * tn, D_FF + j * tn):
            cols = slice(base, base + tn)
            up = _dot(h, wup_ref[:, cols])
            if ns == 1:
                b0 = carry_scr[0:1, cols]
                b1 = carry_scr[1:2, cols]
                carry_scr[:, cols] = up[tm - 2:tm, :]
                cbuf_ref[0, :, cols] = up[tm - 2:tm, :]
            else:
                bufv = buf_ref[:, :, cols]
                b0 = jnp.broadcast_to(bufv[:, 0:1, :], (ns, ls, tn)).reshape(tm, tn)
                b1 = jnp.broadcast_to(bufv[:, 1:2, :], (ns, ls, tn)).reshape(tm, tn)
                cbuf_ref[:, :, cols] = up.reshape(ns, ls, tn)[:, ls - 2:ls, :]
            def conv(x, x1, x2):
                c = cb_ref[:, cols] + x2 * cw_ref[0:1, cols]
                c = c + x1 * cw_ref[1:2, cols]
                return c + x * cw_ref[2:3, cols]

            def fix(p, x1, x2):
                return jnp.where(p == 0, b1, x1), jnp.where(p == 0, b0, jnp.where(p == 1, b1, x2))

            r1, r2 = pltpu.roll(up, 1, 0), pltpu.roll(up, 2, 0)
            if ns == 1:
                head = conv(up[0:8], *fix(pos[0:8], r1[0:8], r2[0:8]))
                c = jnp.concatenate([head, conv(up, r1, r2)[8:]], axis=0)
            else:
                c = conv(up, *fix(pos, r1, r2))
            halves.append(c)
        act_scr[:, j * tn:(j + 1) * tn] = (jax.nn.silu(halves[0]) * halves[1]).astype(BF16)
    o_ref[...] = x + _rms(_dot(act_scr[...], wdn_ref[...]), gpost_ref[...])


def _ffn(x, n_seq, g_pre, w_up, conv_w, conv_b, w_down, g_post, buf, tm):
    R, D = x.shape
    L = R // n_seq
    C2 = 2 * D_FF
    if L >= tm:
        ns, nt = 1, L // tm
        grid = (n_seq, nt)
        buf_spec = pl.BlockSpec((1, 2, C2), lambda b, t: (b, 0, 0))
    else:
        ns, nt, tm = n_seq, 1, R
        grid = (1, 1)
        buf_spec = pl.BlockSpec((ns, 2, C2), lambda b, t: (0, 0, 0))
    const = lambda shape: pl.BlockSpec(shape, lambda b, t: (0,) * len(shape), pipeline_mode=pl.Buffered(1))
    return pl.pallas_call(
        functools.partial(_ffn_kernel, ns=ns, tn=256), grid=grid,
        in_specs=[pl.BlockSpec((tm, D), lambda b, t: (b * nt + t, 0)), const((1, D)), const((D, C2)),
                  const((CONV_W, C2)), const((1, C2)), const((D_FF, D)), const((1, D)), buf_spec],
        out_specs=[pl.BlockSpec((tm, D), lambda b, t: (b * nt + t, 0)), buf_spec],
        out_shape=[jax.ShapeDtypeStruct((R, D), F32), jax.ShapeDtypeStruct((n_seq, 2, C2), F32)],
        scratch_shapes=[pltpu.VMEM((tm, D_FF), BF16), pltpu.VMEM((2, C2), F32)],
        compiler_params=_cparams(2), name="conv_ffn",
    )(x, g_pre.reshape(1, D), w_up, conv_w, conv_b.reshape(1, C2), w_down, g_post.reshape(1, D), buf)


def _pad_rows(x, n):
    if x.shape[0] >= n:
        return x
    return jnp.concatenate([x, jnp.zeros((n - x.shape[0],) + x.shape[1:], x.dtype)], axis=0)


def _hgrn_kernel(x_ref, gpre_ref, w_ref, lb_ref, ng_ref, s0_ref, o_ref, so_ref, st_scr, *, chunk):
    t = pl.program_id(1)
    tc = x_ref.shape[0]
    hp = HG_HEADS
    W = hp * HG_DK
    C = chunk
    zh = _dot(_rms(x_ref[...], gpre_ref[...]).astype(BF16), w_ref[...])
    zq, zf, zi, zg = (zh[:, s * W:(s + 1) * W] for s in range(4))

    @pl.when(t == 0)
    def _():
        for hh in range(hp):
            st_scr[hh] = s0_ref[0, hh].T

    tp = max(tc, 16)
    n_ch = tc // C
    r = lax.broadcasted_iota(I32, (tp, tp), 0)
    c = lax.broadcasted_iota(I32, (tp, tp), 1)
    causal = (c <= r) & ((r // C) == (c // C))
    lmat = jnp.where(causal, 1.0, 0.0).astype(BF16)
    in_blk = ((lax.broadcasted_iota(I32, (tp, n_ch * HG_DK), 0) // C)
              == (lax.broadcasted_iota(I32, (tp, n_ch * HG_DK), 1) // HG_DK))

    lb = lb_ref[...]
    z = _pad_rows(zf, tp)
    logf = jnp.log(lb + (1.0 - lb) * jax.nn.sigmoid(z))
    k = (1.0 - lb) * jax.nn.sigmoid(-z)
    q = jax.nn.silu(_pad_rows(zq, tp))
    hi = logf.astype(BF16)
    r1 = logf - hi.astype(F32)
    mid = r1.astype(BF16)
    lo = (r1 - mid.astype(F32)).astype(BF16)
    b = _dot(lmat, hi) + _dot(lmat, mid) + _dot(lmat, lo)
    last = [b[(ci + 1) * C - 1:(ci + 1) * C, :] for ci in range(tp // C)]
    bl = jnp.concatenate([jnp.broadcast_to(x, (C, x.shape[1])) for x in last], axis=0)
    q_in = q * jnp.exp(b)
    k_out = k * jnp.exp(bl - b)
    q_rel = (q * jnp.exp(b - bl)).astype(BF16)
    decay = [jnp.exp(x) for x in last[:n_ch]]
    q_in = q_in.astype(BF16)

    for hh in range(hp):
        lanes = slice(hh * HG_DK, (hh + 1) * HG_DK)
        vb = _pad_rows(zi[:, lanes], tp).astype(BF16)
        ko = k_out[:, lanes]
        attn = jnp.where(causal, _dot(q_rel[:, lanes], ko.astype(BF16), NT), 0.0)
        o = _dot(attn.astype(BF16), vb)
        kx = jnp.where(in_blk, jnp.concatenate([ko] * n_ch, axis=1), 0.0).astype(BF16)
        ut_all = _dot(vb, kx, TN)
        st = st_scr[hh]
        inter = []
        for ci in range(n_ch):
            rows = slice(ci * C, min((ci + 1) * C, tp) if C >= 16 else ci * C + 16)
            inter.append(_dot(q_in[rows, lanes], st.astype(BF16), NT)[0:C])
            st = decay[ci][:, lanes] * st + ut_all[:, ci * HG_DK:(ci + 1) * HG_DK]
        o = (o[0:tc] + (inter[0] if n_ch == 1 else jnp.concatenate(inter, axis=0)))
        st_scr[hh] = st
        o_ref[:, lanes] = _rms(o, ng_ref[...]) * jax.nn.silu(zg[:, lanes])

        @pl.when(t == pl.num_programs(1) - 1)
        def _():
            so_ref[0, hh] = st.T


def _hgrn(x, g_pre, w_hg, lbs, norm_g, s0, B, T):
    C = math.gcd(T, HG_CHUNK)
    tc = min(T, 256)
    nt = T // tc
    D = x.shape[1]
    W = HG_HEADS * HG_DK
    const = lambda shape: pl.BlockSpec(shape, lambda b, t: (0, 0), pipeline_mode=pl.Buffered(1))
    st_spec = pl.BlockSpec((1, HG_HEADS, HG_DK, HG_DK), lambda b, t: (b, 0, 0, 0))
    return pl.pallas_call(
        functools.partial(_hgrn_kernel, chunk=C), grid=(B, nt),
        in_specs=[pl.BlockSpec((tc, D), lambda b, t: (b * nt + t, 0)), const((1, D)), const((D, 4 * W)),
                  const((1, W)), const((1, HG_DK)), st_spec],
        out_specs=[pl.BlockSpec((tc, W), lambda b, t: (b * nt + t, 0)), st_spec],
        out_shape=[jax.ShapeDtypeStruct((B * T, W), F32),
                   jax.ShapeDtypeStruct((B, HG_HEADS, HG_DK, HG_DK), F32)],
        scratch_shapes=[pltpu.VMEM((HG_HEADS, HG_DK, HG_DK), F32)],
        compiler_params=_cparams(2), name="hgrn2",
    )(x, g_pre.reshape(1, D), w_hg, lbs.reshape(1, W), norm_g.reshape(1, HG_DK), s0)


def _topk_rows(score, k):
    n = score.shape[0]
    rid = lax.broadcasted_iota(I32, score.shape, 0)
    sel = jnp.zeros(score.shape, F32)
    s = score
    for _ in range(k):
        m = jnp.max(s, axis=0, keepdims=True)
        first = jnp.min(jnp.where(s == m, rid, n), axis=0, keepdims=True)
        hit = rid == first
        sel = jnp.where(hit, 1.0, sel)
        s = jnp.where(hit, -jnp.inf, s)
    return sel


def _softmax2_rows(s, mask):
    s = jnp.where(mask, s, NEG)
    e = jnp.where(mask, jnp.exp2(s - jnp.max(s, axis=0, keepdims=True)), 0.0)
    return e / jnp.maximum(jnp.sum(e, axis=0, keepdims=True), 1e-30)


def _online2(carry, s, v_aug):
    m, acc = carry
    m_new = jnp.maximum(m, jnp.max(s, axis=0, keepdims=True))
    p = jnp.exp2(s - m_new)
    return m_new, jnp.exp2(m - m_new) * acc + _dot(v_aug, p.astype(BF16))


def _block_means_even_odd(x):
    s = 1.0 / CMP_BLOCK
    return (x[:, 0:CMP_BLOCK, :].sum(axis=1) * s, x[:, CMP_BLOCK:SEL_BLOCK, :].sum(axis=1) * s)


def _cmp_prep_kernel(kvc_ref, vct_ref, km_ref, vmt_ref):
    hb = kvc_ref.shape[0]
    nb = 2 * hb
    T = hb * SEL_BLOCK
    ev, od = _block_means_even_odd(kvc_ref[:, :, 0:KV_W])
    km_ref[0:hb, :] = ev
    km_ref[hb:nb, :] = od
    ti = lax.broadcasted_iota(I32, (T, nb), 0)
    ci = lax.broadcasted_iota(I32, (T, nb), 1)
    blk = jnp.where(ci < hb, 2 * ci, 2 * (ci - hb) + 1)
    pool = jnp.where((ti // CMP_BLOCK) == blk, 1.0 / CMP_BLOCK, 0.0).astype(BF16)
    vmt_ref[...] = _dot(vct_ref[...], pool)


def _cmp_prep(kvc, vt, B, T):
    nb = T // CMP_BLOCK
    hb = T // SEL_BLOCK
    return pl.pallas_call(
        _cmp_prep_kernel, grid=(B,),
        in_specs=[pl.BlockSpec((hb, SEL_BLOCK, KV_ROW), lambda b: (b, 0, 0)), pl.BlockSpec((KV_W, T), lambda b: (0, b))],
        out_specs=[pl.BlockSpec((nb, KV_W), lambda b: (b, 0)), pl.BlockSpec((KV_W, nb), lambda b: (b, 0))],
        out_shape=[jax.ShapeDtypeStruct((B * nb, KV_W), F32), jax.ShapeDtypeStruct((B * KV_W, nb), F32)],
        compiler_params=_cparams(1), name="cmp_prep",
    )(kvc.reshape(B * hb, SEL_BLOCK, KV_ROW), vt)


def _nsa_prompt_kernel(qt_ref, km_ref, vmt_ref, ks_ref, vst_ref, kw_ref, vwt_ref, ngt_ref, tri_ref, o_ref,
                       *, tq, tk):
    g = pl.program_id(1)
    qi = pl.program_id(2)
    par = g % 2
    L = GROUP * tq
    nblk = km_ref.shape[0]
    nsel = nblk // 2

    qt = qt_ref[...]
    qs = jnp.concatenate([qt[r * NSA_HD:(r + 1) * NSA_HD, :] for r in range(GROUP)], axis=1)
    zq = jnp.zeros_like(qs)
    qp = jnp.concatenate([jnp.where(par == 0, qs, zq), jnp.where(par == 1, qs, zq)], axis=0)

    lane = lax.broadcasted_iota(I32, (1, L), 1)
    qpos = qi * tq + (lane % tq)

    row = lax.broadcasted_iota(I32, (nblk, 1), 0)
    cblk = jnp.where(row < nsel, 2 * row, 2 * (row - nsel) + 1)
    vis = ((cblk + 1) * CMP_BLOCK - 1) <= qpos
    p_c = _softmax2_rows(_dot(km_ref[...].astype(BF16), qp), vis)
    o_c = _dot(vmt_ref[...].astype(BF16), p_c.astype(BF16))
    imp = p_c[:, 0:tq]
    for r in range(1, GROUP):
        imp = imp + p_c[:, r * tq:(r + 1) * tq]
    imp = imp[0:nsel] + imp[nsel:nblk]
    sblk = lax.broadcasted_iota(I32, (nsel, 1), 0)
    cur = qpos[:, 0:tq] // SEL_BLOCK
    forced = (sblk == 0) | (sblk == cur)
    score = jnp.where(sblk <= cur, imp + jnp.where(forced, FORCE_BONUS, 0.0), NEG)
    keep = (_topk_rows(score, min(SEL_TOPK, nsel)) > 0.5) & (score > NEG / 2)
    bias = jnp.where(keep, 0.0, NEG)
    bias = jnp.concatenate([bias] * GROUP, axis=1).astype(BF16)
    qa = jnp.concatenate([qp, bias, jnp.zeros((128 - nsel, L), BF16)], axis=0)

    NR = NSA_HD + 16
    init = (jnp.full((1, L), M_INIT, F32), jnp.zeros((NR, L), F32))
    sub = tq // tk
    spt = tk // SEL_BLOCK
    wt = WINDOW // tk
    ones = jnp.ones((16, tk), BF16)
    kblk = lax.broadcasted_iota(I32, (tk, 1), 0) // SEL_BLOCK
    lane128 = lax.broadcasted_iota(I32, (1, 128), 1)

    def tile(k_ref, vt_ref, kt, with_bias):
        start = pl.multiple_of(kt * tk, tk)
        kb = k_ref[pl.ds(start, tk), :].astype(BF16)
        if with_bias:
            hot = jnp.where(lane128 == kt * spt + kblk, 1.0, 0.0).astype(BF16)
            s = _dot(jnp.concatenate([kb, hot], axis=1), qa)
        else:
            s = _dot(kb, qp)
        return s, jnp.concatenate([vt_ref[:, pl.ds(start, tk)], ones], axis=0)

    def finish(acc):
        return acc[0:NSA_HD] / jnp.maximum(acc[NSA_HD:NSA_HD + 1], 1e-30)

    def branch(k_ref, vt_ref, with_bias, first, n_prev, far_mask):
        carry = init
        for j in range(sub):
            s, v_a = tile(k_ref, vt_ref, qi * sub + j, with_bias)
            carry = _online2(carry, s + tri_ref[j], v_a)

        def one(kt, c):
            s, v_a = tile(k_ref, vt_ref, kt, with_bias)
            if far_mask:
                j = kt - (qi * sub - wt)
                s = s + tri_ref[jnp.where((j >= 0) & (j < sub), sub + 1 + j, sub)]
            return _online2(c, s, v_a)

        carry = lax.fori_loop(0, n_prev // 2, lambda i, c: one(first + 2 * i + 1, one(first + 2 * i, c)), carry)
        carry = lax.cond(n_prev % 2 == 1, lambda c: one(first + n_prev - 1, c), lambda c: c, carry)
        return finish(carry[1])

    o_s = branch(ks_ref, vst_ref, True, 0, qi * sub, False)
    n_w = jnp.minimum(qi * sub, wt)
    o_w = branch(kw_ref, vwt_ref, False, qi * sub - n_w, n_w, True)

    gts = jax.nn.sigmoid(ngt_ref[...])
    grow = lambda i: jnp.concatenate([gts[i * GROUP + r:i * GROUP + r + 1, :] for r in range(GROUP)], axis=1)
    o = grow(0) * o_c + grow(1) * o_s + grow(2) * o_w
    for pr in range(GROUP // 2):
        two = jnp.concatenate([o[:, (2 * pr) * tq:(2 * pr + 1) * tq], o[:, (2 * pr + 1) * tq:(2 * pr + 2) * tq]], axis=0)
        o_ref[:, pr * 128:(pr + 1) * 128] = two.T


def _nsa_prompt(qt, kmean, vmt, kvs, kvw, vt, ngt, B, T, tq=256, tk=256):
    nq = T // tq
    nblk = T // CMP_BLOCK
    nsel = nblk // 2
    L = GROUP * tq
    GW = GROUP * NSA_HD
    sub = tq // tk
    assert tq % tk == 0 and tk % SEL_BLOCK == 0 and WINDOW % tk == 0 and WINDOW >= tq and nsel <= 128
    vrow = lambda base: pl.BlockSpec((NSA_HD, T), lambda b, g, q: (base + g, b))
    kslab = pl.BlockSpec((T, 128), lambda b, g, q: (b, g // 2))
    qc = (jnp.arange(L) % tq)[None, :]
    kr = [jnp.arange(tk)[:, None] + j * tk for j in range(sub)]
    tri = jnp.stack([jnp.where(k <= qc, 0.0, NEG) for k in kr] + [jnp.zeros((tk, L))]
                    + [jnp.where(k > qc, 0.0, NEG) for k in kr]).astype(F32)
    return pl.pallas_call(
        functools.partial(_nsa_prompt_kernel, tq=tq, tk=tk), grid=(B, KVH, nq),
        in_specs=[pl.BlockSpec((GW, tq), lambda b, g, q: (g, b * nq + q)),
                  pl.BlockSpec((nblk, 128), lambda b, g, q: (b, g // 2)),
                  pl.BlockSpec((NSA_HD, nblk), lambda b, g, q: (b * KVH + g, 0)),
                  kslab, vrow(KVH), kslab, vrow(2 * KVH),
                  pl.BlockSpec((16, tq), lambda b, g, q: (g, b * nq + q)),
                  pl.BlockSpec((2 * sub + 1, tk, L), lambda b, g, q: (0, 0, 0))],
        out_specs=pl.BlockSpec((tq, GW), lambda b, g, q: (b * nq + q, g)),
        out_shape=jax.ShapeDtypeStruct((B * T, NSA_HEADS * NSA_HD), F32),
        compiler_params=_cparams(3), name="nsa_prompt",
    )(qt, kmean, vmt, kvs, vt, kvw, vt, ngt, tri)


def _q_rows(qx_ref):
    T = qx_ref.shape[0]
    parts = [qx_ref[:, (g * GROUP + r) * KV_W:(g * GROUP + r + 1) * KV_W] for r in range(GROUP) for g in range(KVH)]
    return jnp.concatenate(parts, axis=0).astype(BF16)


def _heads_to_row(o, T):
    parts = []
    for g in range(KVH):
        for r in range(GROUP):
            r0 = (r * KVH + g) * T
            parts.append(o[r0:r0 + T, g * NSA_HD:(g + 1) * NSA_HD])
    return jnp.concatenate(parts, axis=1)


def _softmax_lanes(s, mask):
    s = jnp.where(mask, s, NEG)
    e = jnp.where(mask, jnp.exp(s - jnp.max(s, axis=1, keepdims=True)), 0.0)
    return e / jnp.maximum(jnp.sum(e, axis=1, keepdims=True), 1e-30)


def _topk_lanes(score, jid, cand, k):
    sel = jnp.zeros(score.shape, F32)
    s = score
    for _ in range(k):
        m = jnp.max(s, axis=1, keepdims=True)
        first = jnp.min(jnp.where((s == m) & cand, jid, 3.0e38), axis=1, keepdims=True)
        hit = (jid == first) & cand
        sel = jnp.where(hit, 1.0, sel)
        s = jnp.where(hit, -jnp.inf, s)
    return sel


def _online_lanes(carry, s, v_t):
    m, l, acc = carry
    m_new = jnp.maximum(m, jnp.max(s, axis=1, keepdims=True))
    alpha = jnp.exp(m - m_new)
    p = jnp.exp(s - m_new)
    return (m_new, alpha * l + jnp.sum(p, axis=1, keepdims=True),
            alpha * acc + _dot(p.astype(BF16), v_t, NT))


def _s_cmp_kernel(pt_ref, *refs, n_pg, past):
    pages = refs[:n_pg]
    qx_ref, pool_ref, oc_ref, bias_ref, km_scr = refs[n_pg:]
    c = pl.program_id(1)
    nch = pl.num_programs(1)
    T = qx_ref.shape[0]
    per = n_pg * (PAGE // CMP_BLOCK)
    hs = per // 2
    x = jnp.concatenate([pg[0].astype(BF16) for pg in pages], axis=1)
    km_scr[c] = _dot(x, pool_ref[...])

    @pl.when(c == nch - 1)
    def _():
        n_steps = km_scr.shape[0]
        nblk = n_steps * per
        npast = nblk // 2
        gl = KVH * T
        kvm = jnp.concatenate([km_scr[i] for i in range(n_steps)], axis=1)
        qr = _q_rows(qx_ref)
        row = lax.broadcasted_iota(I32, (NSA_HEADS * T, 1), 0)
        qpos = past + (row % T)
        lane = lax.broadcasted_iota(I32, (1, nblk), 1)
        jsel = (lane // per) * hs + (lane % hs)
        odd = (lane // hs) % 2
        vis = ((2 * jsel + odd + 1) * CMP_BLOCK - 1) <= qpos
        p = _softmax_lanes(_dot(qr, kvm[0:KV_W].astype(BF16)), vis)
        o_c = _dot(p.astype(BF16), kvm[KV_W:KV_ROW].astype(BF16), NT)
        oc_ref[...] = _heads_to_row(o_c, T)
        imp = p[0:gl] + p[gl:2 * gl] + p[2 * gl:3 * gl] + p[3 * gl:4 * gl]
        imp = imp + pltpu.roll(imp, nblk - hs, 1)
        imp = jnp.concatenate([imp, jnp.zeros((gl, 128), F32)], axis=1)
        lane2 = lax.broadcasted_iota(I32, (1, nblk + 128), 1)
        jid = jnp.where(lane2 < nblk, (lane2 // per) * hs + (lane2 % hs), npast + lane2 - nblk)
        cand = ((lane2 < nblk) & (((lane2 // hs) % 2) == 0)) | (lane2 == nblk)
        cur = (past + (lax.broadcasted_iota(I32, (gl, 1), 0) % T)) // SEL_BLOCK
        forced = (jid == 0) | (jid == cur)
        score = jnp.where(cand, jnp.where(jid <= cur, imp + jnp.where(forced, FORCE_BONUS, 0.0), NEG), -jnp.inf)
        keep = (_topk_lanes(score, jid.astype(F32), cand, SEL_TOPK) > 0.5) & (score > NEG / 2)
        bias = jnp.where(keep, 0.0, NEG)
        bias = jnp.concatenate([bias] * GROUP, axis=0)
        zpad = jnp.zeros((NSA_HEADS * T, 128 - hs), F32)
        for i in range(n_steps):
            bias_ref[0, i] = jnp.concatenate([bias[:, i * per:i * per + hs], zpad], axis=1)
        bias_ref[0, n_steps] = bias[:, nblk:nblk + 128]


def _pool_matrix(n_pg):
    rows = n_pg * PAGE
    per = rows // CMP_BLOCK
    blk = jnp.arange(rows) // CMP_BLOCK
    col = (blk % 2) * (per // 2) + blk // 2
    return jnp.where(col[:, None] == jnp.arange(per)[None, :], 1.0 / CMP_BLOCK, 0.0).astype(BF16)


def _s_cmp(page_table, pool_t, qx, DB, T, n_pg=32):
    n_pages = page_table.shape[1]
    past = n_pages * PAGE
    n_pg = min(n_pg, n_pages)
    nch = n_pages // n_pg
    per = n_pg * (PAGE // CMP_BLOCK)
    L = NSA_HEADS * T
    page_spec = lambda i: pl.BlockSpec((1, KV_ROW, PAGE), lambda b, c, pt: (pt[b, c * n_pg + i], 0, 0))
    gs = pltpu.PrefetchScalarGridSpec(
        num_scalar_prefetch=1, grid=(DB, nch),
        in_specs=[page_spec(i) for i in range(n_pg)] + [
            pl.BlockSpec((T, NSA_HEADS * KV_W), lambda b, c, pt: (b, 0)),
            pl.BlockSpec((n_pg * PAGE, per), lambda b, c, pt: (0, 0))],
        out_specs=[pl.BlockSpec((T, NSA_HEADS * NSA_HD), lambda b, c, pt: (b, 0)),
                   pl.BlockSpec((1, nch + 1, L, 128), lambda b, c, pt: (b, 0, 0, 0))],
        scratch_shapes=[pltpu.VMEM((nch, KV_ROW, per), F32)])
    return pl.pallas_call(
        functools.partial(_s_cmp_kernel, n_pg=n_pg, past=past), grid_spec=gs,
        out_shape=[jax.ShapeDtypeStruct((DB * T, NSA_HEADS * NSA_HD), F32),
                   jax.ShapeDtypeStruct((DB, nch + 1, L, 128), F32)],
        compiler_params=_cparams(2), name="sample_cmp",
    )(page_table, *([pool_t] * n_pg), qx, _pool_matrix(n_pg))


def _s_attn_kernel(pt_ref, *refs, n_pg, past):
    pages = refs[:n_pg]
    (qx_ref, bias_ref, exp_ref, win_ref, ksn_ref, kwn_ref, oc_ref, gr_ref,
     o_ref, wout_ref, m_scr, l_scr, acc_scr) = refs[n_pg:]
    c = pl.program_id(1)
    n_steps = pl.num_programs(1)
    T = qx_ref.shape[0]
    L = NSA_HEADS * T
    qr = _q_rows(qx_ref)

    @pl.when(c == 0)
    def _():
        m_scr[...] = jnp.full(m_scr.shape, M_INIT, F32)
        l_scr[...] = jnp.zeros_like(l_scr)
        acc_scr[...] = jnp.zeros_like(acc_scr)

    k_t = jnp.concatenate([pg[0, 0:KV_W, :].astype(BF16) for pg in pages], axis=1)
    v_t = jnp.concatenate([pg[0, KV_W:KV_ROW, :].astype(BF16) for pg in pages], axis=1)
    s = _dot(qr, k_t) + _dot(bias_ref[0, c].astype(BF16), exp_ref[...])
    m, l, acc = _online_lanes((m_scr[...], l_scr[...], acc_scr[...]), s, v_t)
    m_scr[...] = m
    l_scr[...] = l
    acc_scr[...] = acc

    @pl.when(c == n_steps - 1)
    def _():
        row = lax.broadcasted_iota(I32, (L, 1), 0)
        tq = row % T
        pad = 128
        jl = lax.broadcasted_iota(I32, (1, pad), 1)
        new_ok = (jl < T) & (jl <= tq)
        zpad = jnp.zeros((pad - T, KV_ROW), F32)

        def new_rows(ref):
            t = jnp.concatenate([ref[...], zpad], axis=0).T
            return t[0:KV_W].astype(BF16), t[KV_W:KV_ROW].astype(BF16), t

        k_t, v_t, _ = new_rows(ksn_ref)
        s = jnp.where(new_ok, _dot(qr, k_t), NEG) + bias_ref[0, n_steps][:, 0:1]
        m_s, l_s, a_s = _online_lanes((m_scr[...], l_scr[...], acc_scr[...]), s, v_t)
        o_s = a_s / jnp.maximum(l_s, 1e-30)
        wb = win_ref.shape[2]
        win = win_ref[0]
        il = lax.broadcasted_iota(I32, (1, wb), 1)
        d = (past + tq) - (past - wb + il)
        init = (jnp.full((L, 1), M_INIT, F32), jnp.zeros((L, 1), F32), jnp.zeros((L, KV_W), F32))
        k_t, v_t, new_t = new_rows(kwn_ref)
        carry = _online_lanes(init, jnp.where(new_ok, _dot(qr, k_t), NEG), v_t)
        s = jnp.where((d >= 0) & (d < WINDOW), _dot(qr, win[0:KV_W].astype(BF16)), NEG)
        m_w, l_w, a_w = _online_lanes(carry, s, win[KV_W:KV_ROW].astype(BF16))
        o_w = a_w / jnp.maximum(l_w, 1e-30)
        W = NSA_HEADS * NSA_HD
        gts = jax.nn.sigmoid(gr_ref[...])
        o_ref[...] = (gts[:, 0:W] * oc_ref[...] + gts[:, W:2 * W] * _heads_to_row(o_s, T)
                      + gts[:, 2 * W:3 * W] * _heads_to_row(o_w, T))
        shifted = pltpu.roll(win, wb - T, 1)
        wout_ref[0, :, 0:wb - 128] = shifted[:, 0:wb - 128]
        tail = jnp.where(jl >= 128 - T, pltpu.roll(new_t, 128 - T, 1), shifted[:, wb - 128:wb])
        wout_ref[0, :, wb - 128:wb] = tail


def _s_attn(page_table, pool_t, qx, bias, win_t, kvs_new, kvw_new, o_c, g_rep, DB, T, n_pg=32):
    n_pages = page_table.shape[1]
    past = n_pages * PAGE
    n_pg = min(n_pg, n_pages)
    nch = n_pages // n_pg
    L = NSA_HEADS * T
    W = NSA_HEADS * NSA_HD
    keys = n_pg * PAGE
    expand = jnp.where(jnp.arange(128)[:, None] == (jnp.arange(keys) // SEL_BLOCK)[None, :], 1.0, 0.0).astype(BF16)
    page_spec = lambda i: pl.BlockSpec((1, KV_ROW, PAGE), lambda b, c, pt: (pt[b, c * n_pg + i], 0, 0))
    rowb = lambda w: pl.BlockSpec((T, w), lambda b, c, pt: (b, 0))
    per_b = lambda a: pl.BlockSpec((1,) + a.shape[1:], lambda b, c, pt: (b,) + (0,) * (a.ndim - 1))
    gs = pltpu.PrefetchScalarGridSpec(
        num_scalar_prefetch=1, grid=(DB, nch),
        in_specs=[page_spec(i) for i in range(n_pg)] + [
            rowb(NSA_HEADS * KV_W), per_b(bias), pl.BlockSpec((128, keys), lambda b, c, pt: (0, 0)),
            per_b(win_t), rowb(KV_ROW), rowb(KV_ROW), rowb(W), rowb(3 * W)],
        out_specs=[rowb(W), per_b(win_t)],
        scratch_shapes=[pltpu.VMEM((L, 1), F32), pltpu.VMEM((L, 1), F32), pltpu.VMEM((L, KV_W), F32)])
    return pl.pallas_call(
        functools.partial(_s_attn_kernel, n_pg=n_pg, past=past), grid_spec=gs,
        out_shape=[jax.ShapeDtypeStruct((DB * T, W), F32), jax.ShapeDtypeStruct(win_t.shape, F32)],
        compiler_params=_cparams(2), name="sample_attn",
    )(page_table, *([pool_t] * n_pg), qx, bias, expand, win_t, kvs_new, kvw_new, o_c, g_rep)


def _prep_weights(w_in):
    o_q = 4 * HG_HEADS * HG_DK
    o_kv = o_q + NSA_HEADS * NSA_HD
    o_ng = o_kv + 6 * KV_W
    o_mg = o_ng + NSA_HEADS * 3
    w_hg = w_in[:, :o_q].astype(BF16)
    w_q = w_in[:, o_q:o_kv] * ATT_SCALE
    w_kv = w_in[:, o_kv:o_ng].astype(BF16)
    w_ng = w_in[:, o_ng:o_mg]
    w_mg = w_in[:, o_mg:].astype(BF16)
    ng4 = w_ng.reshape(D_MODEL, KVH, GROUP, 3).transpose(0, 1, 3, 2)
    ng4 = jnp.pad(ng4.reshape(D_MODEL, KVH, 3 * GROUP), ((0, 0), (0, 0), (0, 16 - 3 * GROUP)))
    vcols = jnp.concatenate([w_in[:, o_kv + (2 * i + 1) * KV_W:o_kv + (2 * i + 2) * KV_W] for i in range(3)], axis=1)
    w_t = jnp.concatenate([w_q * LOG2E, vcols], axis=1).T.astype(BF16)
    w_ngt = ng4.reshape(D_MODEL, KVH * 16).T.astype(BF16)
    wq4 = w_q.reshape(D_MODEL, KVH, GROUP, NSA_HD)
    eye = jnp.eye(KVH, dtype=w_q.dtype)
    w_qx = jnp.einsum('dgrh,gk->dgrkh', wq4, eye).reshape(D_MODEL, NSA_HEADS * KV_W).astype(BF16)
    w_gr = jnp.repeat(w_ng.reshape(D_MODEL, NSA_HEADS, 3).transpose(0, 2, 1), NSA_HD, axis=2)
    w_gr = w_gr.reshape(D_MODEL, 3 * NSA_HEADS * NSA_HD).astype(BF16)
    return w_hg, w_kv, w_mg, w_t, w_ngt, w_qx, w_gr


def kernel(x_prompt, x_sample, cache_cmp_kv, cache_sel_kv, page_table, state_win_kv, state_hgrn, state_conv, w_in, hg_lb_logits, hg_norm_g, w_out, w_up, conv_w, conv_b, w_down, g_pre_mix, g_post_mix, g_pre_ffn, g_post_ffn):
    B, T, D = x_prompt.shape
    DB, TS, _ = x_sample.shape
    depth = w_in.shape[0]
    assert depth == 1 and D == D_MODEL and T % 256 == 0 and TS == 8 and TS < CMP_BLOCK
    n_pool = cache_cmp_kv.shape[1]
    wb = state_win_kv.shape[2]
    C2 = 2 * D_FF
    l = 0

    lbs = jnp.cumsum(jax.nn.softmax(hg_lb_logits.astype(F32), axis=0), axis=0)[l]
    w_hg, w_kv, w_mg, w_t, w_ngt, w_qx, w_gr = _prep_weights(w_in[l])
    w_out_b, w_up_b, w_dn_b = w_out[l].astype(BF16), w_up[l].astype(BF16), w_down[l].astype(BF16)

    outs = {}
    for name, x, nb, tl in (("p", x_prompt, B, T), ("s", x_sample, DB, TS)):
        R = nb * tl
        x2 = x.reshape(R, D)
        h = _rmsnorm_bf16(x2, g_pre_mix[l])
        kv = [_mm(h, w_kv[:, i * KV_ROW:(i + 1) * KV_ROW], F32, name="proj_kv") for i in range(3)]
        s0 = jnp.zeros((nb, HG_HEADS, HG_DK, HG_DK), F32) if name == "p" else state_hgrn[l]
        o_hg, s_hg = _hgrn(x2, g_pre_mix[l], w_hg, lbs, hg_norm_g[l], s0, nb, tl)
        if name == "p":
            qvt = _mm_nt(w_t, h, BF16, name="proj_qv_t")
            ngt = _mm_nt(w_ngt, h, F32, tn=64, name="proj_gate_t")
            kmean, vmt = _cmp_prep(kv[0], qvt[NSA_HEADS * NSA_HD:], nb, tl)
            o_nsa = _nsa_prompt(qvt, kmean, vmt, kv[1], kv[2], qvt[NSA_HEADS * NSA_HD:], ngt, nb, tl)
            win_new = kv[2].reshape(nb, tl, KV_ROW)[:, tl - min(WINDOW, tl):]
            cbuf0 = jnp.zeros((nb, CONV_W - 1, C2), F32)
        else:
            qx = _mm(h, w_qx, F32, name="proj_q_pad")
            g_rep = _mm(h, w_gr, F32, name="proj_gate_rep")
            fmaj = lambda a: a.transpose(0, 2, 3, 4, 1).reshape(a.shape[0], KV_ROW, a.shape[1])
            o_c, bias = _s_cmp(page_table, fmaj(cache_cmp_kv[l]), qx, nb, tl)
            o_nsa, win_t = _s_attn(page_table, fmaj(cache_sel_kv[l]), qx, bias, fmaj(state_win_kv[l]),
                                   kv[1], kv[2], o_c, g_rep, nb, tl)
            win_new = win_t.reshape(nb, 2, KVH, NSA_HD, wb).transpose(0, 4, 1, 2, 3).reshape(nb, wb, KV_ROW)
            cbuf0 = state_conv[l]
        x1 = _merge(x2, g_pre_mix[l], w_mg, o_hg, o_nsa, w_out_b, g_post_mix[l])
        x3, cbuf = _ffn(x1, nb, g_pre_ffn[l], w_up_b, conv_w[l], conv_b[l], w_dn_b, g_post_ffn[l], cbuf0, tm=512)
        kv6 = lambda a, n: a.reshape(1, nb, n, 2, KVH, NSA_HD)
        outs[name] = (x3.reshape(nb, tl, D), kv6(kv[0], tl), kv6(kv[1], tl), kv6(win_new, win_new.shape[1]),
                      s_hg[None], cbuf[None])
    p, s = outs["p"], outs["s"]
    return (p[0], s[0], p[1], s[1], p[2], s[2], p[3], s[3], p[4], s[4], p[5], s[5])
```

```python
import functools
import math

import jax
import jax.numpy as jnp
from jax import lax
from jax.experimental import pallas as pl
from jax.experimental.pallas import tpu as pltpu

F32 = jnp.float32
BF16 = jnp.bfloat16
I32 = jnp.int32

D_MODEL = 1024
HG_HEADS = 8
HG_DK = 128
HG_CHUNK = 32
NSA_HEADS = 16
NSA_HD = 64
KVH = 4
GROUP = 4
KV_W = KVH * NSA_HD
KV_ROW = 2 * KV_W
CMP_BLOCK = 32
SEL_BLOCK = 64
SEL_TOPK = 8
WINDOW = 512
FORCE_BONUS = 1.0e4
ATT_SCALE = NSA_HD ** -0.5
LOG2E = math.log2(math.e)
D_FF = 2816
CONV_W = 3
PAGE = 128
EPS = 1e-6
NEG = -1e30
M_INIT = -1e38

VMEM_LIMIT = 56 * 1024 * 1024

NT = (((1,), (1,)), ((), ()))
TN = (((0,), (0,)), ((), ()))


def _cparams(n_axes):
    return pltpu.CompilerParams(dimension_semantics=("arbitrary",) * n_axes,
                                vmem_limit_bytes=VMEM_LIMIT)


def _dot(a, b, dims=None):
    if dims is None:
        return jnp.dot(a, b, preferred_element_type=F32)
    return lax.dot_general(a, b, dims, preferred_element_type=F32)


def _rms(x, g):
    return x * lax.rsqrt(jnp.mean(x * x, axis=-1, keepdims=True) + EPS) * g


def _rmsnorm_kernel(x_ref, g_ref, o_ref):
    o_ref[...] = _rms(x_ref[...], g_ref[...]).astype(o_ref.dtype)


def _rmsnorm_bf16(x, g, tm=512):
    R, D = x.shape
    tm = min(tm, R)
    return pl.pallas_call(
        _rmsnorm_kernel, grid=(R // tm,),
        in_specs=[pl.BlockSpec((tm, D), lambda i: (i, 0)), pl.BlockSpec((1, D), lambda i: (0, 0))],
        out_specs=pl.BlockSpec((tm, D), lambda i: (i, 0)),
        out_shape=jax.ShapeDtypeStruct((R, D), BF16),
        compiler_params=_cparams(1), name="rmsnorm",
    )(x, g.reshape(1, D))


def _mm_kernel(a_ref, w_ref, o_ref):
    o_ref[...] = _dot(a_ref[...], w_ref[...]).astype(o_ref.dtype)


def _mm(a, w, out_dtype, tm=1024, tn=512, name="proj"):
    R, K = a.shape
    N = w.shape[1]
    tm, tn = min(tm, R), min(tn, N)
    return pl.pallas_call(
        _mm_kernel, grid=(R // tm, N // tn),
        in_specs=[pl.BlockSpec((tm, K), lambda i, j: (i, 0)), pl.BlockSpec((K, tn), lambda i, j: (0, j))],
        out_specs=pl.BlockSpec((tm, tn), lambda i, j: (i, j)),
        out_shape=jax.ShapeDtypeStruct((R, N), out_dtype),
        compiler_params=_cparams(2), name=name,
    )(a, w)


def _mm_nt_kernel(wt_ref, a_ref, o_ref):
    o_ref[...] = _dot(wt_ref[...], a_ref[...], NT).astype(o_ref.dtype)


def _mm_nt(wt, a, out_dtype, tm=1024, tn=256, name="proj_t"):
    N, K = wt.shape
    R = a.shape[0]
    tm, tn = min(tm, R), min(tn, N)
    return pl.pallas_call(
        _mm_nt_kernel, grid=(R // tm, N // tn),
        in_specs=[pl.BlockSpec((tn, K), lambda i, j: (j, 0)), pl.BlockSpec((tm, K), lambda i, j: (i, 0))],
        out_specs=pl.BlockSpec((tn, tm), lambda i, j: (j, i)),
        out_shape=jax.ShapeDtypeStruct((N, R), out_dtype),
        compiler_params=_cparams(2), name=name,
    )(wt, a)


def _merge_kernel(x_ref, gpre_ref, wmg_ref, ohg_ref, onsa_ref, w_ref, g_ref, o_ref):
    x = x_ref[...]
    mg = _dot(_rms(x, gpre_ref[...]).astype(BF16), wmg_ref[...])
    u = jax.nn.sigmoid(mg[:, :D_MODEL]) * ohg_ref[...] + jax.nn.sigmoid(mg[:, D_MODEL:]) * onsa_ref[...]
    y = _dot(u.astype(BF16), w_ref[...])
    o_ref[...] = x + _rms(y, g_ref[...])


def _merge(x, g_pre, w_mg, ohg, onsa, w_out, g_post, tm=512):
    R, D = x.shape
    tm = min(tm, R)
    row = pl.BlockSpec((tm, D), lambda i: (i, 0))
    const = lambda shape: pl.BlockSpec(shape, lambda i: (0, 0), pipeline_mode=pl.Buffered(1))
    return pl.pallas_call(
        _merge_kernel, grid=(R // tm,),
        in_specs=[row, const((1, D)), const((D, 2 * D)), row, row, const((D, D)), const((1, D))],
        out_specs=row,
        out_shape=jax.ShapeDtypeStruct((R, D), F32),
        compiler_params=_cparams(1), name="merge",
    )(x, g_pre.reshape(1, D), w_mg, ohg, onsa, w_out, g_post.reshape(1, D))


def _ffn_kernel(x_ref, gpre_ref, wup_ref, cw_ref, cb_ref, wdn_ref, gpost_ref, buf_ref,
                o_ref, cbuf_ref, act_scr, carry_scr, *, ns, tn):
    tm = x_ref.shape[0]
    ls = tm // ns
    x = x_ref[...]
    h = _rms(x, gpre_ref[...]).astype(BF16)
    pos = lax.broadcasted_iota(I32, (tm, 1), 0) % ls
    if ns == 1:
        @pl.when(pl.program_id(1) == 0)
        def _():
            carry_scr[...] = buf_ref[0]
    for j in range(D_FF // tn):
        halves = []
        for base in (j * tn, D_FF + j * tn):
            cols = slice(base, base + tn)
            up = _dot(h, wup_ref[:, cols])
            if ns == 1:
                b0 = carry_scr[0:1, cols]
                b1 = carry_scr[1:2, cols]
                carry_scr[:, cols] = up[tm - 2:tm, :]
                cbuf_ref[0, :, cols] = up[tm - 2:tm, :]
            else:
                bufv = buf_ref[:, :, cols]
                b0 = jnp.broadcast_to(bufv[:, 0:1, :], (ns, ls, tn)).reshape(tm, tn)
                b1 = jnp.broadcast_to(bufv[:, 1:2, :], (ns, ls, tn)).reshape(tm, tn)
                cbuf_ref[:, :, cols] = up.reshape(ns, ls, tn)[:, ls - 2:ls, :]
            def conv(x, x1, x2):
                c = cb_ref[:, cols] + x2 * cw_ref[0:1, cols]
                c = c + x1 * cw_ref[1:2, cols]
                return c + x * cw_ref[2:3, cols]

            def fix(p, x1, x2):
                return jnp.where(p == 0, b1, x1), jnp.where(p == 0, b0, jnp.where(p == 1, b1, x2))

            r1, r2 = pltpu.roll(up, 1, 0), pltpu.roll(up, 2, 0)
            if ns == 1:
                head = conv(up[0:8], *fix(pos[0:8], r1[0:8], r2[0:8]))
                c = jnp.concatenate([head, conv(up, r1, r2)[8:]], axis=0)
            else:
                c = conv(up, *fix(pos, r1, r2))
            halves.append(c)
        act_scr[:, j * tn:(j + 1) * tn] = (jax.nn.silu(halves[0]) * halves[1]).astype(BF16)
    o_ref[...] = x + _rms(_dot(act_scr[...], wdn_ref[...]), gpost_ref[...])


def _ffn(x, n_seq, g_pre, w_up, conv_w, conv_b, w_down, g_post, buf, tm):
    R, D = x.shape
    L = R // n_seq
    C2 = 2 * D_FF
    if L >= tm:
        ns, nt = 1, L // tm
        grid = (n_seq, nt)
        buf_spec = pl.BlockSpec((1, 2, C2), lambda b, t: (b, 0, 0))
    else:
        ns, nt, tm = n_seq, 1, R
        grid = (1, 1)
        buf_spec = pl.BlockSpec((ns, 2, C2), lambda b, t: (0, 0, 0))
    const = lambda shape: pl.BlockSpec(shape, lambda b, t: (0,) * len(shape), pipeline_mode=pl.Buffered(1))
    return pl.pallas_call(
        functools.partial(_ffn_kernel, ns=ns, tn=256), grid=grid,
        in_specs=[pl.BlockSpec((tm, D), lambda b, t: (b * nt + t, 0)), const((1, D)), const((D, C2)),
                  const((CONV_W, C2)), const((1, C2)), const((D_FF, D)), const((1, D)), buf_spec],
        out_specs=[pl.BlockSpec((tm, D), lambda b, t: (b * nt + t, 0)), buf_spec],
        out_shape=[jax.ShapeDtypeStruct((R, D), F32), jax.ShapeDtypeStruct((n_seq, 2, C2), F32)],
        scratch_shapes=[pltpu.VMEM((tm, D_FF), BF16), pltpu.VMEM((2, C2), F32)],
        compiler_params=_cparams(2), name="conv_ffn",
    )(x, g_pre.reshape(1, D), w_up, conv_w, conv_b.reshape(1, C2), w_down, g_post.reshape(1, D), buf)


def _pad_rows(x, n):
    if x.shape[0] >= n:
        return x
    return jnp.concatenate([x, jnp.zeros((n - x.shape[0],) + x.shape[1:], x.dtype)], axis=0)


def _hgrn_kernel(x_ref, gpre_ref, w_ref, lb_ref, ng_ref, s0_ref, o_ref, so_ref, st_scr, *, chunk):
    t = pl.program_id(1)
    tc = x_ref.shape[0]
    hp = HG_HEADS
    W = hp * HG_DK
    C = chunk
    zh = _dot(_rms(x_ref[...], gpre_ref[...]).astype(BF16), w_ref[...])
    zq, zf, zi, zg = (zh[:, s * W:(s + 1) * W] for s in range(4))

    @pl.when(t == 0)
    def _():
        for hh in range(hp):
            st_scr[hh] = s0_ref[0, hh].T

    tp = max(tc, 16)
    n_ch = tc // C
    r = lax.broadcasted_iota(I32, (tp, tp), 0)
    c = lax.broadcasted_iota(I32, (tp, tp), 1)
    causal = (c <= r) & ((r // C) == (c // C))
    lmat = jnp.where(causal, 1.0, 0.0).astype(BF16)
    in_blk = ((lax.broadcasted_iota(I32, (tp, n_ch * HG_DK), 0) // C)
              == (lax.broadcasted_iota(I32, (tp, n_ch * HG_DK), 1) // HG_DK))

    lb = lb_ref[...]
    z = _pad_rows(zf, tp)
    logf = jnp.log(lb + (1.0 - lb) * jax.nn.sigmoid(z))
    k = (1.0 - lb) * jax.nn.sigmoid(-z)
    q = jax.nn.silu(_pad_rows(zq, tp))
    hi = logf.astype(BF16)
    r1 = logf - hi.astype(F32)
    mid = r1.astype(BF16)
    lo = (r1 - mid.astype(F32)).astype(BF16)
    b = _dot(lmat, hi) + _dot(lmat, mid) + _dot(lmat, lo)
    last = [b[(ci + 1) * C - 1:(ci + 1) * C, :] for ci in range(tp // C)]
    bl = jnp.concatenate([jnp.broadcast_to(x, (C, x.shape[1])) for x in last], axis=0)
    q_in = q * jnp.exp(b)
    k_out = k * jnp.exp(bl - b)
    q_rel = (q * jnp.exp(b - bl)).astype(BF16)
    decay = [jnp.exp(x) for x in last[:n_ch]]
    q_in = q_in.astype(BF16)

    for hh in range(hp):
        lanes = slice(hh * HG_DK, (hh + 1) * HG_DK)
        vb = _pad_rows(zi[:, lanes], tp).astype(BF16)
        ko = k_out[:, lanes]
        attn = jnp.where(causal, _dot(q_rel[:, lanes], ko.astype(BF16), NT), 0.0)
        o = _dot(attn.astype(BF16), vb)
        kx = jnp.where(in_blk, jnp.concatenate([ko] * n_ch, axis=1), 0.0).astype(BF16)
        ut_all = _dot(vb, kx, TN)
        st = st_scr[hh]
        inter = []
        for ci in range(n_ch):
            rows = slice(ci * C, min((ci + 1) * C, tp) if C >= 16 else ci * C + 16)
            inter.append(_dot(q_in[rows, lanes], st.astype(BF16), NT)[0:C])
            st = decay[ci][:, lanes] * st + ut_all[:, ci * HG_DK:(ci + 1) * HG_DK]
        o = (o[0:tc] + (inter[0] if n_ch == 1 else jnp.concatenate(inter, axis=0)))
        st_scr[hh] = st
        o_ref[:, lanes] = _rms(o, ng_ref[...]) * jax.nn.silu(zg[:, lanes])

        @pl.when(t == pl.num_programs(1) - 1)
        def _():
            so_ref[0, hh] = st.T


def _hgrn(x, g_pre, w_hg, lbs, norm_g, s0, B, T):
    C = math.gcd(T, HG_CHUNK)
    tc = min(T, 256)
    nt = T // tc
    D = x.shape[1]
    W = HG_HEADS * HG_DK
    const = lambda shape: pl.BlockSpec(shape, lambda b, t: (0, 0), pipeline_mode=pl.Buffered(1))
    st_spec = pl.BlockSpec((1, HG_HEADS, HG_DK, HG_DK), lambda b, t: (b, 0, 0, 0))
    return pl.pallas_call(
        functools.partial(_hgrn_kernel, chunk=C), grid=(B, nt),
        in_specs=[pl.BlockSpec((tc, D), lambda b, t: (b * nt + t, 0)), const((1, D)), const((D, 4 * W)),
                  const((1, W)), const((1, HG_DK)), st_spec],
        out_specs=[pl.BlockSpec((tc, W), lambda b, t: (b * nt + t, 0)), st_spec],
        out_shape=[jax.ShapeDtypeStruct((B * T, W), F32),
                   jax.ShapeDtypeStruct((B, HG_HEADS, HG_DK, HG_DK), F32)],
        scratch_shapes=[pltpu.VMEM((HG_HEADS, HG_DK, HG_DK), F32)],
        compiler_params=_cparams(2), name="hgrn2",
    )(x, g_pre.reshape(1, D), w_hg, lbs.reshape(1, W), norm_g.reshape(1, HG_DK), s0)


def _topk_rows(score, k):
    n = score.shape[0]
    rid = lax.broadcasted_iota(I32, score.shape, 0)
    sel = jnp.zeros(score.shape, F32)
    s = score
    for _ in range(k):
        m = jnp.max(s, axis=0, keepdims=True)
        first = jnp.min(jnp.where(s == m, rid, n), axis=0, keepdims=True)
        hit = rid == first
        sel = jnp.where(hit, 1.0, sel)
        s = jnp.where(hit, -jnp.inf, s)
    return sel


def _softmax2_rows(s, mask):
    s = jnp.where(mask, s, NEG)
    e = jnp.where(mask, jnp.exp2(s - jnp.max(s, axis=0, keepdims=True)), 0.0)
    return e / jnp.maximum(jnp.sum(e, axis=0, keepdims=True), 1e-30)


def _online2(carry, s, v_aug):
    m, acc = carry
    m_new = jnp.maximum(m, jnp.max(s, axis=0, keepdims=True))
    p = jnp.exp2(s - m_new)
    return m_new, jnp.exp2(m - m_new) * acc + _dot(v_aug, p.astype(BF16))


def _block_means_even_odd(x):
    s = 1.0 / CMP_BLOCK
    return (x[:, 0:CMP_BLOCK, :].sum(axis=1) * s, x[:, CMP_BLOCK:SEL_BLOCK, :].sum(axis=1) * s)


def _cmp_prep_kernel(kvc_ref, vct_ref, km_ref, vmt_ref):
    hb = kvc_ref.shape[0]
    nb = 2 * hb
    T = hb * SEL_BLOCK
    ev, od = _block_means_even_odd(kvc_ref[:, :, 0:KV_W])
    km_ref[0:hb, :] = ev
    km_ref[hb:nb, :] = od
    ti = lax.broadcasted_iota(I32, (T, nb), 0)
    ci = lax.broadcasted_iota(I32, (T, nb), 1)
    blk = jnp.where(ci < hb, 2 * ci, 2 * (ci - hb) + 1)
    pool = jnp.where((ti // CMP_BLOCK) == blk, 1.0 / CMP_BLOCK, 0.0).astype(BF16)
    vmt_ref[...] = _dot(vct_ref[...], pool)


def _cmp_prep(kvc, qvt, B, T):
    nb = T // CMP_BLOCK
    hb = T // SEL_BLOCK
    vc_blk = NSA_HEADS * NSA_HD // KV_W
    return pl.pallas_call(
        _cmp_prep_kernel, grid=(B,),
        in_specs=[pl.BlockSpec((hb, SEL_BLOCK, KV_ROW), lambda b: (b, 0, 0)),
                  pl.BlockSpec((KV_W, T), lambda b: (vc_blk, b))],
        out_specs=[pl.BlockSpec((nb, KV_W), lambda b: (b, 0)), pl.BlockSpec((KV_W, nb), lambda b: (b, 0))],
        out_shape=[jax.ShapeDtypeStruct((B * nb, KV_W), F32), jax.ShapeDtypeStruct((B * KV_W, nb), F32)],
        compiler_params=_cparams(1), name="cmp_prep",
    )(kvc.reshape(B * hb, SEL_BLOCK, KV_ROW), qvt)


def _nsa_prompt_kernel(qt_ref, km_ref, vmt_ref, ks_ref, vst_ref, kw_ref, vwt_ref, ngt_ref, tri_ref, o_ref,
                       *, tq, tk):
    g = pl.program_id(1)
    qi = pl.program_id(2)
    par = g % 2
    L = GROUP * tq
    nblk = km_ref.shape[0]
    nsel = nblk // 2

    qt = qt_ref[...]
    qs = jnp.concatenate([qt[r * NSA_HD:(r + 1) * NSA_HD, :] for r in range(GROUP)], axis=1)
    zq = jnp.zeros_like(qs)
    qp = jnp.concatenate([jnp.where(par == 0, qs, zq), jnp.where(par == 1, qs, zq)], axis=0)

    lane = lax.broadcasted_iota(I32, (1, L), 1)
    qpos = qi * tq + (lane % tq)

    row = lax.broadcasted_iota(I32, (nblk, 1), 0)
    cblk = jnp.where(row < nsel, 2 * row, 2 * (row - nsel) + 1)
    vis = ((cblk + 1) * CMP_BLOCK - 1) <= qpos
    p_c = _softmax2_rows(_dot(km_ref[...].astype(BF16), qp), vis)
    o_c = _dot(vmt_ref[...].astype(BF16), p_c.astype(BF16))
    imp = p_c[:, 0:tq]
    for r in range(1, GROUP):
        imp = imp + p_c[:, r * tq:(r + 1) * tq]
    imp = imp[0:nsel] + imp[nsel:nblk]
    sblk = lax.broadcasted_iota(I32, (nsel, 1), 0)
    cur = qpos[:, 0:tq] // SEL_BLOCK
    forced = (sblk == 0) | (sblk == cur)
    score = jnp.where(sblk <= cur, imp + jnp.where(forced, FORCE_BONUS, 0.0), NEG)
    keep = (_topk_rows(score, min(SEL_TOPK, nsel)) > 0.5) & (score > NEG / 2)
    bias = jnp.where(keep, 0.0, NEG)
    bias = jnp.concatenate([bias] * GROUP, axis=1).astype(BF16)
    qa = jnp.concatenate([qp, bias, jnp.zeros((128 - nsel, L), BF16)], axis=0)

    NR = NSA_HD + 16
    init = (jnp.full((1, L), M_INIT, F32), jnp.zeros((NR, L), F32))
    sub = tq // tk
    spt = tk // SEL_BLOCK
    wt = WINDOW // tk
    ones = jnp.ones((16, tk), BF16)
    kblk = lax.broadcasted_iota(I32, (tk, 1), 0) // SEL_BLOCK
    lane128 = lax.broadcasted_iota(I32, (1, 128), 1)

    def tile(k_ref, vt_ref, kt, with_bias):
        start = pl.multiple_of(kt * tk, tk)
        kb = k_ref[pl.ds(start, tk), :].astype(BF16)
        if with_bias:
            hot = jnp.where(lane128 == kt * spt + kblk, 1.0, 0.0).astype(BF16)
            s = _dot(jnp.concatenate([kb, hot], axis=1), qa)
        else:
            s = _dot(kb, qp)
        return s, jnp.concatenate([vt_ref[:, pl.ds(start, tk)], ones], axis=0)

    def finish(acc):
        return acc[0:NSA_HD] / jnp.maximum(acc[NSA_HD:NSA_HD + 1], 1e-30)

    def branch(k_ref, vt_ref, with_bias, first, n_prev, far_mask):
        carry = init
        for j in range(sub):
            s, v_a = tile(k_ref, vt_ref, qi * sub + j, with_bias)
            carry = _online2(carry, s + tri_ref[j], v_a)

        def one(kt, c):
            s, v_a = tile(k_ref, vt_ref, kt, with_bias)
            if far_mask:
                j = kt - (qi * sub - wt)
                s = s + tri_ref[jnp.where((j >= 0) & (j < sub), sub + 1 + j, sub)]
            return _online2(c, s, v_a)

        carry = lax.fori_loop(0, n_prev // 2, lambda i, c: one(first + 2 * i + 1, one(first + 2 * i, c)), carry)
        if sub % 2:
            carry = lax.cond(n_prev % 2 == 1, lambda c: one(first + n_prev - 1, c), lambda c: c, carry)
        return finish(carry[1])

    o_s = branch(ks_ref, vst_ref, True, 0, qi * sub, False)
    n_w = jnp.minimum(qi * sub, wt)
    o_w = branch(kw_ref, vwt_ref, False, qi * sub - n_w, n_w, True)

    gts = jax.nn.sigmoid(ngt_ref[...])
    grow = lambda i: jnp.concatenate([gts[i * GROUP + r:i * GROUP + r + 1, :] for r in range(GROUP)], axis=1)
    o = grow(0) * o_c + grow(1) * o_s + grow(2) * o_w
    for pr in range(GROUP // 2):
        two = jnp.concatenate([o[:, (2 * pr) * tq:(2 * pr + 1) * tq], o[:, (2 * pr + 1) * tq:(2 * pr + 2) * tq]], axis=0)
        o_ref[:, pr * 128:(pr + 1) * 128] = two.T


def _nsa_prompt(qt, kmean, vmt, kvs, kvw, ngt, B, T, tq=512, tk=256):
    tq = min(tq, T)
    nq = T // tq
    nblk = T // CMP_BLOCK
    nsel = nblk // 2
    L = GROUP * tq
    GW = GROUP * NSA_HD
    sub = tq // tk
    assert tq % tk == 0 and tk % SEL_BLOCK == 0 and WINDOW % tq == 0 and nsel <= 128
    vrow = lambda i: pl.BlockSpec((NSA_HD, T), lambda b, g, q: (NSA_HEADS + i * KVH + g, b))
    kslab = pl.BlockSpec((T, 128), lambda b, g, q: (b, g // 2))
    qc = (jnp.arange(L) % tq)[None, :]
    kr = [jnp.arange(tk)[:, None] + j * tk for j in range(sub)]
    tri = jnp.stack([jnp.where(k <= qc, 0.0, NEG) for k in kr] + [jnp.zeros((tk, L))]
                    + [jnp.where(k > qc, 0.0, NEG) for k in kr]).astype(F32)
    return pl.pallas_call(
        functools.partial(_nsa_prompt_kernel, tq=tq, tk=tk), grid=(B, KVH, nq),
        in_specs=[pl.BlockSpec((GW, tq), lambda b, g, q: (g, b * nq + q)),
                  pl.BlockSpec((nblk, 128), lambda b, g, q: (b, g // 2)),
                  pl.BlockSpec((NSA_HD, nblk), lambda b, g, q: (b * KVH + g, 0)),
                  kslab, vrow(1), kslab, vrow(2),
                  pl.BlockSpec((16, tq), lambda b, g, q: (g, b * nq + q)),
                  pl.BlockSpec((2 * sub + 1, tk, L), lambda b, g, q: (0, 0, 0), pipeline_mode=pl.Buffered(1))],
        out_specs=pl.BlockSpec((tq, GW), lambda b, g, q: (b * nq + q, g)),
        out_shape=jax.ShapeDtypeStruct((B * T, NSA_HEADS * NSA_HD), F32),
        compiler_params=_cparams(3), name="nsa_prompt",
    )(qt, kmean, vmt, kvs, qt, kvw, qt, ngt, tri)


def _q_rows(qx_ref):
    T = qx_ref.shape[0]
    parts = [qx_ref[:, (g * GROUP + r) * KV_W:(g * GROUP + r + 1) * KV_W] for r in range(GROUP) for g in range(KVH)]
    return jnp.concatenate(parts, axis=0).astype(BF16)


def _heads_to_row(o, T):
    parts = []
    for g in range(KVH):
        for r in range(GROUP):
            r0 = (r * KVH + g) * T
            parts.append(o[r0:r0 + T, g * NSA_HD:(g + 1) * NSA_HD])
    return jnp.concatenate(parts, axis=1)


def _softmax_lanes(s, mask):
    s = jnp.where(mask, s, NEG)
    e = jnp.where(mask, jnp.exp(s - jnp.max(s, axis=1, keepdims=True)), 0.0)
    return e / jnp.maximum(jnp.sum(e, axis=1, keepdims=True), 1e-30)


def _topk_lanes(score, jid, cand, k):
    sel = jnp.zeros(score.shape, F32)
    s = score
    for _ in range(k):
        m = jnp.max(s, axis=1, keepdims=True)
        first = jnp.min(jnp.where((s == m) & cand, jid, 3.0e38), axis=1, keepdims=True)
        hit = (jid == first) & cand
        sel = jnp.where(hit, 1.0, sel)
        s = jnp.where(hit, -jnp.inf, s)
    return sel


def _online_lanes(carry, s, v_t):
    m, l, acc = carry
    m_new = jnp.maximum(m, jnp.max(s, axis=1, keepdims=True))
    alpha = jnp.exp(m - m_new)
    p = jnp.exp(s - m_new)
    return (m_new, alpha * l + jnp.sum(p, axis=1, keepdims=True),
            alpha * acc + _dot(p.astype(BF16), v_t, NT))


def _s_cmp_kernel(pt_ref, *refs, n_pg, past):
    pages = refs[:n_pg]
    qx_ref, pool_ref, oc_ref, bias_ref, km_scr = refs[n_pg:]
    c = pl.program_id(1)
    nch = pl.num_programs(1)
    T = qx_ref.shape[0]
    per = n_pg * (PAGE // CMP_BLOCK)
    hs = per // 2
    x = jnp.concatenate([pg[0].astype(BF16) for pg in pages], axis=1)
    km_scr[c] = _dot(x, pool_ref[...])

    @pl.when(c == nch - 1)
    def _():
        n_steps = km_scr.shape[0]
        nblk = n_steps * per
        npast = nblk // 2
        gl = KVH * T
        kvm = jnp.concatenate([km_scr[i] for i in range(n_steps)], axis=1)
        qr = _q_rows(qx_ref)
        row = lax.broadcasted_iota(I32, (NSA_HEADS * T, 1), 0)
        qpos = past + (row % T)
        lane = lax.broadcasted_iota(I32, (1, nblk), 1)
        jsel = (lane // per) * hs + (lane % hs)
        odd = (lane // hs) % 2
        vis = ((2 * jsel + odd + 1) * CMP_BLOCK - 1) <= qpos
        p = _softmax_lanes(_dot(qr, kvm[0:KV_W].astype(BF16)), vis)
        o_c = _dot(p.astype(BF16), kvm[KV_W:KV_ROW].astype(BF16), NT)
        oc_ref[...] = _heads_to_row(o_c, T)
        imp = p[0:gl] + p[gl:2 * gl] + p[2 * gl:3 * gl] + p[3 * gl:4 * gl]
        imp = imp + pltpu.roll(imp, nblk - hs, 1)
        imp = jnp.concatenate([imp, jnp.zeros((gl, 128), F32)], axis=1)
        lane2 = lax.broadcasted_iota(I32, (1, nblk + 128), 1)
        jid = jnp.where(lane2 < nblk, (lane2 // per) * hs + (lane2 % hs), npast + lane2 - nblk)
        cand = ((lane2 < nblk) & (((lane2 // hs) % 2) == 0)) | (lane2 == nblk)
        cur = (past + (lax.broadcasted_iota(I32, (gl, 1), 0) % T)) // SEL_BLOCK
        forced = (jid == 0) | (jid == cur)
        score = jnp.where(cand, jnp.where(jid <= cur, imp + jnp.where(forced, FORCE_BONUS, 0.0), NEG), -jnp.inf)
        keep = (_topk_lanes(score, jid.astype(F32), cand, SEL_TOPK) > 0.5) & (score > NEG / 2)
        bias = jnp.where(keep, 0.0, NEG)
        bias = jnp.concatenate([bias] * GROUP, axis=0)
        zpad = jnp.zeros((NSA_HEADS * T, 128 - hs), F32)
        for i in range(n_steps):
            bias_ref[0, i] = jnp.concatenate([bias[:, i * per:i * per + hs], zpad], axis=1)
        bias_ref[0, n_steps] = bias[:, nblk:nblk + 128]


def _pool_matrix(n_pg):
    rows = n_pg * PAGE
    per = rows // CMP_BLOCK
    blk = jnp.arange(rows) // CMP_BLOCK
    col = (blk % 2) * (per // 2) + blk // 2
    return jnp.where(col[:, None] == jnp.arange(per)[None, :], 1.0 / CMP_BLOCK, 0.0).astype(BF16)


def _s_cmp(page_table, pool_t, qx, DB, T, n_pg=32):
    n_pages = page_table.shape[1]
    past = n_pages * PAGE
    n_pg = min(n_pg, n_pages)
    nch = n_pages // n_pg
    per = n_pg * (PAGE // CMP_BLOCK)
    L = NSA_HEADS * T
    page_spec = lambda i: pl.BlockSpec((1, KV_ROW, PAGE), lambda b, c, pt: (pt[b, c * n_pg + i], 0, 0))
    gs = pltpu.PrefetchScalarGridSpec(
        num_scalar_prefetch=1, grid=(DB, nch),
        in_specs=[page_spec(i) for i in range(n_pg)] + [
            pl.BlockSpec((T, NSA_HEADS * KV_W), lambda b, c, pt: (b, 0)),
            pl.BlockSpec((n_pg * PAGE, per), lambda b, c, pt: (0, 0))],
        out_specs=[pl.BlockSpec((T, NSA_HEADS * NSA_HD), lambda b, c, pt: (b, 0)),
                   pl.BlockSpec((1, nch + 1, L, 128), lambda b, c, pt: (b, 0, 0, 0))],
        scratch_shapes=[pltpu.VMEM((nch, KV_ROW, per), F32)])
    return pl.pallas_call(
        functools.partial(_s_cmp_kernel, n_pg=n_pg, past=past), grid_spec=gs,
        out_shape=[jax.ShapeDtypeStruct((DB * T, NSA_HEADS * NSA_HD), F32),
                   jax.ShapeDtypeStruct((DB, nch + 1, L, 128), F32)],
        compiler_params=_cparams(2), name="sample_cmp",
    )(page_table, *([pool_t] * n_pg), qx, _pool_matrix(n_pg))


def _s_attn_kernel(pt_ref, *refs, n_pg, past):
    pages = refs[:n_pg]
    (qx_ref, bias_ref, exp_ref, win_ref, ksn_ref, kwn_ref, oc_ref, gr_ref,
     o_ref, wout_ref, m_scr, l_scr, acc_scr) = refs[n_pg:]
    c = pl.program_id(1)
    n_steps = pl.num_programs(1)
    T = qx_ref.shape[0]
    L = NSA_HEADS * T
    qr = _q_rows(qx_ref)

    @pl.when(c == 0)
    def _():
        m_scr[...] = jnp.full(m_scr.shape, M_INIT, F32)
        l_scr[...] = jnp.zeros_like(l_scr)
        acc_scr[...] = jnp.zeros_like(acc_scr)

    k_t = jnp.concatenate([pg[0, 0:KV_W, :].astype(BF16) for pg in pages], axis=1)
    v_t = jnp.concatenate([pg[0, KV_W:KV_ROW, :].astype(BF16) for pg in pages], axis=1)
    s = _dot(qr, k_t) + _dot(bias_ref[0, c].astype(BF16), exp_ref[...])
    m, l, acc = _online_lanes((m_scr[...], l_scr[...], acc_scr[...]), s, v_t)
    m_scr[...] = m
    l_scr[...] = l
    acc_scr[...] = acc

    @pl.when(c == n_steps - 1)
    def _():
        row = lax.broadcasted_iota(I32, (L, 1), 0)
        tq = row % T
        pad = 128
        jl = lax.broadcasted_iota(I32, (1, pad), 1)
        new_ok = (jl < T) & (jl <= tq)
        zpad = jnp.zeros((pad - T, KV_ROW), F32)

        def new_rows(ref):
            t = jnp.concatenate([ref[...], zpad], axis=0).T
            return t[0:KV_W].astype(BF16), t[KV_W:KV_ROW].astype(BF16), t

        k_t, v_t, _ = new_rows(ksn_ref)
        s = jnp.where(new_ok, _dot(qr, k_t), NEG) + bias_ref[0, n_steps][:, 0:1]
        m_s, l_s, a_s = _online_lanes((m_scr[...], l_scr[...], acc_scr[...]), s, v_t)
        o_s = a_s / jnp.maximum(l_s, 1e-30)
        wb = win_ref.shape[2]
        win = win_ref[0]
        il = lax.broadcasted_iota(I32, (1, wb), 1)
        d = (past + tq) - (past - wb + il)
        init = (jnp.full((L, 1), M_INIT, F32), jnp.zeros((L, 1), F32), jnp.zeros((L, KV_W), F32))
        k_t, v_t, new_t = new_rows(kwn_ref)
        carry = _online_lanes(init, jnp.where(new_ok, _dot(qr, k_t), NEG), v_t)
        s = jnp.where((d >= 0) & (d < WINDOW), _dot(qr, win[0:KV_W].astype(BF16)), NEG)
        m_w, l_w, a_w = _online_lanes(carry, s, win[KV_W:KV_ROW].astype(BF16))
        o_w = a_w / jnp.maximum(l_w, 1e-30)
        W = NSA_HEADS * NSA_HD
        gts = jax.nn.sigmoid(gr_ref[...])
        o_ref[...] = (gts[:, 0:W] * oc_ref[...] + gts[:, W:2 * W] * _heads_to_row(o_s, T)
                      + gts[:, 2 * W:3 * W] * _heads_to_row(o_w, T))
        shifted = pltpu.roll(win, wb - T, 1)
        wout_ref[0, :, 0:wb - 128] = shifted[:, 0:wb - 128]
        tail = jnp.where(jl >= 128 - T, pltpu.roll(new_t, 128 - T, 1), shifted[:, wb - 128:wb])
        wout_ref[0, :, wb - 128:wb] = tail


def _s_attn(page_table, pool_t, qx, bias, win_t, kvs_new, kvw_new, o_c, g_rep, DB, T, n_pg=32):
    n_pages = page_table.shape[1]
    past = n_pages * PAGE
    n_pg = min(n_pg, n_pages)
    nch = n_pages // n_pg
    L = NSA_HEADS * T
    W = NSA_HEADS * NSA_HD
    keys = n_pg * PAGE
    expand = jnp.where(jnp.arange(128)[:, None] == (jnp.arange(keys) // SEL_BLOCK)[None, :], 1.0, 0.0).astype(BF16)
    page_spec = lambda i: pl.BlockSpec((1, KV_ROW, PAGE), lambda b, c, pt: (pt[b, c * n_pg + i], 0, 0))
    rowb = lambda w: pl.BlockSpec((T, w), lambda b, c, pt: (b, 0))
    per_b = lambda a: pl.BlockSpec((1,) + a.shape[1:], lambda b, c, pt: (b,) + (0,) * (a.ndim - 1))
    gs = pltpu.PrefetchScalarGridSpec(
        num_scalar_prefetch=1, grid=(DB, nch),
        in_specs=[page_spec(i) for i in range(n_pg)] + [
            rowb(NSA_HEADS * KV_W), per_b(bias), pl.BlockSpec((128, keys), lambda b, c, pt: (0, 0)),
            per_b(win_t), rowb(KV_ROW), rowb(KV_ROW), rowb(W), rowb(3 * W)],
        out_specs=[rowb(W), per_b(win_t)],
        scratch_shapes=[pltpu.VMEM((L, 1), F32), pltpu.VMEM((L, 1), F32), pltpu.VMEM((L, KV_W), F32)])
    return pl.pallas_call(
        functools.partial(_s_attn_kernel, n_pg=n_pg, past=past), grid_spec=gs,
        out_shape=[jax.ShapeDtypeStruct((DB * T, W), F32), jax.ShapeDtypeStruct(win_t.shape, F32)],
        compiler_params=_cparams(2), name="sample_attn",
    )(page_table, *([pool_t] * n_pg), qx, bias, expand, win_t, kvs_new, kvw_new, o_c, g_rep)


def _prep_weights(w_in):
    o_q = 4 * HG_HEADS * HG_DK
    o_kv = o_q + NSA_HEADS * NSA_HD
    o_ng = o_kv + 6 * KV_W
    o_mg = o_ng + NSA_HEADS * 3
    w_hg = w_in[:, :o_q].astype(BF16)
    w_q = w_in[:, o_q:o_kv] * ATT_SCALE
    w_kv = w_in[:, o_kv:o_ng].astype(BF16)
    w_ng = w_in[:, o_ng:o_mg]
    w_mg = w_in[:, o_mg:].astype(BF16)
    ng4 = w_ng.reshape(D_MODEL, KVH, GROUP, 3).transpose(0, 1, 3, 2)
    ng4 = jnp.pad(ng4.reshape(D_MODEL, KVH, 3 * GROUP), ((0, 0), (0, 0), (0, 16 - 3 * GROUP)))
    vcols = jnp.concatenate([w_in[:, o_kv + (2 * i + 1) * KV_W:o_kv + (2 * i + 2) * KV_W] for i in range(3)], axis=1)
    w_t = jnp.concatenate([w_q * LOG2E, vcols], axis=1).T.astype(BF16)
    w_ngt = ng4.reshape(D_MODEL, KVH * 16).T.astype(BF16)
    wq4 = w_q.reshape(D_MODEL, KVH, GROUP, NSA_HD)
    eye = jnp.eye(KVH, dtype=w_q.dtype)
    w_qx = jnp.einsum('dgrh,gk->dgrkh', wq4, eye).reshape(D_MODEL, NSA_HEADS * KV_W).astype(BF16)
    w_gr = jnp.repeat(w_ng.reshape(D_MODEL, NSA_HEADS, 3).transpose(0, 2, 1), NSA_HD, axis=2)
    w_gr = w_gr.reshape(D_MODEL, 3 * NSA_HEADS * NSA_HD).astype(BF16)
    return w_hg, w_kv, w_mg, w_t, w_ngt, w_qx, w_gr


def kernel(x_prompt, x_sample, cache_cmp_kv, cache_sel_kv, page_table, state_win_kv, state_hgrn, state_conv, w_in, hg_lb_logits, hg_norm_g, w_out, w_up, conv_w, conv_b, w_down, g_pre_mix, g_post_mix, g_pre_ffn, g_post_ffn):
    B, T, D = x_prompt.shape
    DB, TS, _ = x_sample.shape
    depth = w_in.shape[0]
    assert depth == 1 and D == D_MODEL and T % 256 == 0 and TS == 8 and TS < CMP_BLOCK
    n_pool = cache_cmp_kv.shape[1]
    wb = state_win_kv.shape[2]
    C2 = 2 * D_FF
    l = 0

    lbs = jnp.cumsum(jax.nn.softmax(hg_lb_logits.astype(F32), axis=0), axis=0)[l]
    w_hg, w_kv, w_mg, w_t, w_ngt, w_qx, w_gr = _prep_weights(w_in[l])
    w_out_b, w_up_b, w_dn_b = w_out[l].astype(BF16), w_up[l].astype(BF16), w_down[l].astype(BF16)

    outs = {}
    for name, x, nb, tl in (("p", x_prompt, B, T), ("s", x_sample, DB, TS)):
        R = nb * tl
        x2 = x.reshape(R, D)
        h = _rmsnorm_bf16(x2, g_pre_mix[l])
        kv = [_mm(h, w_kv[:, i * KV_ROW:(i + 1) * KV_ROW], F32, name="proj_kv") for i in range(3)]
        s0 = jnp.zeros((nb, HG_HEADS, HG_DK, HG_DK), F32) if name == "p" else state_hgrn[l]
        o_hg, s_hg = _hgrn(x2, g_pre_mix[l], w_hg, lbs, hg_norm_g[l], s0, nb, tl)
        if name == "p":
            qvt = _mm_nt(w_t, h, BF16, name="proj_qv_t")
            ngt = _mm_nt(w_ngt, h, F32, tn=64, name="proj_gate_t")
            kmean, vmt = _cmp_prep(kv[0], qvt, nb, tl)
            o_nsa = _nsa_prompt(qvt, kmean, vmt, kv[1], kv[2], ngt, nb, tl)
            win_new = kv[2].reshape(nb, tl, KV_ROW)[:, tl - min(WINDOW, tl):]
            cbuf0 = jnp.zeros((nb, CONV_W - 1, C2), F32)
        else:
            qx = _mm(h, w_qx, F32, name="proj_q_pad")
            g_rep = _mm(h, w_gr, F32, name="proj_gate_rep")
            fmaj = lambda a: a.transpose(0, 2, 3, 4, 1).reshape(a.shape[0], KV_ROW, a.shape[1])
            o_c, bias = _s_cmp(page_table, fmaj(cache_cmp_kv[l]), qx, nb, tl)
            o_nsa, win_t = _s_attn(page_table, fmaj(cache_sel_kv[l]), qx, bias, fmaj(state_win_kv[l]),
                                   kv[1], kv[2], o_c, g_rep, nb, tl)
            win_new = win_t.reshape(nb, 2, KVH, NSA_HD, wb).transpose(0, 4, 1, 2, 3).reshape(nb, wb, KV_ROW)
            cbuf0 = state_conv[l]
        x1 = _merge(x2, g_pre_mix[l], w_mg, o_hg, o_nsa, w_out_b, g_post_mix[l])
        x3, cbuf = _ffn(x1, nb, g_pre_ffn[l], w_up_b, conv_w[l], conv_b[l], w_dn_b, g_post_ffn[l], cbuf0, tm=512)
        kv6 = lambda a, n: a.reshape(1, nb, n, 2, KVH, NSA_HD)
        outs[name] = (x3.reshape(nb, tl, D), kv6(kv[0], tl), kv6(kv[1], tl), kv6(win_new, win_new.shape[1]),
                      s_hg[None], cbuf[None])
    p, s = outs["p"], outs["s"]
    return (p[0], s[0], p[1], s[1], p[2], s[2], p[3], s[3], p[4], s[4], p[5], s[5])
```

```python
import functools
import math

import jax
import jax.numpy as jnp
from jax import lax
from jax.experimental import pallas as pl
from jax.experimental.pallas import tpu as pltpu

F32 = jnp.float32
BF16 = jnp.bfloat16
I32 = jnp.int32

D_MODEL = 1024
HG_HEADS = 8
HG_DK = 128
HG_CHUNK = 32
NSA_HEADS = 16
NSA_HD = 64
KVH = 4
GROUP = 4
KV_W = KVH * NSA_HD
KV_ROW = 2 * KV_W
CMP_BLOCK = 32
SEL_BLOCK = 64
SEL_TOPK = 8
WINDOW = 512
FORCE_BONUS = 1.0e4
ATT_SCALE = NSA_HD ** -0.5
LOG2E = math.log2(math.e)
D_FF = 2816
CONV_W = 3
PAGE = 128
EPS = 1e-6
NEG = -1e30
M_INIT = -1e38

VMEM_LIMIT = 56 * 1024 * 1024

NT = (((1,), (1,)), ((), ()))
TN = (((0,), (0,)), ((), ()))


def _cparams(n_axes):
    return pltpu.CompilerParams(dimension_semantics=("arbitrary",) * n_axes,
                                vmem_limit_bytes=VMEM_LIMIT)


def _dot(a, b, dims=None):
    if dims is None:
        return jnp.dot(a, b, preferred_element_type=F32)
    return lax.dot_general(a, b, dims, preferred_element_type=F32)


def _rms(x, g):
    return x * lax.rsqrt(jnp.mean(x * x, axis=-1, keepdims=True) + EPS) * g


def _rmsnorm_kernel(x_ref, g_ref, o_ref):
    o_ref[...] = _rms(x_ref[...], g_ref[...]).astype(o_ref.dtype)


def _rmsnorm_bf16(x, g, tm=512):
    R, D = x.shape
    tm = min(tm, R)
    return pl.pallas_call(
        _rmsnorm_kernel, grid=(R // tm,),
        in_specs=[pl.BlockSpec((tm, D), lambda i: (i, 0)), pl.BlockSpec((1, D), lambda i: (0, 0))],
        out_specs=pl.BlockSpec((tm, D), lambda i: (i, 0)),
        out_shape=jax.ShapeDtypeStruct((R, D), BF16),
        compiler_params=_cparams(1), name="rmsnorm",
    )(x, g.reshape(1, D))


def _mm_kernel(a_ref, w_ref, o_ref):
    o_ref[...] = _dot(a_ref[...], w_ref[...]).astype(o_ref.dtype)


def _mm(a, w, out_dtype, tm=1024, tn=512, name="proj"):
    R, K = a.shape
    N = w.shape[1]
    tm, tn = min(tm, R), min(tn, N)
    return pl.pallas_call(
        _mm_kernel, grid=(R // tm, N // tn),
        in_specs=[pl.BlockSpec((tm, K), lambda i, j: (i, 0)), pl.BlockSpec((K, tn), lambda i, j: (0, j))],
        out_specs=pl.BlockSpec((tm, tn), lambda i, j: (i, j)),
        out_shape=jax.ShapeDtypeStruct((R, N), out_dtype),
        compiler_params=_cparams(2), name=name,
    )(a, w)


def _merge_kernel(x_ref, gpre_ref, wmg_ref, ohg_ref, onsa_ref, w_ref, g_ref, o_ref):
    x = x_ref[...]
    mg = _dot(_rms(x, gpre_ref[...]).astype(BF16), wmg_ref[...])
    u = jax.nn.sigmoid(mg[:, :D_MODEL]) * ohg_ref[...] + jax.nn.sigmoid(mg[:, D_MODEL:]) * onsa_ref[...]
    y = _dot(u.astype(BF16), w_ref[...])
    o_ref[...] = x + _rms(y, g_ref[...])


def _merge(x, g_pre, w_mg, ohg, onsa, w_out, g_post, tm=512):
    R, D = x.shape
    tm = min(tm, R)
    row = pl.BlockSpec((tm, D), lambda i: (i, 0))
    const = lambda shape: pl.BlockSpec(shape, lambda i: (0, 0), pipeline_mode=pl.Buffered(1))
    return pl.pallas_call(
        _merge_kernel, grid=(R // tm,),
        in_specs=[row, const((1, D)), const((D, 2 * D)), row, row, const((D, D)), const((1, D))],
        out_specs=row,
        out_shape=jax.ShapeDtypeStruct((R, D), F32),
        compiler_params=_cparams(1), name="merge",
    )(x, g_pre.reshape(1, D), w_mg, ohg, onsa, w_out, g_post.reshape(1, D))


def _ffn_kernel(x_ref, gpre_ref, wup_ref, cw_ref, cb_ref, wdn_ref, gpost_ref, buf_ref,
                o_ref, cbuf_ref, act_scr, carry_scr, *, ns, tn):
    tm = x_ref.shape[0]
    ls = tm // ns
    x = x_ref[...]
    h = _rms(x, gpre_ref[...]).astype(BF16)
    pos = lax.broadcasted_iota(I32, (tm, 1), 0) % ls
    if ns == 1:
        @pl.when(pl.program_id(1) == 0)
        def _():
            carry_scr[...] = buf_ref[0]
    for j in range(D_FF // tn):
        halves = []
        for base in (j * tn, D_FF + j * tn):
            cols = slice(base, base + tn)
            up = _dot(h, wup_ref[:, cols])
            if ns == 1:
                b0 = carry_scr[0:1, cols]
                b1 = carry_scr[1:2, cols]
                carry_scr[:, cols] = up[tm - 2:tm, :]
                cbuf_ref[0, :, cols] = up[tm - 2:tm, :]
            else:
                bufv = buf_ref[:, :, cols]
                b0 = jnp.broadcast_to(bufv[:, 0:1, :], (ns, ls, tn)).reshape(tm, tn)
                b1 = jnp.broadcast_to(bufv[:, 1:2, :], (ns, ls, tn)).reshape(tm, tn)
                cbuf_ref[:, :, cols] = up.reshape(ns, ls, tn)[:, ls - 2:ls, :]
            def conv(x, x1, x2):
                c = cb_ref[:, cols] + x2 * cw_ref[0:1, cols]
                c = c + x1 * cw_ref[1:2, cols]
                return c + x * cw_ref[2:3, cols]

            def fix(p, x1, x2):
                return jnp.where(p == 0, b1, x1), jnp.where(p == 0, b0, jnp.where(p == 1, b1, x2))

            r1, r2 = pltpu.roll(up, 1, 0), pltpu.roll(up, 2, 0)
            if ns == 1:
                head = conv(up[0:8], *fix(pos[0:8], r1[0:8], r2[0:8]))
                c = jnp.concatenate([head, conv(up, r1, r2)[8:]], axis=0)
            else:
                c = conv(up, *fix(pos, r1, r2))
            halves.append(c)
        act_scr[:, j * tn:(j + 1) * tn] = (jax.nn.silu(halves[0]) * halves[1]).astype(BF16)
    o_ref[...] = x + _rms(_dot(act_scr[...], wdn_ref[...]), gpost_ref[...])


def _ffn(x, n_seq, g_pre, w_up, conv_w, conv_b, w_down, g_post, buf, tm):
    R, D = x.shape
    L = R // n_seq
    C2 = 2 * D_FF
    if L >= tm:
        ns, nt = 1, L // tm
        grid = (n_seq, nt)
        buf_spec = pl.BlockSpec((1, 2, C2), lambda b, t: (b, 0, 0))
    else:
        ns, nt, tm = n_seq, 1, R
        grid = (1, 1)
        buf_spec = pl.BlockSpec((ns, 2, C2), lambda b, t: (0, 0, 0))
    const = lambda shape: pl.BlockSpec(shape, lambda b, t: (0,) * len(shape), pipeline_mode=pl.Buffered(1))
    return pl.pallas_call(
        functools.partial(_ffn_kernel, ns=ns, tn=256), grid=grid,
        in_specs=[pl.BlockSpec((tm, D), lambda b, t: (b * nt + t, 0)), const((1, D)), const((D, C2)),
                  const((CONV_W, C2)), const((1, C2)), const((D_FF, D)), const((1, D)), buf_spec],
        out_specs=[pl.BlockSpec((tm, D), lambda b, t: (b * nt + t, 0)), buf_spec],
        out_shape=[jax.ShapeDtypeStruct((R, D), F32), jax.ShapeDtypeStruct((n_seq, 2, C2), F32)],
        scratch_shapes=[pltpu.VMEM((tm, D_FF), BF16), pltpu.VMEM((2, C2), F32)],
        compiler_params=_cparams(2), name="conv_ffn",
    )(x, g_pre.reshape(1, D), w_up, conv_w, conv_b.reshape(1, C2), w_down, g_post.reshape(1, D), buf)


def _pad_rows(x, n):
    if x.shape[0] >= n:
        return x
    return jnp.concatenate([x, jnp.zeros((n - x.shape[0],) + x.shape[1:], x.dtype)], axis=0)


def _hgrn_kernel(x_ref, gpre_ref, w_ref, lb_ref, ng_ref, s0_ref, o_ref, so_ref, st_scr, *, chunk):
    t = pl.program_id(1)
    tc = x_ref.shape[0]
    hp = HG_HEADS
    W = hp * HG_DK
    C = chunk
    tp = max(tc, 16)
    h = _pad_rows(_rms(x_ref[...], gpre_ref[...]), tp).astype(BF16)
    zh = _dot(h, w_ref[...])
    zq, zf, zi, zg = (zh[:, s * W:(s + 1) * W] for s in range(4))

    @pl.when(t == 0)
    def _():
        for hh in range(hp):
            st_scr[hh] = s0_ref[0, hh].T

    n_ch = tc // C
    r = lax.broadcasted_iota(I32, (tp, tp), 0)
    c = lax.broadcasted_iota(I32, (tp, tp), 1)
    causal = (c <= r) & ((r // C) == (c // C))
    lmat = jnp.where(causal, 1.0, 0.0).astype(BF16)
    in_blk = ((lax.broadcasted_iota(I32, (tp, n_ch * HG_DK), 0) // C)
              == (lax.broadcasted_iota(I32, (tp, n_ch * HG_DK), 1) // HG_DK))

    lb = lb_ref[...]
    logf = jnp.log(lb + (1.0 - lb) * jax.nn.sigmoid(zf))
    k = (1.0 - lb) * jax.nn.sigmoid(-zf)
    q = jax.nn.silu(zq)
    hi = logf.astype(BF16)
    r1 = logf - hi.astype(F32)
    mid = r1.astype(BF16)
    lo = (r1 - mid.astype(F32)).astype(BF16)
    b = _dot(lmat, hi) + _dot(lmat, mid) + _dot(lmat, lo)
    last = [b[(ci + 1) * C - 1:(ci + 1) * C, :] for ci in range(tp // C)]
    bl = jnp.concatenate([jnp.broadcast_to(x, (C, x.shape[1])) for x in last], axis=0)
    q_in = q * jnp.exp(b)
    k_out = k * jnp.exp(bl - b)
    q_rel = (q * jnp.exp(b - bl)).astype(BF16)
    decay = [jnp.exp(x) for x in last[:n_ch]]
    q_in = q_in.astype(BF16)

    for hh in range(hp):
        lanes = slice(hh * HG_DK, (hh + 1) * HG_DK)
        vb = zi[:, lanes].astype(BF16)
        ko = k_out[:, lanes]
        attn = jnp.where(causal, _dot(q_rel[:, lanes], ko.astype(BF16), NT), 0.0)
        o = _dot(attn.astype(BF16), vb)
        kx = jnp.where(in_blk, jnp.concatenate([ko] * n_ch, axis=1), 0.0).astype(BF16)
        ut_all = _dot(vb, kx, TN)
        st = st_scr[hh]
        inter = []
        for ci in range(n_ch):
            rows = slice(ci * C, min((ci + 1) * C, tp) if C >= 16 else ci * C + 16)
            inter.append(_dot(q_in[rows, lanes], st.astype(BF16), NT)[0:C])
            st = decay[ci][:, lanes] * st + ut_all[:, ci * HG_DK:(ci + 1) * HG_DK]
        o = (o[0:tc] + (inter[0] if n_ch == 1 else jnp.concatenate(inter, axis=0)))
        st_scr[hh] = st
        o_ref[:, lanes] = _rms(o, ng_ref[...]) * jax.nn.silu(zg[0:tc, lanes])

        @pl.when(t == pl.num_programs(1) - 1)
        def _():
            so_ref[0, hh] = st.T


def _hgrn(x, g_pre, w_hg, lbs, norm_g, s0, B, T):
    C = math.gcd(T, HG_CHUNK)
    tc = min(T, 256)
    nt = T // tc
    D = x.shape[1]
    W = HG_HEADS * HG_DK
    const = lambda shape: pl.BlockSpec(shape, lambda b, t: (0, 0), pipeline_mode=pl.Buffered(1))
    st_spec = pl.BlockSpec((1, HG_HEADS, HG_DK, HG_DK), lambda b, t: (b, 0, 0, 0))
    return pl.pallas_call(
        functools.partial(_hgrn_kernel, chunk=C), grid=(B, nt),
        in_specs=[pl.BlockSpec((tc, D), lambda b, t: (b * nt + t, 0)), const((1, D)), const((D, 4 * W)),
                  const((1, W)), const((1, HG_DK)), st_spec],
        out_specs=[pl.BlockSpec((tc, W), lambda b, t: (b * nt + t, 0)), st_spec],
        out_shape=[jax.ShapeDtypeStruct((B * T, W), F32),
                   jax.ShapeDtypeStruct((B, HG_HEADS, HG_DK, HG_DK), F32)],
        scratch_shapes=[pltpu.VMEM((HG_HEADS, HG_DK, HG_DK), F32)],
        compiler_params=_cparams(2), name="hgrn2",
    )(x, g_pre.reshape(1, D), w_hg, lbs.reshape(1, W), norm_g.reshape(1, HG_DK), s0)


def _topk_rows(score, k):
    n = score.shape[0]
    rid = lax.broadcasted_iota(I32, score.shape, 0)
    sel = jnp.zeros(score.shape, F32)
    s = score
    for _ in range(k):
        m = jnp.max(s, axis=0, keepdims=True)
        first = jnp.min(jnp.where(s == m, rid, n), axis=0, keepdims=True)
        hit = rid == first
        sel = jnp.where(hit, 1.0, sel)
        s = jnp.where(hit, -jnp.inf, s)
    return sel


def _softmax2_rows(s, mask):
    s = jnp.where(mask, s, NEG)
    e = jnp.where(mask, jnp.exp2(s - jnp.max(s, axis=0, keepdims=True)), 0.0)
    return e / jnp.maximum(jnp.sum(e, axis=0, keepdims=True), 1e-30)


def _online2(carry, s, v_aug):
    m, acc = carry
    m_new = jnp.maximum(m, jnp.max(s, axis=0, keepdims=True))
    p = jnp.exp2(s - m_new)
    return m_new, jnp.exp2(m - m_new) * acc + _dot(v_aug, p.astype(BF16))


def _block_means_even_odd(x):
    s = 1.0 / CMP_BLOCK
    return (x[:, 0:CMP_BLOCK, :].sum(axis=1) * s, x[:, CMP_BLOCK:SEL_BLOCK, :].sum(axis=1) * s)


def _proj_prompt_kernel(x_ref, g_ref, wk_ref, wt_ref, wg_ref, kc_ref, ksw_ref, qt_ref, kvt_ref, ngt_ref):
    nq = qt_ref.shape[0]
    h = _rms(x_ref[...], g_ref[...]).astype(BF16)
    kn = _dot(h, wk_ref[...])
    kc_ref[...] = kn[:, 0:KV_W]
    ksw_ref[...] = kn[:, KV_W:3 * KV_W].astype(BF16)
    t = _dot(wt_ref[...], h, NT)
    qt_ref[...] = t[0:nq].astype(BF16)
    for i in range(3):
        kvt_ref[i, 0] = t[nq + i * KV_ROW:nq + (i + 1) * KV_ROW]
    ngt_ref[...] = _dot(wg_ref[...], h, NT)


def _proj_prompt(x, g_pre, w_k, w_t, w_ngt, B, T, tm=512):
    R, D = x.shape
    nt = T // tm
    nq = NSA_HEADS * NSA_HD
    const = lambda a: pl.BlockSpec(a.shape, lambda b, t: (0, 0), pipeline_mode=pl.Buffered(1))
    return pl.pallas_call(
        _proj_prompt_kernel, grid=(B, nt),
        in_specs=[pl.BlockSpec((tm, D), lambda b, t: (b * nt + t, 0)), pl.BlockSpec((1, D), lambda b, t: (0, 0)),
                  const(w_k), const(w_t), const(w_ngt)],
        out_specs=[pl.BlockSpec((tm, KV_W), lambda b, t: (b * nt + t, 0)),
                   pl.BlockSpec((tm, 2 * KV_W), lambda b, t: (b * nt + t, 0)),
                   pl.BlockSpec((nq, tm), lambda b, t: (0, b * nt + t)),
                   pl.BlockSpec((3, 1, KV_ROW, tm), lambda b, t: (0, b, 0, t)),
                   pl.BlockSpec((w_ngt.shape[0], tm), lambda b, t: (0, b * nt + t))],
        out_shape=[jax.ShapeDtypeStruct((R, KV_W), F32), jax.ShapeDtypeStruct((R, 2 * KV_W), BF16),
                   jax.ShapeDtypeStruct((nq, R), BF16), jax.ShapeDtypeStruct((3, B, KV_ROW, T), F32),
                   jax.ShapeDtypeStruct((w_ngt.shape[0], R), F32)],
        compiler_params=_cparams(2), name="proj_prompt",
    )(x, g_pre.reshape(1, D), w_k, w_t, w_ngt)


def _cmp_prep_kernel(kc_ref, vct_ref, km_ref, vmt_ref):
    hb = kc_ref.shape[0]
    nb = 2 * hb
    T = hb * SEL_BLOCK
    ev, od = _block_means_even_odd(kc_ref[...])
    km_ref[0:hb, :] = ev
    km_ref[hb:nb, :] = od
    ti = lax.broadcasted_iota(I32, (T, nb), 0)
    ci = lax.broadcasted_iota(I32, (T, nb), 1)
    blk = jnp.where(ci < hb, 2 * ci, 2 * (ci - hb) + 1)
    pool = jnp.where((ti // CMP_BLOCK) == blk, 1.0 / CMP_BLOCK, 0.0).astype(BF16)
    vmt_ref[...] = _dot(vct_ref[0, 0].astype(BF16), pool)


def _cmp_prep(kc, kvt, B, T):
    nb = T // CMP_BLOCK
    hb = T // SEL_BLOCK
    return pl.pallas_call(
        _cmp_prep_kernel, grid=(B,),
        in_specs=[pl.BlockSpec((hb, SEL_BLOCK, KV_W), lambda b: (b, 0, 0)),
                  pl.BlockSpec((1, 1, KV_W, T), lambda b: (0, b, 1, 0))],
        out_specs=[pl.BlockSpec((nb, KV_W), lambda b: (b, 0)), pl.BlockSpec((KV_W, nb), lambda b: (b, 0))],
        out_shape=[jax.ShapeDtypeStruct((B * nb, KV_W), F32), jax.ShapeDtypeStruct((B * KV_W, nb), F32)],
        compiler_params=_cparams(1), name="cmp_prep",
    )(kc.reshape(B * hb, SEL_BLOCK, KV_W), kvt)


def _nsa_prompt_kernel(qt_ref, km_ref, vmt_ref, ks_ref, vst_ref, kw_ref, vwt_ref, ngt_ref, tri_ref, o_ref,
                       *, tq, tk):
    g = pl.program_id(1)
    qi = pl.program_id(2)
    par = g % 2
    L = GROUP * tq
    nblk = km_ref.shape[0]
    nsel = nblk // 2

    qt = qt_ref[...]
    qs = jnp.concatenate([qt[r * NSA_HD:(r + 1) * NSA_HD, :] for r in range(GROUP)], axis=1)
    zq = jnp.zeros_like(qs)
    qp = jnp.concatenate([jnp.where(par == 0, qs, zq), jnp.where(par == 1, qs, zq)], axis=0)

    lane = lax.broadcasted_iota(I32, (1, L), 1)
    qpos = qi * tq + (lane % tq)

    row = lax.broadcasted_iota(I32, (nblk, 1), 0)
    cblk = jnp.where(row < nsel, 2 * row, 2 * (row - nsel) + 1)
    vis = ((cblk + 1) * CMP_BLOCK - 1) <= qpos
    p_c = _softmax2_rows(_dot(km_ref[...].astype(BF16), qp), vis)
    o_c = _dot(vmt_ref[...].astype(BF16), p_c.astype(BF16))
    imp = p_c[:, 0:tq]
    for r in range(1, GROUP):
        imp = imp + p_c[:, r * tq:(r + 1) * tq]
    imp = imp[0:nsel] + imp[nsel:nblk]
    sblk = lax.broadcasted_iota(I32, (nsel, 1), 0)
    cur = qpos[:, 0:tq] // SEL_BLOCK
    forced = (sblk == 0) | (sblk == cur)
    score = jnp.where(sblk <= cur, imp + jnp.where(forced, FORCE_BONUS, 0.0), NEG)
    keep = (_topk_rows(score, min(SEL_TOPK, nsel)) > 0.5) & (score > NEG / 2)
    bias = jnp.where(keep, 0.0, NEG)
    bias = jnp.concatenate([bias] * GROUP, axis=1).astype(BF16)
    qa = jnp.concatenate([qp, bias, jnp.zeros((128 - nsel, L), BF16)], axis=0)

    NR = NSA_HD + 16
    init = (jnp.full((1, L), M_INIT, F32), jnp.zeros((NR, L), F32))
    sub = tq // tk
    spt = tk // SEL_BLOCK
    wt = WINDOW // tk
    ones = jnp.ones((16, tk), BF16)
    kblk = lax.broadcasted_iota(I32, (tk, 1), 0) // SEL_BLOCK
    lane128 = lax.broadcasted_iota(I32, (1, 128), 1)

    def tile(k_ref, vt_ref, kt, with_bias):
        start = pl.multiple_of(kt * tk, tk)
        kb = k_ref[pl.ds(start, tk), :]
        if with_bias:
            hot = jnp.where(lane128 == kt * spt + kblk, 1.0, 0.0).astype(BF16)
            s = _dot(jnp.concatenate([kb, hot], axis=1), qa)
        else:
            s = _dot(kb, qp)
        return s, jnp.concatenate([vt_ref[0, 0, :, pl.ds(start, tk)].astype(BF16), ones], axis=0)

    def finish(acc):
        return acc[0:NSA_HD] / jnp.maximum(acc[NSA_HD:NSA_HD + 1], 1e-30)

    def branch(k_ref, vt_ref, with_bias, first, n_prev, far_mask):
        carry = init
        for j in range(sub):
            s, v_a = tile(k_ref, vt_ref, qi * sub + j, with_bias)
            carry = _online2(carry, s + tri_ref[j], v_a)

        def one(kt, c):
            s, v_a = tile(k_ref, vt_ref, kt, with_bias)
            if far_mask:
                j = kt - (qi * sub - wt)
                s = s + tri_ref[jnp.where((j >= 0) & (j < sub), sub + 1 + j, sub)]
            return _online2(c, s, v_a)

        carry = lax.fori_loop(0, n_prev // 2, lambda i, c: one(first + 2 * i + 1, one(first + 2 * i, c)), carry)
        if sub % 2:
            carry = lax.cond(n_prev % 2 == 1, lambda c: one(first + n_prev - 1, c), lambda c: c, carry)
        return finish(carry[1])

    o_s = branch(ks_ref, vst_ref, True, 0, qi * sub, False)
    n_w = jnp.minimum(qi * sub, wt)
    o_w = branch(kw_ref, vwt_ref, False, qi * sub - n_w, n_w, True)

    gts = jax.nn.sigmoid(ngt_ref[...])
    grow = lambda i: jnp.concatenate([gts[i * GROUP + r:i * GROUP + r + 1, :] for r in range(GROUP)], axis=1)
    o = grow(0) * o_c + grow(1) * o_s + grow(2) * o_w
    for pr in range(GROUP // 2):
        two = jnp.concatenate([o[:, (2 * pr) * tq:(2 * pr + 1) * tq], o[:, (2 * pr + 1) * tq:(2 * pr + 2) * tq]], axis=0)
        o_ref[:, pr * 128:(pr + 1) * 128] = two.T


def _nsa_prompt(qt, kmean, vmt, ksw, kvt, ngt, B, T, tq=512, tk=256):
    tq = min(tq, T)
    nq = T // tq
    nblk = T // CMP_BLOCK
    nsel = nblk // 2
    L = GROUP * tq
    GW = GROUP * NSA_HD
    sub = tq // tk
    assert tq % tk == 0 and tk % SEL_BLOCK == 0 and WINDOW % tq == 0 and nsel <= 128
    vrow = lambda i: pl.BlockSpec((1, 1, NSA_HD, T), lambda b, g, q: (i, b, KVH + g, 0))
    kslab = lambda i: pl.BlockSpec((T, 128), lambda b, g, q: (b, (i - 1) * (KVH // 2) + g // 2))
    qc = (jnp.arange(L) % tq)[None, :]
    kr = [jnp.arange(tk)[:, None] + j * tk for j in range(sub)]
    tri = jnp.stack([jnp.where(k <= qc, 0.0, NEG) for k in kr] + [jnp.zeros((tk, L))]
                    + [jnp.where(k > qc, 0.0, NEG) for k in kr]).astype(F32)
    return pl.pallas_call(
        functools.partial(_nsa_prompt_kernel, tq=tq, tk=tk), grid=(B, KVH, nq),
        in_specs=[pl.BlockSpec((GW, tq), lambda b, g, q: (g, b * nq + q)),
                  pl.BlockSpec((nblk, 128), lambda b, g, q: (b, g // 2)),
                  pl.BlockSpec((NSA_HD, nblk), lambda b, g, q: (b * KVH + g, 0)),
                  kslab(1), vrow(1), kslab(2), vrow(2),
                  pl.BlockSpec((16, tq), lambda b, g, q: (g, b * nq + q)),
                  pl.BlockSpec((2 * sub + 1, tk, L), lambda b, g, q: (0, 0, 0), pipeline_mode=pl.Buffered(1))],
        out_specs=pl.BlockSpec((tq, GW), lambda b, g, q: (b * nq + q, g)),
        out_shape=jax.ShapeDtypeStruct((B * T, NSA_HEADS * NSA_HD), F32),
        compiler_params=_cparams(3), name="nsa_prompt",
    )(qt, kmean, vmt, ksw, kvt, ksw, kvt, ngt, tri)


def _q_rows(qx_ref):
    T = qx_ref.shape[0]
    parts = [qx_ref[:, (g * GROUP + r) * KV_W:(g * GROUP + r + 1) * KV_W] for r in range(GROUP) for g in range(KVH)]
    return jnp.concatenate(parts, axis=0).astype(BF16)


def _heads_to_row(o, T):
    parts = []
    for g in range(KVH):
        for r in range(GROUP):
            r0 = (r * KVH + g) * T
            parts.append(o[r0:r0 + T, g * NSA_HD:(g + 1) * NSA_HD])
    return jnp.concatenate(parts, axis=1)


def _softmax_lanes(s, mask):
    s = jnp.where(mask, s, NEG)
    e = jnp.where(mask, jnp.exp(s - jnp.max(s, axis=1, keepdims=True)), 0.0)
    return e / jnp.maximum(jnp.sum(e, axis=1, keepdims=True), 1e-30)


def _topk_lanes(score, jid, cand, k):
    sel = jnp.zeros(score.shape, F32)
    s = score
    for _ in range(k):
        m = jnp.max(s, axis=1, keepdims=True)
        first = jnp.min(jnp.where((s == m) & cand, jid, 3.0e38), axis=1, keepdims=True)
        hit = (jid == first) & cand
        sel = jnp.where(hit, 1.0, sel)
        s = jnp.where(hit, -jnp.inf, s)
    return sel


def _online_lanes(carry, s, v_t):
    m, l, acc = carry
    m_new = jnp.maximum(m, jnp.max(s, axis=1, keepdims=True))
    alpha = jnp.exp(m - m_new)
    p = jnp.exp(s - m_new)
    return (m_new, alpha * l + jnp.sum(p, axis=1, keepdims=True),
            alpha * acc + _dot(p.astype(BF16), v_t, NT))


def _s_cmp_kernel(pt_ref, *refs, n_pg, past):
    pages = refs[:n_pg]
    qx_ref, pool_ref, oc_ref, bias_ref, km_scr = refs[n_pg:]
    c = pl.program_id(1)
    nch = pl.num_programs(1)
    T = qx_ref.shape[0]
    per = n_pg * (PAGE // CMP_BLOCK)
    hs = per // 2
    x = jnp.concatenate([pg[0].astype(BF16) for pg in pages], axis=1)
    km_scr[c] = _dot(x, pool_ref[...])

    @pl.when(c == nch - 1)
    def _():
        n_steps = km_scr.shape[0]
        nblk = n_steps * per
        npast = nblk // 2
        gl = KVH * T
        kvm = jnp.concatenate([km_scr[i] for i in range(n_steps)], axis=1)
        qr = _q_rows(qx_ref)
        row = lax.broadcasted_iota(I32, (NSA_HEADS * T, 1), 0)
        qpos = past + (row % T)
        lane = lax.broadcasted_iota(I32, (1, nblk), 1)
        jsel = (lane // per) * hs + (lane % hs)
        odd = (lane // hs) % 2
        vis = ((2 * jsel + odd + 1) * CMP_BLOCK - 1) <= qpos
        p = _softmax_lanes(_dot(qr, kvm[0:KV_W].astype(BF16)), vis)
        o_c = _dot(p.astype(BF16), kvm[KV_W:KV_ROW].astype(BF16), NT)
        oc_ref[...] = _heads_to_row(o_c, T)
        imp = p[0:gl] + p[gl:2 * gl] + p[2 * gl:3 * gl] + p[3 * gl:4 * gl]
        imp = imp + pltpu.roll(imp, nblk - hs, 1)
        imp = jnp.concatenate([imp, jnp.zeros((gl, 128), F32)], axis=1)
        lane2 = lax.broadcasted_iota(I32, (1, nblk + 128), 1)
        jid = jnp.where(lane2 < nblk, (lane2 // per) * hs + (lane2 % hs), npast + lane2 - nblk)
        cand = ((lane2 < nblk) & (((lane2 // hs) % 2) == 0)) | (lane2 == nblk)
        cur = (past + (lax.broadcasted_iota(I32, (gl, 1), 0) % T)) // SEL_BLOCK
        forced = (jid == 0) | (jid == cur)
        score = jnp.where(cand, jnp.where(jid <= cur, imp + jnp.where(forced, FORCE_BONUS, 0.0), NEG), -jnp.inf)
        keep = (_topk_lanes(score, jid.astype(F32), cand, SEL_TOPK) > 0.5) & (score > NEG / 2)
        bias = jnp.where(keep, 0.0, NEG)
        bias = jnp.concatenate([bias] * GROUP, axis=0)
        zpad = jnp.zeros((NSA_HEADS * T, 128 - hs), F32)
        for i in range(n_steps):
            bias_ref[0, i] = jnp.concatenate([bias[:, i * per:i * per + hs], zpad], axis=1)
        bias_ref[0, n_steps] = bias[:, nblk:nblk + 128]


def _pool_matrix(n_pg):
    rows = n_pg * PAGE
    per = rows // CMP_BLOCK
    blk = jnp.arange(rows) // CMP_BLOCK
    col = (blk % 2) * (per // 2) + blk // 2
    return jnp.where(col[:, None] == jnp.arange(per)[None, :], 1.0 / CMP_BLOCK, 0.0).astype(BF16)


def _s_cmp(page_table, pool_t, qx, DB, T, n_pg=32):
    n_pages = page_table.shape[1]
    past = n_pages * PAGE
    n_pg = min(n_pg, n_pages)
    nch = n_pages // n_pg
    per = n_pg * (PAGE // CMP_BLOCK)
    L = NSA_HEADS * T
    page_spec = lambda i: pl.BlockSpec((1, KV_ROW, PAGE), lambda b, c, pt: (pt[b, c * n_pg + i], 0, 0))
    gs = pltpu.PrefetchScalarGridSpec(
        num_scalar_prefetch=1, grid=(DB, nch),
        in_specs=[page_spec(i) for i in range(n_pg)] + [
            pl.BlockSpec((T, NSA_HEADS * KV_W), lambda b, c, pt: (b, 0)),
            pl.BlockSpec((n_pg * PAGE, per), lambda b, c, pt: (0, 0))],
        out_specs=[pl.BlockSpec((T, NSA_HEADS * NSA_HD), lambda b, c, pt: (b, 0)),
                   pl.BlockSpec((1, nch + 1, L, 128), lambda b, c, pt: (b, 0, 0, 0))],
        scratch_shapes=[pltpu.VMEM((nch, KV_ROW, per), F32)])
    return pl.pallas_call(
        functools.partial(_s_cmp_kernel, n_pg=n_pg, past=past), grid_spec=gs,
        out_shape=[jax.ShapeDtypeStruct((DB * T, NSA_HEADS * NSA_HD), F32),
                   jax.ShapeDtypeStruct((DB, nch + 1, L, 128), F32)],
        compiler_params=_cparams(2), name="sample_cmp",
    )(page_table, *([pool_t] * n_pg), qx, _pool_matrix(n_pg))


def _s_attn_kernel(pt_ref, *refs, n_pg, past):
    pages = refs[:n_pg]
    (qx_ref, bias_ref, exp_ref, win_ref, ksn_ref, kwn_ref, oc_ref, gr_ref,
     o_ref, wout_ref, m_scr, l_scr, acc_scr) = refs[n_pg:]
    c = pl.program_id(1)
    n_steps = pl.num_programs(1)
    T = qx_ref.shape[0]
    L = NSA_HEADS * T
    qr = _q_rows(qx_ref)

    @pl.when(c == 0)
    def _():
        m_scr[...] = jnp.full(m_scr.shape, M_INIT, F32)
        l_scr[...] = jnp.zeros_like(l_scr)
        acc_scr[...] = jnp.zeros_like(acc_scr)

    k_t = jnp.concatenate([pg[0, 0:KV_W, :].astype(BF16) for pg in pages], axis=1)
    v_t = jnp.concatenate([pg[0, KV_W:KV_ROW, :].astype(BF16) for pg in pages], axis=1)
    s = _dot(qr, k_t) + _dot(bias_ref[0, c].astype(BF16), exp_ref[...])
    m, l, acc = _online_lanes((m_scr[...], l_scr[...], acc_scr[...]), s, v_t)
    m_scr[...] = m
    l_scr[...] = l
    acc_scr[...] = acc

    @pl.when(c == n_steps - 1)
    def _():
        row = lax.broadcasted_iota(I32, (L, 1), 0)
        tq = row % T
        pad = 128
        jl = lax.broadcasted_iota(I32, (1, pad), 1)
        new_ok = (jl < T) & (jl <= tq)
        zpad = jnp.zeros((pad - T, KV_ROW), F32)

        def new_rows(ref):
            t = jnp.concatenate([ref[...], zpad], axis=0).T
            return t[0:KV_W].astype(BF16), t[KV_W:KV_ROW].astype(BF16), t

        k_t, v_t, _ = new_rows(ksn_ref)
        s = jnp.where(new_ok, _dot(qr, k_t), NEG) + bias_ref[0, n_steps][:, 0:1]
        m_s, l_s, a_s = _online_lanes((m_scr[...], l_scr[...], acc_scr[...]), s, v_t)
        o_s = a_s / jnp.maximum(l_s, 1e-30)
        wb = win_ref.shape[2]
        win = win_ref[0]
        il = lax.broadcasted_iota(I32, (1, wb), 1)
        d = (past + tq) - (past - wb + il)
        init = (jnp.full((L, 1), M_INIT, F32), jnp.zeros((L, 1), F32), jnp.zeros((L, KV_W), F32))
        k_t, v_t, new_t = new_rows(kwn_ref)
        carry = _online_lanes(init, jnp.where(new_ok, _dot(qr, k_t), NEG), v_t)
        s = jnp.where((d >= 0) & (d < WINDOW), _dot(qr, win[0:KV_W].astype(BF16)), NEG)
        m_w, l_w, a_w = _online_lanes(carry, s, win[KV_W:KV_ROW].astype(BF16))
        o_w = a_w / jnp.maximum(l_w, 1e-30)
        W = NSA_HEADS * NSA_HD
        gts = jax.nn.sigmoid(gr_ref[...])
        o_ref[...] = (gts[:, 0:W] * oc_ref[...] + gts[:, W:2 * W] * _heads_to_row(o_s, T)
                      + gts[:, 2 * W:3 * W] * _heads_to_row(o_w, T))
        shifted = pltpu.roll(win, wb - T, 1)
        wout_ref[0, :, 0:wb - 128] = shifted[:, 0:wb - 128]
        tail = jnp.where(jl >= 128 - T, pltpu.roll(new_t, 128 - T, 1), shifted[:, wb - 128:wb])
        wout_ref[0, :, wb - 128:wb] = tail


def _s_attn(page_table, pool_t, qx, bias, win_t, kvs_new, kvw_new, o_c, g_rep, DB, T, n_pg=32):
    n_pages = page_table.shape[1]
    past = n_pages * PAGE
    n_pg = min(n_pg, n_pages)
    nch = n_pages // n_pg
    L = NSA_HEADS * T
    W = NSA_HEADS * NSA_HD
    keys = n_pg * PAGE
    expand = jnp.where(jnp.arange(128)[:, None] == (jnp.arange(keys) // SEL_BLOCK)[None, :], 1.0, 0.0).astype(BF16)
    page_spec = lambda i: pl.BlockSpec((1, KV_ROW, PAGE), lambda b, c, pt: (pt[b, c * n_pg + i], 0, 0))
    rowb = lambda w: pl.BlockSpec((T, w), lambda b, c, pt: (b, 0))
    per_b = lambda a: pl.BlockSpec((1,) + a.shape[1:], lambda b, c, pt: (b,) + (0,) * (a.ndim - 1))
    gs = pltpu.PrefetchScalarGridSpec(
        num_scalar_prefetch=1, grid=(DB, nch),
        in_specs=[page_spec(i) for i in range(n_pg)] + [
            rowb(NSA_HEADS * KV_W), per_b(bias), pl.BlockSpec((128, keys), lambda b, c, pt: (0, 0)),
            per_b(win_t), rowb(KV_ROW), rowb(KV_ROW), rowb(W), rowb(3 * W)],
        out_specs=[rowb(W), per_b(win_t)],
        scratch_shapes=[pltpu.VMEM((L, 1), F32), pltpu.VMEM((L, 1), F32), pltpu.VMEM((L, KV_W), F32)])
    return pl.pallas_call(
        functools.partial(_s_attn_kernel, n_pg=n_pg, past=past), grid_spec=gs,
        out_shape=[jax.ShapeDtypeStruct((DB * T, W), F32), jax.ShapeDtypeStruct(win_t.shape, F32)],
        compiler_params=_cparams(2), name="sample_attn",
    )(page_table, *([pool_t] * n_pg), qx, bias, expand, win_t, kvs_new, kvw_new, o_c, g_rep)


def _prep_weights(w_in):
    o_q = 4 * HG_HEADS * HG_DK
    o_kv = o_q + NSA_HEADS * NSA_HD
    o_ng = o_kv + 6 * KV_W
    o_mg = o_ng + NSA_HEADS * 3
    w_hg = w_in[:, :o_q].astype(BF16)
    w_q = w_in[:, o_q:o_kv] * ATT_SCALE
    w_kv = w_in[:, o_kv:o_ng].astype(BF16)
    w_ng = w_in[:, o_ng:o_mg]
    w_mg = w_in[:, o_mg:].astype(BF16)
    w_k = jnp.concatenate([w_in[:, o_kv + 2 * i * KV_W:o_kv + (2 * i + 1) * KV_W] for i in range(3)], axis=1)
    w_k = w_k.astype(BF16)
    w_t = jnp.concatenate([w_q * LOG2E, w_in[:, o_kv:o_ng]], axis=1).T.astype(BF16)
    ng4 = w_ng.reshape(D_MODEL, KVH, GROUP, 3).transpose(0, 1, 3, 2)
    ng4 = jnp.pad(ng4.reshape(D_MODEL, KVH, 3 * GROUP), ((0, 0), (0, 0), (0, 16 - 3 * GROUP)))
    w_ngt = ng4.reshape(D_MODEL, KVH * 16).T.astype(BF16)
    wq4 = w_q.reshape(D_MODEL, KVH, GROUP, NSA_HD)
    eye = jnp.eye(KVH, dtype=w_q.dtype)
    w_qx = jnp.einsum('dgrh,gk->dgrkh', wq4, eye).reshape(D_MODEL, NSA_HEADS * KV_W).astype(BF16)
    w_gr = jnp.repeat(w_ng.reshape(D_MODEL, NSA_HEADS, 3).transpose(0, 2, 1), NSA_HD, axis=2)
    w_gr = w_gr.reshape(D_MODEL, 3 * NSA_HEADS * NSA_HD).astype(BF16)
    return w_hg, w_kv, w_k, w_mg, w_t, w_ngt, w_qx, w_gr


def kernel(x_prompt, x_sample, cache_cmp_kv, cache_sel_kv, page_table, state_win_kv, state_hgrn, state_conv, w_in, hg_lb_logits, hg_norm_g, w_out, w_up, conv_w, conv_b, w_down, g_pre_mix, g_post_mix, g_pre_ffn, g_post_ffn):
    B, T, D = x_prompt.shape
    DB, TS, _ = x_sample.shape
    depth = w_in.shape[0]
    assert depth == 1 and D == D_MODEL and T % 256 == 0 and TS == 8 and TS < CMP_BLOCK
    C2 = 2 * D_FF
    l = 0

    lbs = jnp.cumsum(jax.nn.softmax(hg_lb_logits.astype(F32), axis=0), axis=0)[l]
    w_hg, w_kv, w_k, w_mg, w_t, w_ngt, w_qx, w_gr = _prep_weights(w_in[l])
    w_out_b, w_up_b, w_dn_b = w_out[l].astype(BF16), w_up[l].astype(BF16), w_down[l].astype(BF16)
    fmaj = lambda a: a.transpose(0, 2, 3, 4, 1).reshape(a.shape[0], KV_ROW, a.shape[1])
    rows6 = lambda a: a.reshape(a.shape[0], 2, KVH, NSA_HD, a.shape[2]).transpose(0, 4, 1, 2, 3)[None]

    outs = {}
    for name, x, nb, tl in (("p", x_prompt, B, T), ("s", x_sample, DB, TS)):
        R = nb * tl
        x2 = x.reshape(R, D)
        s0 = jnp.zeros((nb, HG_HEADS, HG_DK, HG_DK), F32) if name == "p" else state_hgrn[l]
        o_hg, s_hg = _hgrn(x2, g_pre_mix[l], w_hg, lbs, hg_norm_g[l], s0, nb, tl)
        if name == "p":
            kc, ksw, qt, kvt, ngt = _proj_prompt(x2, g_pre_mix[l], w_k, w_t, w_ngt, nb, tl)
            kmean, vmt = _cmp_prep(kc, kvt, nb, tl)
            o_nsa = _nsa_prompt(qt, kmean, vmt, ksw, kvt, ngt, nb, tl)
            kv_out = (rows6(kvt[0]), rows6(kvt[1]), rows6(kvt[2][:, :, tl - min(WINDOW, tl):]))
            cbuf0 = jnp.zeros((nb, CONV_W - 1, C2), F32)
        else:
            h = _rmsnorm_bf16(x2, g_pre_mix[l])
            kv = [_mm(h, w_kv[:, i * KV_ROW:(i + 1) * KV_ROW], F32, name="proj_kv") for i in range(3)]
            qx = _mm(h, w_qx, F32, name="proj_q_pad")
            g_rep = _mm(h, w_gr, F32, name="proj_gate_rep")
            o_c, bias = _s_cmp(page_table, fmaj(cache_cmp_kv[l]), qx, nb, tl)
            o_nsa, win_t = _s_attn(page_table, fmaj(cache_sel_kv[l]), qx, bias, fmaj(state_win_kv[l]),
                                   kv[1], kv[2], o_c, g_rep, nb, tl)
            kv6 = lambda a: a.reshape(1, nb, tl, 2, KVH, NSA_HD)
            kv_out = (kv6(kv[0]), kv6(kv[1]), rows6(win_t))
            cbuf0 = state_conv[l]
        x1 = _merge(x2, g_pre_mix[l], w_mg, o_hg, o_nsa, w_out_b, g_post_mix[l])
        x3, cbuf = _ffn(x1, nb, g_pre_ffn[l], w_up_b, conv_w[l], conv_b[l], w_dn_b, g_post_ffn[l], cbuf0, tm=512)
        outs[name] = (x3.reshape(nb, tl, D),) + kv_out + (s_hg[None], cbuf[None])
    p, s = outs["p"], outs["s"]
    return (p[0], s[0], p[1], s[1], p[2], s[2], p[3], s[3], p[4], s[4], p[5], s[5])
```

```python
import functools
import math

import jax
import jax.numpy as jnp
from jax import lax
from jax.experimental import pallas as pl
from jax.experimental.pallas import tpu as pltpu

F32 = jnp.float32
BF16 = jnp.bfloat16
I32 = jnp.int32

D_MODEL = 1024
HG_HEADS = 8
HG_DK = 128
HG_CHUNK = 32
NSA_HEADS = 16
NSA_HD = 64
KVH = 4
GROUP = 4
KV_W = KVH * NSA_HD
KV_ROW = 2 * KV_W
CMP_BLOCK = 32
SEL_BLOCK = 64
SEL_TOPK = 8
WINDOW = 512
FORCE_BONUS = 1.0e4
ATT_SCALE = NSA_HD ** -0.5
LOG2E = math.log2(math.e)
D_FF = 2816
CONV_W = 3
PAGE = 128
EPS = 1e-6
NEG = -1e30
M_INIT = -1e38

VMEM_LIMIT = 56 * 1024 * 1024

NT = (((1,), (1,)), ((), ()))
TN = (((0,), (0,)), ((), ()))


def _cparams(n_axes):
    return pltpu.CompilerParams(dimension_semantics=("arbitrary",) * n_axes,
                                vmem_limit_bytes=VMEM_LIMIT)


def _dot(a, b, dims=None):
    if dims is None:
        return jnp.dot(a, b, preferred_element_type=F32)
    return lax.dot_general(a, b, dims, preferred_element_type=F32)


def _rms(x, g):
    return x * lax.rsqrt(jnp.mean(x * x, axis=-1, keepdims=True) + EPS) * g


def _rmsnorm_kernel(x_ref, g_ref, o_ref):
    o_ref[...] = _rms(x_ref[...], g_ref[...]).astype(o_ref.dtype)


def _rmsnorm_bf16(x, g, tm=512):
    R, D = x.shape
    tm = min(tm, R)
    return pl.pallas_call(
        _rmsnorm_kernel, grid=(R // tm,),
        in_specs=[pl.BlockSpec((tm, D), lambda i: (i, 0)), pl.BlockSpec((1, D), lambda i: (0, 0))],
        out_specs=pl.BlockSpec((tm, D), lambda i: (i, 0)),
        out_shape=jax.ShapeDtypeStruct((R, D), BF16),
        compiler_params=_cparams(1), name="rmsnorm",
    )(x, g.reshape(1, D))


def _mm_kernel(a_ref, w_ref, o_ref):
    o_ref[...] = _dot(a_ref[...], w_ref[...]).astype(o_ref.dtype)


def _mm(a, w, out_dtype, tm=1024, tn=512, name="proj"):
    R, K = a.shape
    N = w.shape[1]
    tm, tn = min(tm, R), min(tn, N)
    return pl.pallas_call(
        _mm_kernel, grid=(R // tm, N // tn),
        in_specs=[pl.BlockSpec((tm, K), lambda i, j: (i, 0)), pl.BlockSpec((K, tn), lambda i, j: (0, j))],
        out_specs=pl.BlockSpec((tm, tn), lambda i, j: (i, j)),
        out_shape=jax.ShapeDtypeStruct((R, N), out_dtype),
        compiler_params=_cparams(2), name=name,
    )(a, w)


def _merge_kernel(x_ref, gpre_ref, wmg_ref, ohg_ref, onsa_ref, w_ref, g_ref, o_ref):
    x = x_ref[...]
    mg = _dot(_rms(x, gpre_ref[...]).astype(BF16), wmg_ref[...])
    u = jax.nn.sigmoid(mg[:, :D_MODEL]) * ohg_ref[...] + jax.nn.sigmoid(mg[:, D_MODEL:]) * onsa_ref[...]
    y = _dot(u.astype(BF16), w_ref[...])
    o_ref[...] = x + _rms(y, g_ref[...])


def _merge(x, g_pre, w_mg, ohg, onsa, w_out, g_post, tm=512):
    R, D = x.shape
    tm = min(tm, R)
    row = pl.BlockSpec((tm, D), lambda i: (i, 0))
    const = lambda shape: pl.BlockSpec(shape, lambda i: (0, 0), pipeline_mode=pl.Buffered(1))
    return pl.pallas_call(
        _merge_kernel, grid=(R // tm,),
        in_specs=[row, const((1, D)), const((D, 2 * D)), row, row, const((D, D)), const((1, D))],
        out_specs=row,
        out_shape=jax.ShapeDtypeStruct((R, D), F32),
        compiler_params=_cparams(1), name="merge",
    )(x, g_pre.reshape(1, D), w_mg, ohg, onsa, w_out, g_post.reshape(1, D))


def _ffn_kernel(x_ref, gpre_ref, wup_ref, cw_ref, cb_ref, wdn_ref, gpost_ref, buf_ref,
                o_ref, cbuf_ref, act_scr, carry_scr, *, ns, tn):
    tm = x_ref.shape[0]
    ls = tm // ns
    x = x_ref[...]
    h = _rms(x, gpre_ref[...]).astype(BF16)
    pos = lax.broadcasted_iota(I32, (tm, 1), 0) % ls
    if ns == 1:
        @pl.when(pl.program_id(1) == 0)
        def _():
            carry_scr[...] = buf_ref[0]
    for j in range(D_FF // tn):
        halves = []
        for base in (j * tn, D_FF + j * tn):
            cols = slice(base, base + tn)
            up = _dot(h, wup_ref[:, cols])
            if ns == 1:
                b0 = carry_scr[0:1, cols]
                b1 = carry_scr[1:2, cols]
                carry_scr[:, cols] = up[tm - 2:tm, :]
                cbuf_ref[0, :, cols] = up[tm - 2:tm, :]
            else:
                bufv = buf_ref[:, :, cols]
                b0 = jnp.broadcast_to(bufv[:, 0:1, :], (ns, ls, tn)).reshape(tm, tn)
                b1 = jnp.broadcast_to(bufv[:, 1:2, :], (ns, ls, tn)).reshape(tm, tn)
                cbuf_ref[:, :, cols] = up.reshape(ns, ls, tn)[:, ls - 2:ls, :]
            def conv(x, x1, x2):
                c = cb_ref[:, cols] + x2 * cw_ref[0:1, cols]
                c = c + x1 * cw_ref[1:2, cols]
                return c + x * cw_ref[2:3, cols]

            def fix(p, x1, x2):
                return jnp.where(p == 0, b1, x1), jnp.where(p == 0, b0, jnp.where(p == 1, b1, x2))

            r1, r2 = pltpu.roll(up, 1, 0), pltpu.roll(up, 2, 0)
            if ns == 1:
                head = conv(up[0:8], *fix(pos[0:8], r1[0:8], r2[0:8]))
                c = jnp.concatenate([head, conv(up, r1, r2)[8:]], axis=0)
            else:
                c = conv(up, *fix(pos, r1, r2))
            halves.append(c)
        act_scr[:, j * tn:(j + 1) * tn] = (jax.nn.silu(halves[0]) * halves[1]).astype(BF16)
    o_ref[...] = x + _rms(_dot(act_scr[...], wdn_ref[...]), gpost_ref[...])


def _ffn(x, n_seq, g_pre, w_up, conv_w, conv_b, w_down, g_post, buf, tm):
    R, D = x.shape
    L = R // n_seq
    C2 = 2 * D_FF
    if L >= tm:
        ns, nt = 1, L // tm
        grid = (n_seq, nt)
        buf_spec = pl.BlockSpec((1, 2, C2), lambda b, t: (b, 0, 0))
    else:
        ns, nt, tm = n_seq, 1, R
        grid = (1, 1)
        buf_spec = pl.BlockSpec((ns, 2, C2), lambda b, t: (0, 0, 0))
    const = lambda shape: pl.BlockSpec(shape, lambda b, t: (0,) * len(shape), pipeline_mode=pl.Buffered(1))
    return pl.pallas_call(
        functools.partial(_ffn_kernel, ns=ns, tn=256), grid=grid,
        in_specs=[pl.BlockSpec((tm, D), lambda b, t: (b * nt + t, 0)), const((1, D)), const((D, C2)),
                  const((CONV_W, C2)), const((1, C2)), const((D_FF, D)), const((1, D)), buf_spec],
        out_specs=[pl.BlockSpec((tm, D), lambda b, t: (b * nt + t, 0)), buf_spec],
        out_shape=[jax.ShapeDtypeStruct((R, D), F32), jax.ShapeDtypeStruct((n_seq, 2, C2), F32)],
        scratch_shapes=[pltpu.VMEM((tm, D_FF), BF16), pltpu.VMEM((2, C2), F32)],
        compiler_params=_cparams(2), name="conv_ffn",
    )(x, g_pre.reshape(1, D), w_up, conv_w, conv_b.reshape(1, C2), w_down, g_post.reshape(1, D), buf)


def _pad_rows(x, n):
    if x.shape[0] >= n:
        return x
    return jnp.concatenate([x, jnp.zeros((n - x.shape[0],) + x.shape[1:], x.dtype)], axis=0)


def _hgrn_kernel(x_ref, gpre_ref, w_ref, lb_ref, ng_ref, s0_ref, o_ref, so_ref, st_scr, *, chunk):
    t = pl.program_id(1)
    tc = x_ref.shape[0]
    hp = HG_HEADS
    W = hp * HG_DK
    C = chunk
    tp = max(tc, 16)
    h = _pad_rows(_rms(x_ref[...], gpre_ref[...]), tp).astype(BF16)
    zh = _dot(h, w_ref[...])
    zq, zf, zi, zg = (zh[:, s * W:(s + 1) * W] for s in range(4))

    @pl.when(t == 0)
    def _():
        for hh in range(hp):
            st_scr[hh] = s0_ref[0, hh].T

    n_ch = tc // C
    r = lax.broadcasted_iota(I32, (tp, tp), 0)
    c = lax.broadcasted_iota(I32, (tp, tp), 1)
    causal = (c <= r) & ((r // C) == (c // C))
    lmat = jnp.where(causal, 1.0, 0.0).astype(BF16)
    in_blk = ((lax.broadcasted_iota(I32, (tp, n_ch * HG_DK), 0) // C)
              == (lax.broadcasted_iota(I32, (tp, n_ch * HG_DK), 1) // HG_DK))

    lb = lb_ref[...]
    logf = jnp.log(lb + (1.0 - lb) * jax.nn.sigmoid(zf))
    k = (1.0 - lb) * jax.nn.sigmoid(-zf)
    q = jax.nn.silu(zq)
    hi = logf.astype(BF16)
    r1 = logf - hi.astype(F32)
    mid = r1.astype(BF16)
    lo = (r1 - mid.astype(F32)).astype(BF16)
    b = _dot(lmat, hi) + _dot(lmat, mid) + _dot(lmat, lo)
    last = [b[(ci + 1) * C - 1:(ci + 1) * C, :] for ci in range(tp // C)]
    bl = jnp.concatenate([jnp.broadcast_to(x, (C, x.shape[1])) for x in last], axis=0)
    q_in = q * jnp.exp(b)
    k_out = k * jnp.exp(bl - b)
    q_rel = (q * jnp.exp(b - bl)).astype(BF16)
    decay = [jnp.exp(x) for x in last[:n_ch]]
    q_in = q_in.astype(BF16)

    for hh in range(hp):
        lanes = slice(hh * HG_DK, (hh + 1) * HG_DK)
        vb = zi[:, lanes].astype(BF16)
        ko = k_out[:, lanes]
        attn = jnp.where(causal, _dot(q_rel[:, lanes], ko.astype(BF16), NT), 0.0)
        o = _dot(attn.astype(BF16), vb)
        kx = jnp.where(in_blk, jnp.concatenate([ko] * n_ch, axis=1), 0.0).astype(BF16)
        ut_all = _dot(vb, kx, TN)
        st = st_scr[hh]
        inter = []
        for ci in range(n_ch):
            rows = slice(ci * C, min((ci + 1) * C, tp) if C >= 16 else ci * C + 16)
            inter.append(_dot(q_in[rows, lanes], st.astype(BF16), NT)[0:C])
            st = decay[ci][:, lanes] * st + ut_all[:, ci * HG_DK:(ci + 1) * HG_DK]
        o = (o[0:tc] + (inter[0] if n_ch == 1 else jnp.concatenate(inter, axis=0)))
        st_scr[hh] = st
        o_ref[:, lanes] = _rms(o, ng_ref[...]) * jax.nn.silu(zg[0:tc, lanes])

        @pl.when(t == pl.num_programs(1) - 1)
        def _():
            so_ref[0, hh] = st.T


def _hgrn(x, g_pre, w_hg, lbs, norm_g, s0, B, T):
    C = math.gcd(T, HG_CHUNK)
    tc = min(T, 256)
    nt = T // tc
    D = x.shape[1]
    W = HG_HEADS * HG_DK
    const = lambda shape: pl.BlockSpec(shape, lambda b, t: (0, 0), pipeline_mode=pl.Buffered(1))
    st_spec = pl.BlockSpec((1, HG_HEADS, HG_DK, HG_DK), lambda b, t: (b, 0, 0, 0))
    return pl.pallas_call(
        functools.partial(_hgrn_kernel, chunk=C), grid=(B, nt),
        in_specs=[pl.BlockSpec((tc, D), lambda b, t: (b * nt + t, 0)), const((1, D)), const((D, 4 * W)),
                  const((1, W)), const((1, HG_DK)), st_spec],
        out_specs=[pl.BlockSpec((tc, W), lambda b, t: (b * nt + t, 0)), st_spec],
        out_shape=[jax.ShapeDtypeStruct((B * T, W), F32),
                   jax.ShapeDtypeStruct((B, HG_HEADS, HG_DK, HG_DK), F32)],
        scratch_shapes=[pltpu.VMEM((HG_HEADS, HG_DK, HG_DK), F32)],
        compiler_params=_cparams(2), name="hgrn2",
    )(x, g_pre.reshape(1, D), w_hg, lbs.reshape(1, W), norm_g.reshape(1, HG_DK), s0)


def _topk_rows(score, k):
    n = score.shape[0]
    rid = lax.broadcasted_iota(I32, score.shape, 0)
    sel = jnp.zeros(score.shape, F32)
    s = score
    for _ in range(k):
        m = jnp.max(s, axis=0, keepdims=True)
        first = jnp.min(jnp.where(s == m, rid, n), axis=0, keepdims=True)
        hit = rid == first
        sel = jnp.where(hit, 1.0, sel)
        s = jnp.where(hit, -jnp.inf, s)
    return sel


def _softmax2_rows(s, mask):
    s = jnp.where(mask, s, NEG)
    e = jnp.where(mask, jnp.exp2(s - jnp.max(s, axis=0, keepdims=True)), 0.0)
    return e / jnp.maximum(jnp.sum(e, axis=0, keepdims=True), 1e-30)


def _online2(carry, s, v_aug):
    m, acc = carry
    m_new = jnp.maximum(m, jnp.max(s, axis=0, keepdims=True))
    p = jnp.exp2(s - m_new)
    return m_new, jnp.exp2(m - m_new) * acc + _dot(v_aug, p.astype(BF16))


def _block_means_even_odd(x):
    s = 1.0 / CMP_BLOCK
    return (x[:, 0:CMP_BLOCK, :].sum(axis=1) * s, x[:, CMP_BLOCK:SEL_BLOCK, :].sum(axis=1) * s)


def _proj_prompt_kernel(x_ref, g_ref, wk_ref, wt_ref, wg_ref,
                        kc_ref, ksw_ref, qt_ref, kvc_ref, kvs_ref, kvw_ref, win_ref, ngt_ref):
    nq = qt_ref.shape[0]
    h = _rms(x_ref[...], g_ref[...]).astype(BF16)
    kn = _dot(h, wk_ref[...])
    kc_ref[...] = kn[:, 0:KV_W]
    ksw_ref[...] = kn[:, KV_W:3 * KV_W].astype(BF16)
    t = _dot(wt_ref[...], h, NT)
    qt_ref[...] = t[0:nq].astype(BF16)
    for i, ref in enumerate((kvc_ref, kvs_ref, kvw_ref)):
        ref[0] = t[nq + i * KV_ROW:nq + (i + 1) * KV_ROW]
    win_ref[0] = t[nq + 2 * KV_ROW:nq + 3 * KV_ROW]
    ngt_ref[...] = _dot(wg_ref[...], h, NT)


def _proj_prompt(x, g_pre, w_k, w_t, w_ngt, B, T):
    R, D = x.shape
    tm = min(WINDOW, T)
    nt = T // tm
    nq = NSA_HEADS * NSA_HD
    const = lambda a: pl.BlockSpec(a.shape, lambda b, t: (0, 0), pipeline_mode=pl.Buffered(1))
    kvt_spec = pl.BlockSpec((1, KV_ROW, tm), lambda b, t: (b, 0, t))
    kvt_shape = jax.ShapeDtypeStruct((B, KV_ROW, T), F32)
    return pl.pallas_call(
        _proj_prompt_kernel, grid=(B, nt),
        in_specs=[pl.BlockSpec((tm, D), lambda b, t: (b * nt + t, 0)), pl.BlockSpec((1, D), lambda b, t: (0, 0)),
                  const(w_k), const(w_t), const(w_ngt)],
        out_specs=[pl.BlockSpec((tm, KV_W), lambda b, t: (b * nt + t, 0)),
                   pl.BlockSpec((tm, 2 * KV_W), lambda b, t: (b * nt + t, 0)),
                   pl.BlockSpec((nq, tm), lambda b, t: (0, b * nt + t)),
                   kvt_spec, kvt_spec, kvt_spec,
                   pl.BlockSpec((1, KV_ROW, tm), lambda b, t: (b, 0, 0)),
                   pl.BlockSpec((w_ngt.shape[0], tm), lambda b, t: (0, b * nt + t))],
        out_shape=[jax.ShapeDtypeStruct((R, KV_W), F32), jax.ShapeDtypeStruct((R, 2 * KV_W), BF16),
                   jax.ShapeDtypeStruct((nq, R), BF16), kvt_shape, kvt_shape, kvt_shape,
                   jax.ShapeDtypeStruct((B, KV_ROW, tm), F32),
                   jax.ShapeDtypeStruct((w_ngt.shape[0], R), F32)],
        compiler_params=_cparams(2), name="proj_prompt",
    )(x, g_pre.reshape(1, D), w_k, w_t, w_ngt)


def _cmp_prep_kernel(kc_ref, vct_ref, km_ref, vmt_ref):
    hb = kc_ref.shape[0]
    nb = 2 * hb
    T = hb * SEL_BLOCK
    ev, od = _block_means_even_odd(kc_ref[...])
    km_ref[0:hb, :] = ev
    km_ref[hb:nb, :] = od
    ti = lax.broadcasted_iota(I32, (T, nb), 0)
    ci = lax.broadcasted_iota(I32, (T, nb), 1)
    blk = jnp.where(ci < hb, 2 * ci, 2 * (ci - hb) + 1)
    pool = jnp.where((ti // CMP_BLOCK) == blk, 1.0 / CMP_BLOCK, 0.0).astype(BF16)
    vmt_ref[...] = _dot(vct_ref[0].astype(BF16), pool)


def _cmp_prep(kc, kvt, B, T):
    nb = T // CMP_BLOCK
    hb = T // SEL_BLOCK
    return pl.pallas_call(
        _cmp_prep_kernel, grid=(B,),
        in_specs=[pl.BlockSpec((hb, SEL_BLOCK, KV_W), lambda b: (b, 0, 0)),
                  pl.BlockSpec((1, KV_W, T), lambda b: (b, 1, 0))],
        out_specs=[pl.BlockSpec((nb, KV_W), lambda b: (b, 0)), pl.BlockSpec((KV_W, nb), lambda b: (b, 0))],
        out_shape=[jax.ShapeDtypeStruct((B * nb, KV_W), F32), jax.ShapeDtypeStruct((B * KV_W, nb), F32)],
        compiler_params=_cparams(1), name="cmp_prep",
    )(kc.reshape(B * hb, SEL_BLOCK, KV_W), kvt)


def _nsa_prompt_kernel(qt_ref, km_ref, vmt_ref, ks_ref, vst_ref, kw_ref, vwt_ref, ngt_ref, tri_ref, o_ref,
                       *, tq, tk):
    g = pl.program_id(1)
    qi = pl.program_id(2)
    par = g % 2
    L = GROUP * tq
    nblk = km_ref.shape[0]
    nsel = nblk // 2

    qt = qt_ref[...]
    qs = jnp.concatenate([qt[r * NSA_HD:(r + 1) * NSA_HD, :] for r in range(GROUP)], axis=1)
    zq = jnp.zeros_like(qs)
    qp = jnp.concatenate([jnp.where(par == 0, qs, zq), jnp.where(par == 1, qs, zq)], axis=0)

    lane = lax.broadcasted_iota(I32, (1, L), 1)
    qpos = qi * tq + (lane % tq)

    row = lax.broadcasted_iota(I32, (nblk, 1), 0)
    cblk = jnp.where(row < nsel, 2 * row, 2 * (row - nsel) + 1)
    vis = ((cblk + 1) * CMP_BLOCK - 1) <= qpos
    p_c = _softmax2_rows(_dot(km_ref[...].astype(BF16), qp), vis)
    o_c = _dot(vmt_ref[...].astype(BF16), p_c.astype(BF16))
    imp = p_c[:, 0:tq]
    for r in range(1, GROUP):
        imp = imp + p_c[:, r * tq:(r + 1) * tq]
    imp = imp[0:nsel] + imp[nsel:nblk]
    sblk = lax.broadcasted_iota(I32, (nsel, 1), 0)
    cur = qpos[:, 0:tq] // SEL_BLOCK
    forced = (sblk == 0) | (sblk == cur)
    score = jnp.where(sblk <= cur, imp + jnp.where(forced, FORCE_BONUS, 0.0), NEG)
    keep = (_topk_rows(score, min(SEL_TOPK, nsel)) > 0.5) & (score > NEG / 2)
    bias = jnp.where(keep, 0.0, NEG)
    bias = jnp.concatenate([bias] * GROUP, axis=1).astype(BF16)
    qa = jnp.concatenate([qp, bias, jnp.zeros((128 - nsel, L), BF16)], axis=0)

    NR = NSA_HD + 16
    init = (jnp.full((1, L), M_INIT, F32), jnp.zeros((NR, L), F32))
    sub = tq // tk
    spt = tk // SEL_BLOCK
    wt = WINDOW // tk
    ones = jnp.ones((16, tk), BF16)
    kblk = lax.broadcasted_iota(I32, (tk, 1), 0) // SEL_BLOCK
    lane128 = lax.broadcasted_iota(I32, (1, 128), 1)

    def tile(k_ref, vt_ref, kt, with_bias):
        start = pl.multiple_of(kt * tk, tk)
        kb = k_ref[pl.ds(start, tk), :]
        if with_bias:
            hot = jnp.where(lane128 == kt * spt + kblk, 1.0, 0.0).astype(BF16)
            s = _dot(jnp.concatenate([kb, hot], axis=1), qa)
        else:
            s = _dot(kb, qp)
        return s, jnp.concatenate([vt_ref[0, :, pl.ds(start, tk)].astype(BF16), ones], axis=0)

    def finish(acc):
        return acc[0:NSA_HD] / jnp.maximum(acc[NSA_HD:NSA_HD + 1], 1e-30)

    def branch(k_ref, vt_ref, with_bias, first, n_prev, far_mask):
        carry = init
        for j in range(sub):
            s, v_a = tile(k_ref, vt_ref, qi * sub + j, with_bias)
            carry = _online2(carry, s + tri_ref[j], v_a)

        def one(kt, c):
            s, v_a = tile(k_ref, vt_ref, kt, with_bias)
            if far_mask:
                j = kt - (qi * sub - wt)
                s = s + tri_ref[jnp.where((j >= 0) & (j < sub), sub + 1 + j, sub)]
            return _online2(c, s, v_a)

        carry = lax.fori_loop(0, n_prev // 2, lambda i, c: one(first + 2 * i + 1, one(first + 2 * i, c)), carry)
        if sub % 2:
            carry = lax.cond(n_prev % 2 == 1, lambda c: one(first + n_prev - 1, c), lambda c: c, carry)
        return finish(carry[1])

    o_s = branch(ks_ref, vst_ref, True, 0, qi * sub, False)
    n_w = jnp.minimum(qi * sub, wt)
    o_w = branch(kw_ref, vwt_ref, False, qi * sub - n_w, n_w, True)

    gts = jax.nn.sigmoid(ngt_ref[...])
    grow = lambda i: jnp.concatenate([gts[i * GROUP + r:i * GROUP + r + 1, :] for r in range(GROUP)], axis=1)
    o = grow(0) * o_c + grow(1) * o_s + grow(2) * o_w
    for pr in range(GROUP // 2):
        two = jnp.concatenate([o[:, (2 * pr) * tq:(2 * pr + 1) * tq], o[:, (2 * pr + 1) * tq:(2 * pr + 2) * tq]], axis=0)
        o_ref[:, pr * 128:(pr + 1) * 128] = two.T


def _nsa_prompt(qt, kmean, vmt, ksw, kvt_s, kvt_w, ngt, B, T, tq=512, tk=256):
    tq = min(tq, T)
    nq = T // tq
    nblk = T // CMP_BLOCK
    nsel = nblk // 2
    L = GROUP * tq
    GW = GROUP * NSA_HD
    sub = tq // tk
    assert tq % tk == 0 and tk % SEL_BLOCK == 0 and WINDOW % tq == 0 and nsel <= 128
    vrow = pl.BlockSpec((1, NSA_HD, T), lambda b, g, q: (b, KVH + g, 0))
    kslab = lambda i: pl.BlockSpec((T, 128), lambda b, g, q: (b, (i - 1) * (KVH // 2) + g // 2))
    qc = (jnp.arange(L) % tq)[None, :]
    kr = [jnp.arange(tk)[:, None] + j * tk for j in range(sub)]
    tri = jnp.stack([jnp.where(k <= qc, 0.0, NEG) for k in kr] + [jnp.zeros((tk, L))]
                    + [jnp.where(k > qc, 0.0, NEG) for k in kr]).astype(F32)
    return pl.pallas_call(
        functools.partial(_nsa_prompt_kernel, tq=tq, tk=tk), grid=(B, KVH, nq),
        in_specs=[pl.BlockSpec((GW, tq), lambda b, g, q: (g, b * nq + q)),
                  pl.BlockSpec((nblk, 128), lambda b, g, q: (b, g // 2)),
                  pl.BlockSpec((NSA_HD, nblk), lambda b, g, q: (b * KVH + g, 0)),
                  kslab(1), vrow, kslab(2), vrow,
                  pl.BlockSpec((16, tq), lambda b, g, q: (g, b * nq + q)),
                  pl.BlockSpec((2 * sub + 1, tk, L), lambda b, g, q: (0, 0, 0), pipeline_mode=pl.Buffered(1))],
        out_specs=pl.BlockSpec((tq, GW), lambda b, g, q: (b * nq + q, g)),
        out_shape=jax.ShapeDtypeStruct((B * T, NSA_HEADS * NSA_HD), F32),
        compiler_params=_cparams(3), name="nsa_prompt",
    )(qt, kmean, vmt, ksw, kvt_s, ksw, kvt_w, ngt, tri)


def _q_rows(qx_ref):
    T = qx_ref.shape[0]
    parts = [qx_ref[:, (g * GROUP + r) * KV_W:(g * GROUP + r + 1) * KV_W] for r in range(GROUP) for g in range(KVH)]
    return jnp.concatenate(parts, axis=0).astype(BF16)


def _heads_to_row(o, T):
    parts = []
    for g in range(KVH):
        for r in range(GROUP):
            r0 = (r * KVH + g) * T
            parts.append(o[r0:r0 + T, g * NSA_HD:(g + 1) * NSA_HD])
    return jnp.concatenate(parts, axis=1)


def _softmax_lanes(s, mask):
    s = jnp.where(mask, s, NEG)
    e = jnp.where(mask, jnp.exp(s - jnp.max(s, axis=1, keepdims=True)), 0.0)
    return e / jnp.maximum(jnp.sum(e, axis=1, keepdims=True), 1e-30)


def _topk_lanes(score, jid, cand, k):
    sel = jnp.zeros(score.shape, F32)
    s = score
    for _ in range(k):
        m = jnp.max(s, axis=1, keepdims=True)
        first = jnp.min(jnp.where((s == m) & cand, jid, 3.0e38), axis=1, keepdims=True)
        hit = (jid == first) & cand
        sel = jnp.where(hit, 1.0, sel)
        s = jnp.where(hit, -jnp.inf, s)
    return sel


def _online_lanes(carry, s, v_t):
    m, l, acc = carry
    m_new = jnp.maximum(m, jnp.max(s, axis=1, keepdims=True))
    alpha = jnp.exp(m - m_new)
    p = jnp.exp(s - m_new)
    return (m_new, alpha * l + jnp.sum(p, axis=1, keepdims=True),
            alpha * acc + _dot(p.astype(BF16), v_t, NT))


def _s_cmp_kernel(pt_ref, *refs, n_pg, past):
    pages = refs[:n_pg]
    qx_ref, pool_ref, oc_ref, bias_ref, km_scr = refs[n_pg:]
    c = pl.program_id(1)
    nch = pl.num_programs(1)
    T = qx_ref.shape[0]
    per = n_pg * (PAGE // CMP_BLOCK)
    hs = per // 2
    x = jnp.concatenate([pg[0].astype(BF16) for pg in pages], axis=1)
    km_scr[c] = _dot(pool_ref[...], x, NT)

    @pl.when(c == nch - 1)
    def _():
        n_steps = km_scr.shape[0]
        nblk = n_steps * per
        npast = nblk // 2
        gl = KVH * T
        kvm = jnp.concatenate([km_scr[i] for i in range(n_steps)], axis=0)
        qr = _q_rows(qx_ref)
        row = lax.broadcasted_iota(I32, (NSA_HEADS * T, 1), 0)
        qpos = past + (row % T)
        lane = lax.broadcasted_iota(I32, (1, nblk), 1)
        jsel = (lane // per) * hs + (lane % hs)
        odd = (lane // hs) % 2
        vis = ((2 * jsel + odd + 1) * CMP_BLOCK - 1) <= qpos
        p = _softmax_lanes(_dot(qr, kvm[:, 0:KV_W].astype(BF16), NT), vis)
        o_c = _dot(p.astype(BF16), kvm[:, KV_W:KV_ROW].astype(BF16))
        oc_ref[...] = _heads_to_row(o_c, T)
        imp = p[0:gl] + p[gl:2 * gl] + p[2 * gl:3 * gl] + p[3 * gl:4 * gl]
        imp = imp + pltpu.roll(imp, nblk - hs, 1)
        imp = jnp.concatenate([imp, jnp.zeros((gl, 128), F32)], axis=1)
        lane2 = lax.broadcasted_iota(I32, (1, nblk + 128), 1)
        jid = jnp.where(lane2 < nblk, (lane2 // per) * hs + (lane2 % hs), npast + lane2 - nblk)
        cand = ((lane2 < nblk) & (((lane2 // hs) % 2) == 0)) | (lane2 == nblk)
        cur = (past + (lax.broadcasted_iota(I32, (gl, 1), 0) % T)) // SEL_BLOCK
        forced = (jid == 0) | (jid == cur)
        score = jnp.where(cand, jnp.where(jid <= cur, imp + jnp.where(forced, FORCE_BONUS, 0.0), NEG), -jnp.inf)
        keep = (_topk_lanes(score, jid.astype(F32), cand, SEL_TOPK) > 0.5) & (score > NEG / 2)
        bias = jnp.where(keep, 0.0, NEG)
        bias = jnp.concatenate([bias] * GROUP, axis=0)
        zpad = jnp.zeros((NSA_HEADS * T, 128 - hs), F32)
        for i in range(n_steps):
            bias_ref[0, i] = jnp.concatenate([bias[:, i * per:i * per + hs], zpad], axis=1)
        bias_ref[0, n_steps] = bias[:, nblk:nblk + 128]


def _pool_matrix(n_pg):
    rows = n_pg * PAGE
    per = rows // CMP_BLOCK
    blk = jnp.arange(rows) // CMP_BLOCK
    col = (blk % 2) * (per // 2) + blk // 2
    return jnp.where(jnp.arange(per)[:, None] == col[None, :], 1.0 / CMP_BLOCK, 0.0).astype(BF16)


def _s_cmp(page_table, pool_t, qx, DB, T, n_pg=32):
    n_pages = page_table.shape[1]
    past = n_pages * PAGE
    n_pg = min(n_pg, n_pages)
    nch = n_pages // n_pg
    per = n_pg * (PAGE // CMP_BLOCK)
    L = NSA_HEADS * T
    page_spec = lambda i: pl.BlockSpec((1, KV_ROW, PAGE), lambda b, c, pt: (pt[b, c * n_pg + i], 0, 0))
    gs = pltpu.PrefetchScalarGridSpec(
        num_scalar_prefetch=1, grid=(DB, nch),
        in_specs=[page_spec(i) for i in range(n_pg)] + [
            pl.BlockSpec((T, NSA_HEADS * KV_W), lambda b, c, pt: (b, 0)),
            pl.BlockSpec((per, n_pg * PAGE), lambda b, c, pt: (0, 0))],
        out_specs=[pl.BlockSpec((T, NSA_HEADS * NSA_HD), lambda b, c, pt: (b, 0)),
                   pl.BlockSpec((1, nch + 1, L, 128), lambda b, c, pt: (b, 0, 0, 0))],
        scratch_shapes=[pltpu.VMEM((nch, per, KV_ROW), F32)])
    return pl.pallas_call(
        functools.partial(_s_cmp_kernel, n_pg=n_pg, past=past), grid_spec=gs,
        out_shape=[jax.ShapeDtypeStruct((DB * T, NSA_HEADS * NSA_HD), F32),
                   jax.ShapeDtypeStruct((DB, nch + 1, L, 128), F32)],
        compiler_params=_cparams(2), name="sample_cmp",
    )(page_table, *([pool_t] * n_pg), qx, _pool_matrix(n_pg))


def _s_attn_kernel(pt_ref, *refs, n_pg, past):
    pages = refs[:n_pg]
    (qx_ref, bias_ref, exp_ref, win_ref, ksn_ref, kwn_ref, oc_ref, gr_ref,
     o_ref, wout_ref, m_scr, l_scr, acc_scr) = refs[n_pg:]
    c = pl.program_id(1)
    n_steps = pl.num_programs(1)
    T = qx_ref.shape[0]
    L = NSA_HEADS * T
    qr = _q_rows(qx_ref)

    @pl.when(c == 0)
    def _():
        m_scr[...] = jnp.full(m_scr.shape, M_INIT, F32)
        l_scr[...] = jnp.zeros_like(l_scr)
        acc_scr[...] = jnp.zeros_like(acc_scr)

    k_t = jnp.concatenate([pg[0, 0:KV_W, :].astype(BF16) for pg in pages], axis=1)
    v_t = jnp.concatenate([pg[0, KV_W:KV_ROW, :].astype(BF16) for pg in pages], axis=1)
    s = _dot(qr, k_t) + _dot(bias_ref[0, c].astype(BF16), exp_ref[...])
    m, l, acc = _online_lanes((m_scr[...], l_scr[...], acc_scr[...]), s, v_t)
    m_scr[...] = m
    l_scr[...] = l
    acc_scr[...] = acc

    @pl.when(c == n_steps - 1)
    def _():
        row = lax.broadcasted_iota(I32, (L, 1), 0)
        tq = row % T
        pad = 128
        jl = lax.broadcasted_iota(I32, (1, pad), 1)
        new_ok = (jl < T) & (jl <= tq)
        zpad = jnp.zeros((pad - T, KV_ROW), F32)

        def new_rows(ref):
            t = jnp.concatenate([ref[...], zpad], axis=0).T
            return t[0:KV_W].astype(BF16), t[KV_W:KV_ROW].astype(BF16), t

        k_t, v_t, _ = new_rows(ksn_ref)
        s = jnp.where(new_ok, _dot(qr, k_t), NEG) + bias_ref[0, n_steps][:, 0:1]
        m_s, l_s, a_s = _online_lanes((m_scr[...], l_scr[...], acc_scr[...]), s, v_t)
        o_s = a_s / jnp.maximum(l_s, 1e-30)
        wb = win_ref.shape[2]
        win = win_ref[0]
        il = lax.broadcasted_iota(I32, (1, wb), 1)
        d = (past + tq) - (past - wb + il)
        init = (jnp.full((L, 1), M_INIT, F32), jnp.zeros((L, 1), F32), jnp.zeros((L, KV_W), F32))
        k_t, v_t, new_t = new_rows(kwn_ref)
        carry = _online_lanes(init, jnp.where(new_ok, _dot(qr, k_t), NEG), v_t)
        s = jnp.where((d >= 0) & (d < WINDOW), _dot(qr, win[0:KV_W].astype(BF16)), NEG)
        m_w, l_w, a_w = _online_lanes(carry, s, win[KV_W:KV_ROW].astype(BF16))
        o_w = a_w / jnp.maximum(l_w, 1e-30)
        W = NSA_HEADS * NSA_HD
        gts = jax.nn.sigmoid(gr_ref[...])
        o_ref[...] = (gts[:, 0:W] * oc_ref[...] + gts[:, W:2 * W] * _heads_to_row(o_s, T)
                      + gts[:, 2 * W:3 * W] * _heads_to_row(o_w, T))
        shifted = pltpu.roll(win, wb - T, 1)
        wout_ref[0, :, 0:wb - 128] = shifted[:, 0:wb - 128]
        tail = jnp.where(jl >= 128 - T, pltpu.roll(new_t, 128 - T, 1), shifted[:, wb - 128:wb])
        wout_ref[0, :, wb - 128:wb] = tail


def _s_attn(page_table, pool_t, qx, bias, win_t, kvs_new, kvw_new, o_c, g_rep, DB, T, n_pg=32):
    n_pages = page_table.shape[1]
    past = n_pages * PAGE
    n_pg = min(n_pg, n_pages)
    nch = n_pages // n_pg
    L = NSA_HEADS * T
    W = NSA_HEADS * NSA_HD
    keys = n_pg * PAGE
    expand = jnp.where(jnp.arange(128)[:, None] == (jnp.arange(keys) // SEL_BLOCK)[None, :], 1.0, 0.0).astype(BF16)
    page_spec = lambda i: pl.BlockSpec((1, KV_ROW, PAGE), lambda b, c, pt: (pt[b, c * n_pg + i], 0, 0))
    rowb = lambda w: pl.BlockSpec((T, w), lambda b, c, pt: (b, 0))
    per_b = lambda a: pl.BlockSpec((1,) + a.shape[1:], lambda b, c, pt: (b,) + (0,) * (a.ndim - 1))
    gs = pltpu.PrefetchScalarGridSpec(
        num_scalar_prefetch=1, grid=(DB, nch),
        in_specs=[page_spec(i) for i in range(n_pg)] + [
            rowb(NSA_HEADS * KV_W), per_b(bias), pl.BlockSpec((128, keys), lambda b, c, pt: (0, 0)),
            per_b(win_t), rowb(KV_ROW), rowb(KV_ROW), rowb(W), rowb(3 * W)],
        out_specs=[rowb(W), per_b(win_t)],
        scratch_shapes=[pltpu.VMEM((L, 1), F32), pltpu.VMEM((L, 1), F32), pltpu.VMEM((L, KV_W), F32)])
    return pl.pallas_call(
        functools.partial(_s_attn_kernel, n_pg=n_pg, past=past), grid_spec=gs,
        out_shape=[jax.ShapeDtypeStruct((DB * T, W), F32), jax.ShapeDtypeStruct(win_t.shape, F32)],
        compiler_params=_cparams(2), name="sample_attn",
    )(page_table, *([pool_t] * n_pg), qx, bias, expand, win_t, kvs_new, kvw_new, o_c, g_rep)


def _prep_weights(w_in):
    o_q = 4 * HG_HEADS * HG_DK
    o_kv = o_q + NSA_HEADS * NSA_HD
    o_ng = o_kv + 6 * KV_W
    o_mg = o_ng + NSA_HEADS * 3
    w_hg = w_in[:, :o_q].astype(BF16)
    w_q = w_in[:, o_q:o_kv] * ATT_SCALE
    w_kv = w_in[:, o_kv:o_ng].astype(BF16)
    w_ng = w_in[:, o_ng:o_mg]
    w_mg = w_in[:, o_mg:].astype(BF16)
    w_k = jnp.concatenate([w_in[:, o_kv + 2 * i * KV_W:o_kv + (2 * i + 1) * KV_W] for i in range(3)], axis=1)
    w_k = w_k.astype(BF16)
    w_t = jnp.concatenate([w_q * LOG2E, w_in[:, o_kv:o_ng]], axis=1).T.astype(BF16)
    ng4 = w_ng.reshape(D_MODEL, KVH, GROUP, 3).transpose(0, 1, 3, 2)
    ng4 = jnp.pad(ng4.reshape(D_MODEL, KVH, 3 * GROUP), ((0, 0), (0, 0), (0, 16 - 3 * GROUP)))
    w_ngt = ng4.reshape(D_MODEL, KVH * 16).T.astype(BF16)
    wq4 = w_q.reshape(D_MODEL, KVH, GROUP, NSA_HD)
    eye = jnp.eye(KVH, dtype=w_q.dtype)
    w_qx = jnp.einsum('dgrh,gk->dgrkh', wq4, eye).reshape(D_MODEL, NSA_HEADS * KV_W).astype(BF16)
    w_gr = jnp.repeat(w_ng.reshape(D_MODEL, NSA_HEADS, 3).transpose(0, 2, 1), NSA_HD, axis=2)
    w_gr = w_gr.reshape(D_MODEL, 3 * NSA_HEADS * NSA_HD).astype(BF16)
    return w_hg, w_kv, w_k, w_mg, w_t, w_ngt, w_qx, w_gr


def kernel(x_prompt, x_sample, cache_cmp_kv, cache_sel_kv, page_table, state_win_kv, state_hgrn, state_conv, w_in, hg_lb_logits, hg_norm_g, w_out, w_up, conv_w, conv_b, w_down, g_pre_mix, g_post_mix, g_pre_ffn, g_post_ffn):
    B, T, D = x_prompt.shape
    DB, TS, _ = x_sample.shape
    depth = w_in.shape[0]
    assert depth == 1 and D == D_MODEL and T % 256 == 0 and TS == 8 and TS < CMP_BLOCK
    C2 = 2 * D_FF
    l = 0

    lbs = jnp.cumsum(jax.nn.softmax(hg_lb_logits.astype(F32), axis=0), axis=0)[l]
    w_hg, w_kv, w_k, w_mg, w_t, w_ngt, w_qx, w_gr = _prep_weights(w_in[l])
    w_out_b, w_up_b, w_dn_b = w_out[l].astype(BF16), w_up[l].astype(BF16), w_down[l].astype(BF16)
    fmaj = lambda a: a.transpose(0, 2, 3, 4, 1).reshape(a.shape[0], KV_ROW, a.shape[1])
    rows6 = lambda a: a.reshape(a.shape[0], 2, KVH, NSA_HD, a.shape[2]).transpose(0, 4, 1, 2, 3)[None]

    outs = {}
    for name, x, nb, tl in (("p", x_prompt, B, T), ("s", x_sample, DB, TS)):
        R = nb * tl
        x2 = x.reshape(R, D)
        s0 = jnp.zeros((nb, HG_HEADS, HG_DK, HG_DK), F32) if name == "p" else state_hgrn[l]
        o_hg, s_hg = _hgrn(x2, g_pre_mix[l], w_hg, lbs, hg_norm_g[l], s0, nb, tl)
        if name == "p":
            kc, ksw, qt, kvt_c, kvt_s, kvt_w, win_t, ngt = _proj_prompt(x2, g_pre_mix[l], w_k, w_t, w_ngt, nb, tl)
            kmean, vmt = _cmp_prep(kc, kvt_c, nb, tl)
            o_nsa = _nsa_prompt(qt, kmean, vmt, ksw, kvt_s, kvt_w, ngt, nb, tl)
            kv_out = (rows6(kvt_c), rows6(kvt_s), rows6(win_t))
            cbuf0 = jnp.zeros((nb, CONV_W - 1, C2), F32)
        else:
            h = _rmsnorm_bf16(x2, g_pre_mix[l])
            kv = [_mm(h, w_kv[:, i * KV_ROW:(i + 1) * KV_ROW], F32, name="proj_kv") for i in range(3)]
            qx = _mm(h, w_qx, F32, name="proj_q_pad")
            g_rep = _mm(h, w_gr, F32, name="proj_gate_rep")
            o_c, bias = _s_cmp(page_table, fmaj(cache_cmp_kv[l]), qx, nb, tl)
            o_nsa, win_t = _s_attn(page_table, fmaj(cache_sel_kv[l]), qx, bias, fmaj(state_win_kv[l]),
                                   kv[1], kv[2], o_c, g_rep, nb, tl)
            kv6 = lambda a: a.reshape(1, nb, tl, 2, KVH, NSA_HD)
            kv_out = (kv6(kv[0]), kv6(kv[1]), rows6(win_t))
            cbuf0 = state_conv[l]
        x1 = _merge(x2, g_pre_mix[l], w_mg, o_hg, o_nsa, w_out_b, g_post_mix[l])
        x3, cbuf = _ffn(x1, nb, g_pre_ffn[l], w_up_b, conv_w[l], conv_b[l], w_dn_b, g_post_ffn[l], cbuf0, tm=1024)
        outs[name] = (x3.reshape(nb, tl, D),) + kv_out + (s_hg[None], cbuf[None])
    p, s = outs["p"], outs["s"]
    return (p[0], s[0], p[1], s[1], p[2], s[2], p[3], s[3], p[4], s[4], p[5], s[5])
```

```python
import functools
import math

import jax
import jax.numpy as jnp
from jax import lax
from jax.experimental import pallas as pl
from jax.experimental.pallas import tpu as pltpu

F32 = jnp.float32
BF16 = jnp.bfloat16
I32 = jnp.int32

D_MODEL = 1024
HG_HEADS = 8
HG_DK = 128
HG_CHUNK = 32
NSA_HEADS = 16
NSA_HD = 64
KVH = 4
GROUP = 4
KV_W = KVH * NSA_HD
KV_ROW = 2 * KV_W
CMP_BLOCK = 32
SEL_BLOCK = 64
SEL_TOPK = 8
WINDOW = 512
FORCE_BONUS = 1.0e4
ATT_SCALE = NSA_HD ** -0.5
LOG2E = math.log2(math.e)
D_FF = 2816
CONV_W = 3
PAGE = 128
EPS = 1e-6
NEG = -1e30
M_INIT = -1e38

VMEM_LIMIT = 56 * 1024 * 1024

NT = (((1,), (1,)), ((), ()))
TN = (((0,), (0,)), ((), ()))


def _cparams(n_axes):
    return pltpu.CompilerParams(dimension_semantics=("arbitrary",) * n_axes,
                                vmem_limit_bytes=VMEM_LIMIT)


def _dot(a, b, dims=None):
    if dims is None:
        return jnp.dot(a, b, preferred_element_type=F32)
    return lax.dot_general(a, b, dims, preferred_element_type=F32)


def _rms(x, g):
    return x * lax.rsqrt(jnp.mean(x * x, axis=-1, keepdims=True) + EPS) * g


def _rmsnorm_kernel(x_ref, g_ref, o_ref):
    o_ref[...] = _rms(x_ref[...], g_ref[...]).astype(o_ref.dtype)


def _rmsnorm_bf16(x, g, tm=512):
    R, D = x.shape
    tm = min(tm, R)
    return pl.pallas_call(
        _rmsnorm_kernel, grid=(R // tm,),
        in_specs=[pl.BlockSpec((tm, D), lambda i: (i, 0)), pl.BlockSpec((1, D), lambda i: (0, 0))],
        out_specs=pl.BlockSpec((tm, D), lambda i: (i, 0)),
        out_shape=jax.ShapeDtypeStruct((R, D), BF16),
        compiler_params=_cparams(1), name="rmsnorm",
    )(x, g.reshape(1, D))


def _mm_kernel(a_ref, w_ref, o_ref):
    o_ref[...] = _dot(a_ref[...], w_ref[...]).astype(o_ref.dtype)


def _mm(a, w, out_dtype, tm=1024, tn=512, name="proj"):
    R, K = a.shape
    N = w.shape[1]
    tm, tn = min(tm, R), min(tn, N)
    return pl.pallas_call(
        _mm_kernel, grid=(R // tm, N // tn),
        in_specs=[pl.BlockSpec((tm, K), lambda i, j: (i, 0)), pl.BlockSpec((K, tn), lambda i, j: (0, j))],
        out_specs=pl.BlockSpec((tm, tn), lambda i, j: (i, j)),
        out_shape=jax.ShapeDtypeStruct((R, N), out_dtype),
        compiler_params=_cparams(2), name=name,
    )(a, w)


def _merge_kernel(x_ref, gpre_ref, wmg_ref, ohg_ref, onsa_ref, w_ref, g_ref, o_ref):
    x = x_ref[...]
    mg = _dot(_rms(x, gpre_ref[...]).astype(BF16), wmg_ref[...])
    u = jax.nn.sigmoid(mg[:, :D_MODEL]) * ohg_ref[...] + jax.nn.sigmoid(mg[:, D_MODEL:]) * onsa_ref[...]
    y = _dot(u.astype(BF16), w_ref[...])
    o_ref[...] = x + _rms(y, g_ref[...])


def _merge(x, g_pre, w_mg, ohg, onsa, w_out, g_post, tm=512):
    R, D = x.shape
    tm = min(tm, R)
    row = pl.BlockSpec((tm, D), lambda i: (i, 0))
    const = lambda shape: pl.BlockSpec(shape, lambda i: (0, 0), pipeline_mode=pl.Buffered(1))
    return pl.pallas_call(
        _merge_kernel, grid=(R // tm,),
        in_specs=[row, const((1, D)), const((D, 2 * D)), row, row, const((D, D)), const((1, D))],
        out_specs=row,
        out_shape=jax.ShapeDtypeStruct((R, D), F32),
        compiler_params=_cparams(1), name="merge",
    )(x, g_pre.reshape(1, D), w_mg, ohg, onsa, w_out, g_post.reshape(1, D))


def _ffn_kernel(x_ref, gpre_ref, wup_ref, cw_ref, cb_ref, wdn_ref, gpost_ref, buf_ref,
                o_ref, cbuf_ref, act_scr, carry_scr, *, ns, tn):
    tm = x_ref.shape[0]
    ls = tm // ns
    x = x_ref[...]
    h = _rms(x, gpre_ref[...]).astype(BF16)
    pos = lax.broadcasted_iota(I32, (tm, 1), 0) % ls
    if ns == 1:
        @pl.when(pl.program_id(1) == 0)
        def _():
            carry_scr[...] = buf_ref[0]
    for j in range(D_FF // tn):
        halves = []
        for base in (j * tn, D_FF + j * tn):
            cols = slice(base, base + tn)
            up = _dot(h, wup_ref[:, cols])
            if ns == 1:
                b0 = carry_scr[0:1, cols]
                b1 = carry_scr[1:2, cols]
                carry_scr[:, cols] = up[tm - 2:tm, :]
                cbuf_ref[0, :, cols] = up[tm - 2:tm, :]
            else:
                bufv = buf_ref[:, :, cols]
                b0 = jnp.broadcast_to(bufv[:, 0:1, :], (ns, ls, tn)).reshape(tm, tn)
                b1 = jnp.broadcast_to(bufv[:, 1:2, :], (ns, ls, tn)).reshape(tm, tn)
                cbuf_ref[:, :, cols] = up.reshape(ns, ls, tn)[:, ls - 2:ls, :]
            def conv(x, x1, x2):
                c = cb_ref[:, cols] + x2 * cw_ref[0:1, cols]
                c = c + x1 * cw_ref[1:2, cols]
                return c + x * cw_ref[2:3, cols]

            def fix(p, x1, x2):
                return jnp.where(p == 0, b1, x1), jnp.where(p == 0, b0, jnp.where(p == 1, b1, x2))

            r1, r2 = pltpu.roll(up, 1, 0), pltpu.roll(up, 2, 0)
            if ns == 1:
                head = conv(up[0:8], *fix(pos[0:8], r1[0:8], r2[0:8]))
                c = jnp.concatenate([head, conv(up, r1, r2)[8:]], axis=0)
            else:
                c = conv(up, *fix(pos, r1, r2))
            halves.append(c)
        act_scr[:, j * tn:(j + 1) * tn] = (jax.nn.silu(halves[0]) * halves[1]).astype(BF16)
    o_ref[...] = x + _rms(_dot(act_scr[...], wdn_ref[...]), gpost_ref[...])


def _ffn(x, n_seq, g_pre, w_up, conv_w, conv_b, w_down, g_post, buf, tm):
    R, D = x.shape
    L = R // n_seq
    C2 = 2 * D_FF
    if L >= tm:
        ns, nt = 1, L // tm
        grid = (n_seq, nt)
        buf_spec = pl.BlockSpec((1, 2, C2), lambda b, t: (b, 0, 0))
    else:
        ns, nt, tm = n_seq, 1, R
        grid = (1, 1)
        buf_spec = pl.BlockSpec((ns, 2, C2), lambda b, t: (0, 0, 0))
    const = lambda shape: pl.BlockSpec(shape, lambda b, t: (0,) * len(shape), pipeline_mode=pl.Buffered(1))
    return pl.pallas_call(
        functools.partial(_ffn_kernel, ns=ns, tn=256), grid=grid,
        in_specs=[pl.BlockSpec((tm, D), lambda b, t: (b * nt + t, 0)), const((1, D)), const((D, C2)),
                  const((CONV_W, C2)), const((1, C2)), const((D_FF, D)), const((1, D)), buf_spec],
        out_specs=[pl.BlockSpec((tm, D), lambda b, t: (b * nt + t, 0)), buf_spec],
        out_shape=[jax.ShapeDtypeStruct((R, D), F32), jax.ShapeDtypeStruct((n_seq, 2, C2), F32)],
        scratch_shapes=[pltpu.VMEM((tm, D_FF), BF16), pltpu.VMEM((2, C2), F32)],
        compiler_params=_cparams(2), name="conv_ffn",
    )(x, g_pre.reshape(1, D), w_up, conv_w, conv_b.reshape(1, C2), w_down, g_post.reshape(1, D), buf)


def _pad_rows(x, n):
    if x.shape[0] >= n:
        return x
    return jnp.concatenate([x, jnp.zeros((n - x.shape[0],) + x.shape[1:], x.dtype)], axis=0)


def _hgrn_kernel(x_ref, gpre_ref, w_ref, lb_ref, ng_ref, s0_ref, o_ref, so_ref, st_scr, *, chunk, nseq):
    t = pl.program_id(1)
    tc = x_ref.shape[0]
    hp = HG_HEADS
    W = hp * HG_DK
    C = chunk
    tp = max(tc, 16)
    h = _pad_rows(_rms(x_ref[...], gpre_ref[...]), tp).astype(BF16)
    zh = _dot(h, w_ref[...])
    zq, zf, zi, zg = (zh[:, s * W:(s + 1) * W] for s in range(4))

    if nseq == 1:
        @pl.when(t == 0)
        def _():
            for hh in range(hp):
                st_scr[hh] = s0_ref[0, hh].T

    n_ch = tc // C
    r = lax.broadcasted_iota(I32, (tp, tp), 0)
    c = lax.broadcasted_iota(I32, (tp, tp), 1)
    causal = (c <= r) & ((r // C) == (c // C))
    lmat = jnp.where(causal, 1.0, 0.0).astype(BF16)
    in_blk = ((lax.broadcasted_iota(I32, (tp, n_ch * HG_DK), 0) // C)
              == (lax.broadcasted_iota(I32, (tp, n_ch * HG_DK), 1) // HG_DK))

    lb = lb_ref[...]
    logf = jnp.log(lb + (1.0 - lb) * jax.nn.sigmoid(zf))
    k = (1.0 - lb) * jax.nn.sigmoid(-zf)
    q = jax.nn.silu(zq)
    hi = logf.astype(BF16)
    r1 = logf - hi.astype(F32)
    mid = r1.astype(BF16)
    lo = (r1 - mid.astype(F32)).astype(BF16)
    b = _dot(lmat, hi) + _dot(lmat, mid) + _dot(lmat, lo)
    last = [b[(ci + 1) * C - 1:(ci + 1) * C, :] for ci in range(tp // C)]
    bl = jnp.concatenate([jnp.broadcast_to(x, (C, x.shape[1])) for x in last], axis=0)
    k_out = k * jnp.exp(bl - b)
    q_rel = (q * jnp.exp(b - bl)).astype(BF16)
    decay = [jnp.exp(x) for x in last[:n_ch]]
    mrows = max(C, 16)
    q_in = _pad_rows(q * jnp.exp(b), tp + mrows - C).astype(BF16)

    for hh in range(hp):
        lanes = slice(hh * HG_DK, (hh + 1) * HG_DK)
        vb = zi[:, lanes].astype(BF16)
        ko = k_out[:, lanes]
        attn = jnp.where(causal, _dot(q_rel[:, lanes], ko.astype(BF16), NT), 0.0)
        o = _dot(attn.astype(BF16), vb)
        kx = jnp.where(in_blk, jnp.concatenate([ko] * n_ch, axis=1), 0.0).astype(BF16)
        ut_all = _dot(vb, kx, TN)
        st = st_scr[hh] if nseq == 1 else None
        inter = []
        for ci in range(n_ch):
            if nseq > 1:
                st = s0_ref[ci, hh].T
            inter.append(_dot(q_in[ci * C:ci * C + mrows, lanes], st.astype(BF16), NT)[0:C])
            st = decay[ci][:, lanes] * st + ut_all[:, ci * HG_DK:(ci + 1) * HG_DK]
            if nseq > 1:
                so_ref[ci, hh] = st.T
        o = (o[0:tc] + (inter[0] if n_ch == 1 else jnp.concatenate(inter, axis=0)))
        o_ref[:, lanes] = _rms(o, ng_ref[...]) * jax.nn.silu(zg[0:tc, lanes])
        if nseq == 1:
            st_scr[hh] = st

            @pl.when(t == pl.num_programs(1) - 1)
            def _():
                so_ref[0, hh] = st.T


def _hgrn(x, g_pre, w_hg, lbs, norm_g, s0, B, T):
    C = math.gcd(T, HG_CHUNK)
    nseq = math.gcd(B, 4) if T == C and T < 32 else 1
    tc = min(T, 256) * nseq
    nt = T * nseq // tc
    B = B // nseq
    D = x.shape[1]
    W = HG_HEADS * HG_DK
    const = lambda shape: pl.BlockSpec(shape, lambda b, t: (0, 0), pipeline_mode=pl.Buffered(1))
    st_spec = pl.BlockSpec((nseq, HG_HEADS, HG_DK, HG_DK), lambda b, t: (b, 0, 0, 0))
    return pl.pallas_call(
        functools.partial(_hgrn_kernel, chunk=C, nseq=nseq), grid=(B, nt),
        in_specs=[pl.BlockSpec((tc, D), lambda b, t: (b * nt + t, 0)), const((1, D)), const((D, 4 * W)),
                  const((1, W)), const((1, HG_DK)), st_spec],
        out_specs=[pl.BlockSpec((tc, W), lambda b, t: (b * nt + t, 0)), st_spec],
        out_shape=[jax.ShapeDtypeStruct((x.shape[0], W), F32), jax.ShapeDtypeStruct(s0.shape, F32)],
        scratch_shapes=[pltpu.VMEM((HG_HEADS, HG_DK, HG_DK), F32)],
        compiler_params=_cparams(2), name="hgrn2",
    )(x, g_pre.reshape(1, D), w_hg, lbs.reshape(1, W), norm_g.reshape(1, HG_DK), s0)


def _topk_rows(score, k):
    n = score.shape[0]
    rid = lax.broadcasted_iota(I32, score.shape, 0)
    sel = jnp.zeros(score.shape, F32)
    s = score
    for _ in range(k):
        m = jnp.max(s, axis=0, keepdims=True)
        first = jnp.min(jnp.where(s == m, rid, n), axis=0, keepdims=True)
        hit = rid == first
        sel = jnp.where(hit, 1.0, sel)
        s = jnp.where(hit, -jnp.inf, s)
    return sel


def _softmax2_rows(s, mask):
    s = jnp.where(mask, s, NEG)
    e = jnp.where(mask, jnp.exp2(s - jnp.max(s, axis=0, keepdims=True)), 0.0)
    return e / jnp.maximum(jnp.sum(e, axis=0, keepdims=True), 1e-30)


def _online2(carry, s, v_aug):
    m, acc = carry
    m_new = jnp.maximum(m, jnp.max(s, axis=0, keepdims=True))
    p = jnp.exp2(s - m_new)
    return m_new, jnp.exp2(m - m_new) * acc + _dot(v_aug, p.astype(BF16))


def _block_means_even_odd(x):
    s = 1.0 / CMP_BLOCK
    return (x[:, 0:CMP_BLOCK, :].sum(axis=1) * s, x[:, CMP_BLOCK:SEL_BLOCK, :].sum(axis=1) * s)


def _proj_prompt_kernel(x_ref, g_ref, wk_ref, wt_ref, wg_ref,
                        kc_ref, ksw_ref, qt_ref, kvc_ref, kvs_ref, kvw_ref, win_ref, ngt_ref):
    nq = qt_ref.shape[0]
    h = _rms(x_ref[...], g_ref[...]).astype(BF16)
    kn = _dot(h, wk_ref[...])
    kc_ref[...] = kn[:, 0:KV_W]
    ksw_ref[...] = kn[:, KV_W:3 * KV_W].astype(BF16)
    t = _dot(wt_ref[...], h, NT)
    qt_ref[...] = t[0:nq].astype(BF16)
    for i, ref in enumerate((kvc_ref, kvs_ref, kvw_ref)):
        ref[0] = t[nq + i * KV_ROW:nq + (i + 1) * KV_ROW]
    win_ref[0] = t[nq + 2 * KV_ROW:nq + 3 * KV_ROW]
    ngt_ref[...] = _dot(wg_ref[...], h, NT)


def _proj_prompt(x, g_pre, w_k, w_t, w_ngt, B, T):
    R, D = x.shape
    tm = min(WINDOW, T)
    nt = T // tm
    nq = NSA_HEADS * NSA_HD
    const = lambda a: pl.BlockSpec(a.shape, lambda b, t: (0, 0), pipeline_mode=pl.Buffered(1))
    kvt_spec = pl.BlockSpec((1, KV_ROW, tm), lambda b, t: (b, 0, t))
    kvt_shape = jax.ShapeDtypeStruct((B, KV_ROW, T), F32)
    return pl.pallas_call(
        _proj_prompt_kernel, grid=(B, nt),
        in_specs=[pl.BlockSpec((tm, D), lambda b, t: (b * nt + t, 0)), pl.BlockSpec((1, D), lambda b, t: (0, 0)),
                  const(w_k), const(w_t), const(w_ngt)],
        out_specs=[pl.BlockSpec((tm, KV_W), lambda b, t: (b * nt + t, 0)),
                   pl.BlockSpec((tm, 2 * KV_W), lambda b, t: (b * nt + t, 0)),
                   pl.BlockSpec((nq, tm), lambda b, t: (0, b * nt + t)),
                   kvt_spec, kvt_spec, kvt_spec,
                   pl.BlockSpec((1, KV_ROW, tm), lambda b, t: (b, 0, 0)),
                   pl.BlockSpec((w_ngt.shape[0], tm), lambda b, t: (0, b * nt + t))],
        out_shape=[jax.ShapeDtypeStruct((R, KV_W), F32), jax.ShapeDtypeStruct((R, 2 * KV_W), BF16),
                   jax.ShapeDtypeStruct((nq, R), BF16), kvt_shape, kvt_shape, kvt_shape,
                   jax.ShapeDtypeStruct((B, KV_ROW, tm), F32),
                   jax.ShapeDtypeStruct((w_ngt.shape[0], R), F32)],
        compiler_params=_cparams(2), name="proj_prompt",
    )(x, g_pre.reshape(1, D), w_k, w_t, w_ngt)


def _cmp_prep_kernel(kc_ref, vct_ref, km_ref, vmt_ref):
    hb = kc_ref.shape[0]
    nb = 2 * hb
    T = hb * SEL_BLOCK
    ev, od = _block_means_even_odd(kc_ref[...])
    km_ref[0:hb, :] = ev
    km_ref[hb:nb, :] = od
    ti = lax.broadcasted_iota(I32, (T, nb), 0)
    ci = lax.broadcasted_iota(I32, (T, nb), 1)
    blk = jnp.where(ci < hb, 2 * ci, 2 * (ci - hb) + 1)
    pool = jnp.where((ti // CMP_BLOCK) == blk, 1.0 / CMP_BLOCK, 0.0).astype(BF16)
    vmt_ref[...] = _dot(vct_ref[0].astype(BF16), pool)


def _cmp_prep(kc, kvt, B, T):
    nb = T // CMP_BLOCK
    hb = T // SEL_BLOCK
    return pl.pallas_call(
        _cmp_prep_kernel, grid=(B,),
        in_specs=[pl.BlockSpec((hb, SEL_BLOCK, KV_W), lambda b: (b, 0, 0)),
                  pl.BlockSpec((1, KV_W, T), lambda b: (b, 1, 0))],
        out_specs=[pl.BlockSpec((nb, KV_W), lambda b: (b, 0)), pl.BlockSpec((KV_W, nb), lambda b: (b, 0))],
        out_shape=[jax.ShapeDtypeStruct((B * nb, KV_W), F32), jax.ShapeDtypeStruct((B * KV_W, nb), F32)],
        compiler_params=_cparams(1), name="cmp_prep",
    )(kc.reshape(B * hb, SEL_BLOCK, KV_W), kvt)


def _nsa_prompt_kernel(qt_ref, km_ref, vmt_ref, ks_ref, vst_ref, kw_ref, vwt_ref, ngt_ref, tri_ref, o_ref,
                       *, tq, tk):
    g = pl.program_id(1)
    qi = pl.program_id(2)
    par = g % 2
    L = GROUP * tq
    nblk = km_ref.shape[0]
    nsel = nblk // 2

    qt = qt_ref[...]
    qs = jnp.concatenate([qt[r * NSA_HD:(r + 1) * NSA_HD, :] for r in range(GROUP)], axis=1)
    zq = jnp.zeros_like(qs)
    qp = jnp.concatenate([jnp.where(par == 0, qs, zq), jnp.where(par == 1, qs, zq)], axis=0)

    lane = lax.broadcasted_iota(I32, (1, L), 1)
    qpos = qi * tq + (lane % tq)

    row = lax.broadcasted_iota(I32, (nblk, 1), 0)
    cblk = jnp.where(row < nsel, 2 * row, 2 * (row - nsel) + 1)
    vis = ((cblk + 1) * CMP_BLOCK - 1) <= qpos
    p_c = _softmax2_rows(_dot(km_ref[...].astype(BF16), qp), vis)
    o_c = _dot(vmt_ref[...].astype(BF16), p_c.astype(BF16))
    imp = p_c[:, 0:tq]
    for r in range(1, GROUP):
        imp = imp + p_c[:, r * tq:(r + 1) * tq]
    imp = imp[0:nsel] + imp[nsel:nblk]
    sblk = lax.broadcasted_iota(I32, (nsel, 1), 0)
    cur = qpos[:, 0:tq] // SEL_BLOCK
    forced = (sblk == 0) | (sblk == cur)
    score = jnp.where(sblk <= cur, imp + jnp.where(forced, FORCE_BONUS, 0.0), NEG)
    keep = (_topk_rows(score, min(SEL_TOPK, nsel)) > 0.5) & (score > NEG / 2)
    bias = jnp.where(keep, 0.0, NEG)
    bias = jnp.concatenate([bias] * GROUP, axis=1).astype(BF16)
    qa = jnp.concatenate([qp, bias, jnp.zeros((128 - nsel, L), BF16)], axis=0)

    NR = NSA_HD + 16
    init = (jnp.full((1, L), M_INIT, F32), jnp.zeros((NR, L), F32))
    sub = tq // tk
    spt = tk // SEL_BLOCK
    wt = WINDOW // tk
    ones = jnp.ones((16, tk), BF16)
    kblk = lax.broadcasted_iota(I32, (tk, 1), 0) // SEL_BLOCK
    lane128 = lax.broadcasted_iota(I32, (1, 128), 1)

    def tile(k_ref, vt_ref, kt, with_bias):
        start = pl.multiple_of(kt * tk, tk)
        kb = k_ref[pl.ds(start, tk), :]
        if with_bias:
            hot = jnp.where(lane128 == kt * spt + kblk, 1.0, 0.0).astype(BF16)
            s = _dot(jnp.concatenate([kb, hot], axis=1), qa)
        else:
            s = _dot(kb, qp)
        return s, jnp.concatenate([vt_ref[0, :, pl.ds(start, tk)].astype(BF16), ones], axis=0)

    def finish(acc):
        return acc[0:NSA_HD] / jnp.maximum(acc[NSA_HD:NSA_HD + 1], 1e-30)

    def branch(k_ref, vt_ref, with_bias, first, n_prev, far_mask):
        carry = init
        for j in range(sub):
            s, v_a = tile(k_ref, vt_ref, qi * sub + j, with_bias)
            carry = _online2(carry, s + tri_ref[j], v_a)

        def one(kt, c):
            s, v_a = tile(k_ref, vt_ref, kt, with_bias)
            if far_mask:
                j = kt - (qi * sub - wt)
                s = s + tri_ref[jnp.where((j >= 0) & (j < sub), sub + 1 + j, sub)]
            return _online2(c, s, v_a)

        carry = lax.fori_loop(0, n_prev // 2, lambda i, c: one(first + 2 * i + 1, one(first + 2 * i, c)), carry)
        if sub % 2:
            carry = lax.cond(n_prev % 2 == 1, lambda c: one(first + n_prev - 1, c), lambda c: c, carry)
        return finish(carry[1])

    o_s = branch(ks_ref, vst_ref, True, 0, qi * sub, False)
    n_w = jnp.minimum(qi * sub, wt)
    o_w = branch(kw_ref, vwt_ref, False, qi * sub - n_w, n_w, True)

    gts = jax.nn.sigmoid(ngt_ref[...])
    grow = lambda i: jnp.concatenate([gts[i * GROUP + r:i * GROUP + r + 1, :] for r in range(GROUP)], axis=1)
    o = grow(0) * o_c + grow(1) * o_s + grow(2) * o_w
    for pr in range(GROUP // 2):
        two = jnp.concatenate([o[:, (2 * pr) * tq:(2 * pr + 1) * tq], o[:, (2 * pr + 1) * tq:(2 * pr + 2) * tq]], axis=0)
        o_ref[:, pr * 128:(pr + 1) * 128] = two.T


def _nsa_prompt(qt, kmean, vmt, ksw, kvt_s, kvt_w, ngt, B, T, tq=512, tk=256):
    tq = min(tq, T)
    nq = T // tq
    nblk = T // CMP_BLOCK
    nsel = nblk // 2
    L = GROUP * tq
    GW = GROUP * NSA_HD
    sub = tq // tk
    assert tq % tk == 0 and tk % SEL_BLOCK == 0 and WINDOW % tq == 0 and nsel <= 128
    vrow = pl.BlockSpec((1, NSA_HD, T), lambda b, g, q: (b, KVH + g, 0))
    kslab = lambda i: pl.BlockSpec((T, 128), lambda b, g, q: (b, (i - 1) * (KVH // 2) + g // 2))
    qc = (jnp.arange(L) % tq)[None, :]
    kr = [jnp.arange(tk)[:, None] + j * tk for j in range(sub)]
    tri = jnp.stack([jnp.where(k <= qc, 0.0, NEG) for k in kr] + [jnp.zeros((tk, L))]
                    + [jnp.where(k > qc, 0.0, NEG) for k in kr]).astype(F32)
    return pl.pallas_call(
        functools.partial(_nsa_prompt_kernel, tq=tq, tk=tk), grid=(B, KVH, nq),
        in_specs=[pl.BlockSpec((GW, tq), lambda b, g, q: (g, b * nq + q)),
                  pl.BlockSpec((nblk, 128), lambda b, g, q: (b, g // 2)),
                  pl.BlockSpec((NSA_HD, nblk), lambda b, g, q: (b * KVH + g, 0)),
                  kslab(1), vrow, kslab(2), vrow,
                  pl.BlockSpec((16, tq), lambda b, g, q: (g, b * nq + q)),
                  pl.BlockSpec((2 * sub + 1, tk, L), lambda b, g, q: (0, 0, 0), pipeline_mode=pl.Buffered(1))],
        out_specs=pl.BlockSpec((tq, GW), lambda b, g, q: (b * nq + q, g)),
        out_shape=jax.ShapeDtypeStruct((B * T, NSA_HEADS * NSA_HD), F32),
        compiler_params=_cparams(3), name="nsa_prompt",
    )(qt, kmean, vmt, ksw, kvt_s, ksw, kvt_w, ngt, tri)


def _q_rows(qx_ref):
    T = qx_ref.shape[0]
    parts = [qx_ref[:, (g * GROUP + r) * KV_W:(g * GROUP + r + 1) * KV_W] for r in range(GROUP) for g in range(KVH)]
    return jnp.concatenate(parts, axis=0).astype(BF16)


def _heads_to_row(o, T):
    parts = []
    for g in range(KVH):
        for r in range(GROUP):
            r0 = (r * KVH + g) * T
            parts.append(o[r0:r0 + T, g * NSA_HD:(g + 1) * NSA_HD])
    return jnp.concatenate(parts, axis=1)


def _softmax_lanes(s, mask):
    s = jnp.where(mask, s, NEG)
    e = jnp.where(mask, jnp.exp(s - jnp.max(s, axis=1, keepdims=True)), 0.0)
    return e / jnp.maximum(jnp.sum(e, axis=1, keepdims=True), 1e-30)


def _topk_lanes(score, jid, cand, k):
    sel = jnp.zeros(score.shape, F32)
    s = score
    for _ in range(k):
        m = jnp.max(s, axis=1, keepdims=True)
        first = jnp.min(jnp.where((s == m) & cand, jid, 3.0e38), axis=1, keepdims=True)
        hit = (jid == first) & cand
        sel = jnp.where(hit, 1.0, sel)
        s = jnp.where(hit, -jnp.inf, s)
    return sel


def _online_lanes(carry, s, v_t):
    m, l, acc = carry
    m_new = jnp.maximum(m, jnp.max(s, axis=1, keepdims=True))
    alpha = jnp.exp(m - m_new)
    p = jnp.exp(s - m_new)
    return (m_new, alpha * l + jnp.sum(p, axis=1, keepdims=True),
            alpha * acc + _dot(p.astype(BF16), v_t, NT))


def _s_cmp_kernel(pt_ref, *refs, n_pg, past):
    pages = refs[:n_pg]
    qx_ref, pool_ref, oc_ref, bias_ref, km_scr = refs[n_pg:]
    c = pl.program_id(1)
    nch = pl.num_programs(1)
    T = qx_ref.shape[0]
    per = n_pg * (PAGE // CMP_BLOCK)
    hs = per // 2
    x = jnp.concatenate([pg[0].astype(BF16) for pg in pages], axis=1)
    km_scr[c] = _dot(x, pool_ref[...])

    @pl.when(c == nch - 1)
    def _():
        n_steps = km_scr.shape[0]
        nblk = n_steps * per
        npast = nblk // 2
        gl = KVH * T
        kvm = jnp.concatenate([km_scr[i] for i in range(n_steps)], axis=1)
        qr = _q_rows(qx_ref)
        row = lax.broadcasted_iota(I32, (NSA_HEADS * T, 1), 0)
        qpos = past + (row % T)
        lane = lax.broadcasted_iota(I32, (1, nblk), 1)
        jsel = (lane // per) * hs + (lane % hs)
        odd = (lane // hs) % 2
        vis = ((2 * jsel + odd + 1) * CMP_BLOCK - 1) <= qpos
        p = _softmax_lanes(_dot(qr, kvm[0:KV_W].astype(BF16)), vis)
        o_c = _dot(p.astype(BF16), kvm[KV_W:KV_ROW].astype(BF16), NT)
        oc_ref[...] = _heads_to_row(o_c, T)
        imp = p[0:gl] + p[gl:2 * gl] + p[2 * gl:3 * gl] + p[3 * gl:4 * gl]
        imp = imp + pltpu.roll(imp, nblk - hs, 1)
        imp = jnp.concatenate([imp, jnp.zeros((gl, 128), F32)], axis=1)
        lane2 = lax.broadcasted_iota(I32, (1, nblk + 128), 1)
        jid = jnp.where(lane2 < nblk, (lane2 // per) * hs + (lane2 % hs), npast + lane2 - nblk)
        cand = ((lane2 < nblk) & (((lane2 // hs) % 2) == 0)) | (lane2 == nblk)
        cur = (past + (lax.broadcasted_iota(I32, (gl, 1), 0) % T)) // SEL_BLOCK
        forced = (jid == 0) | (jid == cur)
        score = jnp.where(cand, jnp.where(jid <= cur, imp + jnp.where(forced, FORCE_BONUS, 0.0), NEG), -jnp.inf)
        keep = (_topk_lanes(score, jid.astype(F32), cand, SEL_TOPK) > 0.5) & (score > NEG / 2)
        bias = jnp.where(keep, 0.0, NEG)
        bias = jnp.concatenate([bias] * GROUP, axis=0)
        zpad = jnp.zeros((NSA_HEADS * T, 128 - hs), F32)
        for i in range(n_steps):
            bias_ref[0, i] = jnp.concatenate([bias[:, i * per:i * per + hs], zpad], axis=1)
        bias_ref[0, n_steps] = bias[:, nblk:nblk + 128]


def _pool_matrix(n_pg):
    rows = n_pg * PAGE
    per = rows // CMP_BLOCK
    blk = jnp.arange(rows) // CMP_BLOCK
    col = (blk % 2) * (per // 2) + blk // 2
    return jnp.where(col[:, None] == jnp.arange(per)[None, :], 1.0 / CMP_BLOCK, 0.0).astype(BF16)


def _s_cmp(page_table, pool_t, qx, DB, T, n_pg=32):
    n_pages = page_table.shape[1]
    past = n_pages * PAGE
    n_pg = min(n_pg, n_pages)
    nch = n_pages // n_pg
    per = n_pg * (PAGE // CMP_BLOCK)
    L = NSA_HEADS * T
    page_spec = lambda i: pl.BlockSpec((1, KV_ROW, PAGE), lambda b, c, pt: (pt[b, c * n_pg + i], 0, 0))
    gs = pltpu.PrefetchScalarGridSpec(
        num_scalar_prefetch=1, grid=(DB, nch),
        in_specs=[page_spec(i) for i in range(n_pg)] + [
            pl.BlockSpec((T, NSA_HEADS * KV_W), lambda b, c, pt: (b, 0)),
            pl.BlockSpec((n_pg * PAGE, per), lambda b, c, pt: (0, 0))],
        out_specs=[pl.BlockSpec((T, NSA_HEADS * NSA_HD), lambda b, c, pt: (b, 0)),
                   pl.BlockSpec((1, nch + 1, L, 128), lambda b, c, pt: (b, 0, 0, 0))],
        scratch_shapes=[pltpu.VMEM((nch, KV_ROW, per), F32)])
    return pl.pallas_call(
        functools.partial(_s_cmp_kernel, n_pg=n_pg, past=past), grid_spec=gs,
        out_shape=[jax.ShapeDtypeStruct((DB * T, NSA_HEADS * NSA_HD), F32),
                   jax.ShapeDtypeStruct((DB, nch + 1, L, 128), F32)],
        compiler_params=_cparams(2), name="sample_cmp",
    )(page_table, *([pool_t] * n_pg), qx, _pool_matrix(n_pg))


def _s_attn_kernel(pt_ref, *refs, n_pg, past):
    pages = refs[:n_pg]
    (qx_ref, bias_ref, exp_ref, win_ref, ksn_ref, kwn_ref, oc_ref, gr_ref,
     o_ref, wout_ref, m_scr, l_scr, acc_scr) = refs[n_pg:]
    c = pl.program_id(1)
    n_steps = pl.num_programs(1)
    T = qx_ref.shape[0]
    L = NSA_HEADS * T
    qr = _q_rows(qx_ref)

    @pl.when(c == 0)
    def _():
        m_scr[...] = jnp.full(m_scr.shape, M_INIT, F32)
        l_scr[...] = jnp.zeros_like(l_scr)
        acc_scr[...] = jnp.zeros_like(acc_scr)

    k_t = jnp.concatenate([pg[0, 0:KV_W, :].astype(BF16) for pg in pages], axis=1)
    v_t = jnp.concatenate([pg[0, KV_W:KV_ROW, :].astype(BF16) for pg in pages], axis=1)
    s = _dot(qr, k_t) + _dot(bias_ref[0, c].astype(BF16), exp_ref[...])
    m, l, acc = _online_lanes((m_scr[...], l_scr[...], acc_scr[...]), s, v_t)
    m_scr[...] = m
    l_scr[...] = l
    acc_scr[...] = acc

    @pl.when(c == n_steps - 1)
    def _():
        row = lax.broadcasted_iota(I32, (L, 1), 0)
        tq = row % T
        pad = 128
        jl = lax.broadcasted_iota(I32, (1, pad), 1)
        new_ok = (jl < T) & (jl <= tq)
        zpad = jnp.zeros((pad - T, KV_ROW), F32)

        def new_rows(ref):
            t = jnp.concatenate([ref[...], zpad], axis=0).T
            return t[0:KV_W].astype(BF16), t[KV_W:KV_ROW].astype(BF16), t

        k_t, v_t, _ = new_rows(ksn_ref)
        s = jnp.where(new_ok, _dot(qr, k_t), NEG) + bias_ref[0, n_steps][:, 0:1]
        m_s, l_s, a_s = _online_lanes((m_scr[...], l_scr[...], acc_scr[...]), s, v_t)
        o_s = a_s / jnp.maximum(l_s, 1e-30)
        wb = win_ref.shape[2]
        win = win_ref[0]
        il = lax.broadcasted_iota(I32, (1, wb), 1)
        d = (past + tq) - (past - wb + il)
        init = (jnp.full((L, 1), M_INIT, F32), jnp.zeros((L, 1), F32), jnp.zeros((L, KV_W), F32))
        k_t, v_t, new_t = new_rows(kwn_ref)
        carry = _online_lanes(init, jnp.where(new_ok, _dot(qr, k_t), NEG), v_t)
        s = jnp.where((d >= 0) & (d < WINDOW), _dot(qr, win[0:KV_W].astype(BF16)), NEG)
        m_w, l_w, a_w = _online_lanes(carry, s, win[KV_W:KV_ROW].astype(BF16))
        o_w = a_w / jnp.maximum(l_w, 1e-30)
        W = NSA_HEADS * NSA_HD
        gts = jax.nn.sigmoid(gr_ref[...])
        o_ref[...] = (gts[:, 0:W] * oc_ref[...] + gts[:, W:2 * W] * _heads_to_row(o_s, T)
                      + gts[:, 2 * W:3 * W] * _heads_to_row(o_w, T))
        shifted = pltpu.roll(win, wb - T, 1)
        wout_ref[0, :, 0:wb - 128] = shifted[:, 0:wb - 128]
        tail = jnp.where(jl >= 128 - T, pltpu.roll(new_t, 128 - T, 1), shifted[:, wb - 128:wb])
        wout_ref[0, :, wb - 128:wb] = tail


def _s_attn(page_table, pool_t, qx, bias, win_t, kvs_new, kvw_new, o_c, g_rep, DB, T, n_pg=32):
    n_pages = page_table.shape[1]
    past = n_pages * PAGE
    n_pg = min(n_pg, n_pages)
    nch = n_pages // n_pg
    L = NSA_HEADS * T
    W = NSA_HEADS * NSA_HD
    keys = n_pg * PAGE
    expand = jnp.where(jnp.arange(128)[:, None] == (jnp.arange(keys) // SEL_BLOCK)[None, :], 1.0, 0.0).astype(BF16)
    page_spec = lambda i: pl.BlockSpec((1, KV_ROW, PAGE), lambda b, c, pt: (pt[b, c * n_pg + i], 0, 0))
    rowb = lambda w: pl.BlockSpec((T, w), lambda b, c, pt: (b, 0))
    per_b = lambda a: pl.BlockSpec((1,) + a.shape[1:], lambda b, c, pt: (b,) + (0,) * (a.ndim - 1))
    gs = pltpu.PrefetchScalarGridSpec(
        num_scalar_prefetch=1, grid=(DB, nch),
        in_specs=[page_spec(i) for i in range(n_pg)] + [
            rowb(NSA_HEADS * KV_W), per_b(bias), pl.BlockSpec((128, keys), lambda b, c, pt: (0, 0)),
            per_b(win_t), rowb(KV_ROW), rowb(KV_ROW), rowb(W), rowb(3 * W)],
        out_specs=[rowb(W), per_b(win_t)],
        scratch_shapes=[pltpu.VMEM((L, 1), F32), pltpu.VMEM((L, 1), F32), pltpu.VMEM((L, KV_W), F32)])
    return pl.pallas_call(
        functools.partial(_s_attn_kernel, n_pg=n_pg, past=past), grid_spec=gs,
        out_shape=[jax.ShapeDtypeStruct((DB * T, W), F32), jax.ShapeDtypeStruct(win_t.shape, F32)],
        compiler_params=_cparams(2), name="sample_attn",
    )(page_table, *([pool_t] * n_pg), qx, bias, expand, win_t, kvs_new, kvw_new, o_c, g_rep)


def _prep_weights(w_in):
    o_q = 4 * HG_HEADS * HG_DK
    o_kv = o_q + NSA_HEADS * NSA_HD
    o_ng = o_kv + 6 * KV_W
    o_mg = o_ng + NSA_HEADS * 3
    w_hg = w_in[:, :o_q].astype(BF16)
    w_q = w_in[:, o_q:o_kv] * ATT_SCALE
    w_kv = w_in[:, o_kv:o_ng].astype(BF16)
    w_ng = w_in[:, o_ng:o_mg]
    w_mg = w_in[:, o_mg:].astype(BF16)
    w_k = jnp.concatenate([w_in[:, o_kv + 2 * i * KV_W:o_kv + (2 * i + 1) * KV_W] for i in range(3)], axis=1)
    w_k = w_k.astype(BF16)
    w_t = jnp.concatenate([w_q * LOG2E, w_in[:, o_kv:o_ng]], axis=1).T.astype(BF16)
    ng4 = w_ng.reshape(D_MODEL, KVH, GROUP, 3).transpose(0, 1, 3, 2)
    ng4 = jnp.pad(ng4.reshape(D_MODEL, KVH, 3 * GROUP), ((0, 0), (0, 0), (0, 16 - 3 * GROUP)))
    w_ngt = ng4.reshape(D_MODEL, KVH * 16).T.astype(BF16)
    wq4 = w_q.reshape(D_MODEL, KVH, GROUP, NSA_HD)
    eye = jnp.eye(KVH, dtype=w_q.dtype)
    w_qx = jnp.einsum('dgrh,gk->dgrkh', wq4, eye).reshape(D_MODEL, NSA_HEADS * KV_W).astype(BF16)
    w_gr = jnp.repeat(w_ng.reshape(D_MODEL, NSA_HEADS, 3).transpose(0, 2, 1), NSA_HD, axis=2)
    w_gr = w_gr.reshape(D_MODEL, 3 * NSA_HEADS * NSA_HD).astype(BF16)
    return w_hg, w_kv, w_k, w_mg, w_t, w_ngt, w_qx, w_gr


def kernel(x_prompt, x_sample, cache_cmp_kv, cache_sel_kv, page_table, state_win_kv, state_hgrn, state_conv, w_in, hg_lb_logits, hg_norm_g, w_out, w_up, conv_w, conv_b, w_down, g_pre_mix, g_post_mix, g_pre_ffn, g_post_ffn):
    B, T, D = x_prompt.shape
    DB, TS, _ = x_sample.shape
    depth = w_in.shape[0]
    assert depth == 1 and D == D_MODEL and T % 256 == 0 and TS == 8 and TS < CMP_BLOCK
    C2 = 2 * D_FF
    l = 0

    lbs = jnp.cumsum(jax.nn.softmax(hg_lb_logits.astype(F32), axis=0), axis=0)[l]
    w_hg, w_kv, w_k, w_mg, w_t, w_ngt, w_qx, w_gr = _prep_weights(w_in[l])
    w_out_b, w_up_b, w_dn_b = w_out[l].astype(BF16), w_up[l].astype(BF16), w_down[l].astype(BF16)
    fmaj = lambda a: a.transpose(0, 2, 3, 4, 1).reshape(a.shape[0], KV_ROW, a.shape[1])
    rows6 = lambda a: a.reshape(a.shape[0], 2, KVH, NSA_HD, a.shape[2]).transpose(0, 4, 1, 2, 3)[None]

    outs = {}
    for name, x, nb, tl in (("p", x_prompt, B, T), ("s", x_sample, DB, TS)):
        R = nb * tl
        x2 = x.reshape(R, D)
        s0 = jnp.zeros((nb, HG_HEADS, HG_DK, HG_DK), F32) if name == "p" else state_hgrn[l]
        o_hg, s_hg = _hgrn(x2, g_pre_mix[l], w_hg, lbs, hg_norm_g[l], s0, nb, tl)
        if name == "p":
            kc, ksw, qt, kvt_c, kvt_s, kvt_w, win_t, ngt = _proj_prompt(x2, g_pre_mix[l], w_k, w_t, w_ngt, nb, tl)
            kmean, vmt = _cmp_prep(kc, kvt_c, nb, tl)
            o_nsa = _nsa_prompt(qt, kmean, vmt, ksw, kvt_s, kvt_w, ngt, nb, tl)
            kv_out = (rows6(kvt_c), rows6(kvt_s), rows6(win_t))
            cbuf0 = jnp.zeros((nb, CONV_W - 1, C2), F32)
        else:
            h = _rmsnorm_bf16(x2, g_pre_mix[l])
            kv = [_mm(h, w_kv[:, i * KV_ROW:(i + 1) * KV_ROW], F32, name="proj_kv") for i in range(3)]
            qx = _mm(h, w_qx, F32, name="proj_q_pad")
            g_rep = _mm(h, w_gr, F32, name="proj_gate_rep")
            o_c, bias = _s_cmp(page_table, fmaj(cache_cmp_kv[l]), qx, nb, tl)
            o_nsa, win_t = _s_attn(page_table, fmaj(cache_sel_kv[l]), qx, bias, fmaj(state_win_kv[l]),
                                   kv[1], kv[2], o_c, g_rep, nb, tl)
            kv6 = lambda a: a.reshape(1, nb, tl, 2, KVH, NSA_HD)
            kv_out = (kv6(kv[0]), kv6(kv[1]), rows6(win_t))
            cbuf0 = state_conv[l]
        x1 = _merge(x2, g_pre_mix[l], w_mg, o_hg, o_nsa, w_out_b, g_post_mix[l])
        x3, cbuf = _ffn(x1, nb, g_pre_ffn[l], w_up_b, conv_w[l], conv_b[l], w_dn_b, g_post_ffn[l], cbuf0, tm=1024)
        outs[name] = (x3.reshape(nb, tl, D),) + kv_out + (s_hg[None], cbuf[None])
    p, s = outs["p"], outs["s"]
    return (p[0], s[0], p[1], s[1], p[2], s[2], p[3], s[3], p[4], s[4], p[5], s[5])
```

```python
import functools
import math

import jax
import jax.numpy as jnp
from jax import lax
from jax.experimental import pallas as pl
from jax.experimental.pallas import tpu as pltpu

F32 = jnp.float32
BF16 = jnp.bfloat16
I32 = jnp.int32

D_MODEL = 1024
HG_HEADS = 8
HG_DK = 128
HG_CHUNK = 32
NSA_HEADS = 16
NSA_HD = 64
KVH = 4
GROUP = 4
KV_W = KVH * NSA_HD
KV_ROW = 2 * KV_W
CMP_BLOCK = 32
SEL_BLOCK = 64
SEL_TOPK = 8
WINDOW = 512
FORCE_BONUS = 1.0e4
ATT_SCALE = NSA_HD ** -0.5
LOG2E = math.log2(math.e)
D_FF = 2816
CONV_W = 3
PAGE = 128
EPS = 1e-6
NEG = -1e30
M_INIT = -1e38

VMEM_LIMIT = 56 * 1024 * 1024

NT = (((1,), (1,)), ((), ()))
TN = (((0,), (0,)), ((), ()))


def _cparams(n_axes):
    return pltpu.CompilerParams(dimension_semantics=("arbitrary",) * n_axes,
                                vmem_limit_bytes=VMEM_LIMIT)


def _dot(a, b, dims=None):
    if dims is None:
        return jnp.dot(a, b, preferred_element_type=F32)
    return lax.dot_general(a, b, dims, preferred_element_type=F32)


def _rms(x, g):
    return x * lax.rsqrt(jnp.mean(x * x, axis=-1, keepdims=True) + EPS) * g


def _rmsnorm_kernel(x_ref, g_ref, o_ref):
    o_ref[...] = _rms(x_ref[...], g_ref[...]).astype(o_ref.dtype)


def _rmsnorm_bf16(x, g, tm=512):
    R, D = x.shape
    tm = min(tm, R)
    return pl.pallas_call(
        _rmsnorm_kernel, grid=(R // tm,),
        in_specs=[pl.BlockSpec((tm, D), lambda i: (i, 0)), pl.BlockSpec((1, D), lambda i: (0, 0))],
        out_specs=pl.BlockSpec((tm, D), lambda i: (i, 0)),
        out_shape=jax.ShapeDtypeStruct((R, D), BF16),
        compiler_params=_cparams(1), name="rmsnorm",
    )(x, g.reshape(1, D))


def _mm_kernel(a_ref, w_ref, o_ref):
    o_ref[...] = _dot(a_ref[...], w_ref[...]).astype(o_ref.dtype)


def _mm(a, w, out_dtype, tm=1024, tn=512, name="proj"):
    R, K = a.shape
    N = w.shape[1]
    tm, tn = min(tm, R), min(tn, N)
    return pl.pallas_call(
        _mm_kernel, grid=(R // tm, N // tn),
        in_specs=[pl.BlockSpec((tm, K), lambda i, j: (i, 0)), pl.BlockSpec((K, tn), lambda i, j: (0, j))],
        out_specs=pl.BlockSpec((tm, tn), lambda i, j: (i, j)),
        out_shape=jax.ShapeDtypeStruct((R, N), out_dtype),
        compiler_params=_cparams(2), name=name,
    )(a, w)


def _merge_kernel(x_ref, gpre_ref, wmg_ref, ohg_ref, onsa_ref, w_ref, g_ref, o_ref):
    x = x_ref[...]
    mg = _dot(_rms(x, gpre_ref[...]).astype(BF16), wmg_ref[...])
    u = jax.nn.sigmoid(mg[:, :D_MODEL]) * ohg_ref[...] + jax.nn.sigmoid(mg[:, D_MODEL:]) * onsa_ref[...]
    y = _dot(u.astype(BF16), w_ref[...])
    o_ref[...] = x + _rms(y, g_ref[...])


def _merge(x, g_pre, w_mg, ohg, onsa, w_out, g_post, tm=512):
    R, D = x.shape
    tm = min(tm, R)
    row = pl.BlockSpec((tm, D), lambda i: (i, 0))
    const = lambda shape: pl.BlockSpec(shape, lambda i: (0, 0), pipeline_mode=pl.Buffered(1))
    return pl.pallas_call(
        _merge_kernel, grid=(R // tm,),
        in_specs=[row, const((1, D)), const((D, 2 * D)), row, row, const((D, D)), const((1, D))],
        out_specs=row,
        out_shape=jax.ShapeDtypeStruct((R, D), F32),
        compiler_params=_cparams(1), name="merge",
    )(x, g_pre.reshape(1, D), w_mg, ohg, onsa, w_out, g_post.reshape(1, D))


def _ffn_kernel(x_ref, gpre_ref, wup_ref, cw_ref, cb_ref, wdn_ref, gpost_ref, buf_ref,
                o_ref, cbuf_ref, act_scr, carry_scr, *, ns, tn):
    tm = x_ref.shape[0]
    ls = tm // ns
    x = x_ref[...]
    h = _rms(x, gpre_ref[...]).astype(BF16)
    pos = lax.broadcasted_iota(I32, (tm, 1), 0) % ls
    if ns == 1:
        @pl.when(pl.program_id(1) == 0)
        def _():
            carry_scr[...] = buf_ref[0]
    for j in range(D_FF // tn):
        halves = []
        for base in (j * tn, D_FF + j * tn):
            cols = slice(base, base + tn)
            up = _dot(h, wup_ref[:, cols])
            if ns == 1:
                b0 = carry_scr[0:1, cols]
                b1 = carry_scr[1:2, cols]
                carry_scr[:, cols] = up[tm - 2:tm, :]
                cbuf_ref[0, :, cols] = up[tm - 2:tm, :]
            else:
                bufv = buf_ref[:, :, cols]
                b0 = jnp.broadcast_to(bufv[:, 0:1, :], (ns, ls, tn)).reshape(tm, tn)
                b1 = jnp.broadcast_to(bufv[:, 1:2, :], (ns, ls, tn)).reshape(tm, tn)
                cbuf_ref[:, :, cols] = up.reshape(ns, ls, tn)[:, ls - 2:ls, :]
            def conv(x, x1, x2):
                c = cb_ref[:, cols] + x2 * cw_ref[0:1, cols]
                c = c + x1 * cw_ref[1:2, cols]
                return c + x * cw_ref[2:3, cols]

            def fix(p, x1, x2):
                return jnp.where(p == 0, b1, x1), jnp.where(p == 0, b0, jnp.where(p == 1, b1, x2))

            r1, r2 = pltpu.roll(up, 1, 0), pltpu.roll(up, 2, 0)
            if ns == 1:
                head = conv(up[0:8], *fix(pos[0:8], r1[0:8], r2[0:8]))
                c = jnp.concatenate([head, conv(up, r1, r2)[8:]], axis=0)
            else:
                c = conv(up, *fix(pos, r1, r2))
            halves.append(c)
        act_scr[:, j * tn:(j + 1) * tn] = (jax.nn.silu(halves[0]) * halves[1]).astype(BF16)
    o_ref[...] = x + _rms(_dot(act_scr[...], wdn_ref[...]), gpost_ref[...])


def _ffn(x, n_seq, g_pre, w_up, conv_w, conv_b, w_down, g_post, buf, tm):
    R, D = x.shape
    L = R // n_seq
    C2 = 2 * D_FF
    if L >= tm:
        ns, nt = 1, L // tm
        grid = (n_seq, nt)
        buf_spec = pl.BlockSpec((1, 2, C2), lambda b, t: (b, 0, 0))
    else:
        ns, nt, tm = n_seq, 1, R
        grid = (1, 1)
        buf_spec = pl.BlockSpec((ns, 2, C2), lambda b, t: (0, 0, 0))
    const = lambda shape: pl.BlockSpec(shape, lambda b, t: (0,) * len(shape), pipeline_mode=pl.Buffered(1))
    return pl.pallas_call(
        functools.partial(_ffn_kernel, ns=ns, tn=256), grid=grid,
        in_specs=[pl.BlockSpec((tm, D), lambda b, t: (b * nt + t, 0)), const((1, D)), const((D, C2)),
                  const((CONV_W, C2)), const((1, C2)), const((D_FF, D)), const((1, D)), buf_spec],
        out_specs=[pl.BlockSpec((tm, D), lambda b, t: (b * nt + t, 0)), buf_spec],
        out_shape=[jax.ShapeDtypeStruct((R, D), F32), jax.ShapeDtypeStruct((n_seq, 2, C2), F32)],
        scratch_shapes=[pltpu.VMEM((tm, D_FF), BF16), pltpu.VMEM((2, C2), F32)],
        compiler_params=_cparams(2), name="conv_ffn",
    )(x, g_pre.reshape(1, D), w_up, conv_w, conv_b.reshape(1, C2), w_down, g_post.reshape(1, D), buf)


def _pad_rows(x, n):
    if x.shape[0] >= n:
        return x
    return jnp.concatenate([x, jnp.zeros((n - x.shape[0],) + x.shape[1:], x.dtype)], axis=0)


def _hgrn_kernel(x_ref, gpre_ref, w_ref, lb_ref, ng_ref, s0_ref, o_ref, so_ref, st_scr, *, chunk, nseq):
    t = pl.program_id(1)
    tc = x_ref.shape[0]
    hp = HG_HEADS
    W = hp * HG_DK
    C = chunk
    tp = max(tc, 16)
    h = _pad_rows(_rms(x_ref[...], gpre_ref[...]), tp).astype(BF16)
    zh = _dot(h, w_ref[...])
    zq, zf, zi, zg = (zh[:, s * W:(s + 1) * W] for s in range(4))

    if nseq == 1:
        @pl.when(t == 0)
        def _():
            for hh in range(hp):
                st_scr[hh] = s0_ref[0, hh].T

    n_ch = tc // C
    r = lax.broadcasted_iota(I32, (tp, tp), 0)
    c = lax.broadcasted_iota(I32, (tp, tp), 1)
    causal = (c <= r) & ((r // C) == (c // C))
    lmat = jnp.where(causal, 1.0, 0.0).astype(BF16)
    in_blk = ((lax.broadcasted_iota(I32, (tp, n_ch * HG_DK), 0) // C)
              == (lax.broadcasted_iota(I32, (tp, n_ch * HG_DK), 1) // HG_DK))

    lb = lb_ref[...]
    logf = jnp.log(lb + (1.0 - lb) * jax.nn.sigmoid(zf))
    k = (1.0 - lb) * jax.nn.sigmoid(-zf)
    q = jax.nn.silu(zq)
    hi = logf.astype(BF16)
    r1 = logf - hi.astype(F32)
    mid = r1.astype(BF16)
    lo = (r1 - mid.astype(F32)).astype(BF16)
    b = _dot(lmat, hi) + _dot(lmat, mid) + _dot(lmat, lo)
    last = [b[(ci + 1) * C - 1:(ci + 1) * C, :] for ci in range(tp // C)]
    bl = jnp.concatenate([jnp.broadcast_to(x, (C, x.shape[1])) for x in last], axis=0)
    k_out = k * jnp.exp(bl - b)
    q_rel = (q * jnp.exp(b - bl)).astype(BF16)
    decay = [jnp.exp(x) for x in last[:n_ch]]
    mrows = max(C, 16)
    q_in = _pad_rows(q * jnp.exp(b), tp + mrows - C).astype(BF16)

    for hh in range(hp):
        lanes = slice(hh * HG_DK, (hh + 1) * HG_DK)
        vb = zi[:, lanes].astype(BF16)
        ko = k_out[:, lanes]
        attn = jnp.where(causal, _dot(q_rel[:, lanes], ko.astype(BF16), NT), 0.0)
        o = _dot(attn.astype(BF16), vb)
        kx = jnp.where(in_blk, jnp.concatenate([ko] * n_ch, axis=1), 0.0).astype(BF16)
        ut_all = _dot(vb, kx, TN)
        st = st_scr[hh] if nseq == 1 else None
        inter = []
        for ci in range(n_ch):
            if nseq > 1:
                st = s0_ref[ci, hh].T
            inter.append(_dot(q_in[ci * C:ci * C + mrows, lanes], st.astype(BF16), NT)[0:C])
            st = decay[ci][:, lanes] * st + ut_all[:, ci * HG_DK:(ci + 1) * HG_DK]
            if nseq > 1:
                so_ref[ci, hh] = st.T
        o = (o[0:tc] + (inter[0] if n_ch == 1 else jnp.concatenate(inter, axis=0)))
        o_ref[:, lanes] = _rms(o, ng_ref[...]) * jax.nn.silu(zg[0:tc, lanes])
        if nseq == 1:
            st_scr[hh] = st

            @pl.when(t == pl.num_programs(1) - 1)
            def _():
                so_ref[0, hh] = st.T


def _hgrn(x, g_pre, w_hg, lbs, norm_g, s0, B, T):
    C = math.gcd(T, HG_CHUNK)
    nseq = math.gcd(B, 4) if T == C and T < 32 else 1
    tc = min(T, 256) * nseq
    nt = T * nseq // tc
    B = B // nseq
    D = x.shape[1]
    W = HG_HEADS * HG_DK
    const = lambda shape: pl.BlockSpec(shape, lambda b, t: (0, 0), pipeline_mode=pl.Buffered(1))
    st_spec = pl.BlockSpec((nseq, HG_HEADS, HG_DK, HG_DK), lambda b, t: (b, 0, 0, 0))
    return pl.pallas_call(
        functools.partial(_hgrn_kernel, chunk=C, nseq=nseq), grid=(B, nt),
        in_specs=[pl.BlockSpec((tc, D), lambda b, t: (b * nt + t, 0)), const((1, D)), const((D, 4 * W)),
                  const((1, W)), const((1, HG_DK)), st_spec],
        out_specs=[pl.BlockSpec((tc, W), lambda b, t: (b * nt + t, 0)), st_spec],
        out_shape=[jax.ShapeDtypeStruct((x.shape[0], W), F32), jax.ShapeDtypeStruct(s0.shape, F32)],
        scratch_shapes=[pltpu.VMEM((HG_HEADS, HG_DK, HG_DK), F32)],
        compiler_params=_cparams(2), name="hgrn2",
    )(x, g_pre.reshape(1, D), w_hg, lbs.reshape(1, W), norm_g.reshape(1, HG_DK), s0)


def _topk_rows(score, k):
    n = score.shape[0]
    rid = lax.broadcasted_iota(I32, score.shape, 0)
    sel = jnp.zeros(score.shape, F32)
    s = score
    for _ in range(k):
        m = jnp.max(s, axis=0, keepdims=True)
        first = jnp.min(jnp.where(s == m, rid, n), axis=0, keepdims=True)
        hit = rid == first
        sel = jnp.where(hit, 1.0, sel)
        s = jnp.where(hit, -jnp.inf, s)
    return sel


def _softmax2_rows(s, mask):
    s = jnp.where(mask, s, NEG)
    e = jnp.where(mask, jnp.exp2(s - jnp.max(s, axis=0, keepdims=True)), 0.0)
    return e / jnp.maximum(jnp.sum(e, axis=0, keepdims=True), 1e-30)


def _online2(carry, s, v_aug):
    m, acc = carry
    m_new = jnp.maximum(m, jnp.max(s, axis=0, keepdims=True))
    p = jnp.exp2(s - m_new)
    return m_new, jnp.exp2(m - m_new) * acc + _dot(v_aug, p.astype(BF16))


def _block_means_even_odd(x):
    s = 1.0 / CMP_BLOCK
    return (x[:, 0:CMP_BLOCK, :].sum(axis=1) * s, x[:, CMP_BLOCK:SEL_BLOCK, :].sum(axis=1) * s)


def _proj_prompt_kernel(x_ref, g_ref, wk_ref, wt_ref, wg_ref,
                        kc_ref, ksw_ref, qt_ref, kvc_ref, kvs_ref, kvw_ref, win_ref, ngt_ref):
    nq = qt_ref.shape[0]
    h = _rms(x_ref[...], g_ref[...]).astype(BF16)
    kn = _dot(h, wk_ref[...])
    kc_ref[...] = kn[:, 0:KV_W]
    ksw_ref[...] = kn[:, KV_W:3 * KV_W].astype(BF16)
    t = _dot(wt_ref[...], h, NT)
    qt_ref[...] = t[0:nq].astype(BF16)
    for i, ref in enumerate((kvc_ref, kvs_ref, kvw_ref)):
        ref[0] = t[nq + i * KV_ROW:nq + (i + 1) * KV_ROW]
    win_ref[0] = t[nq + 2 * KV_ROW:nq + 3 * KV_ROW]
    ngt_ref[...] = _dot(wg_ref[...], h, NT)


def _proj_prompt(x, g_pre, w_k, w_t, w_ngt, B, T):
    R, D = x.shape
    tm = min(WINDOW, T)
    nt = T // tm
    nq = NSA_HEADS * NSA_HD
    const = lambda a: pl.BlockSpec(a.shape, lambda b, t: (0, 0), pipeline_mode=pl.Buffered(1))
    kvt_spec = pl.BlockSpec((1, KV_ROW, tm), lambda b, t: (b, 0, t))
    kvt_shape = jax.ShapeDtypeStruct((B, KV_ROW, T), F32)
    return pl.pallas_call(
        _proj_prompt_kernel, grid=(B, nt),
        in_specs=[pl.BlockSpec((tm, D), lambda b, t: (b * nt + t, 0)), pl.BlockSpec((1, D), lambda b, t: (0, 0)),
                  const(w_k), const(w_t), const(w_ngt)],
        out_specs=[pl.BlockSpec((tm, KV_W), lambda b, t: (b * nt + t, 0)),
                   pl.BlockSpec((tm, 2 * KV_W), lambda b, t: (b * nt + t, 0)),
                   pl.BlockSpec((nq, tm), lambda b, t: (0, b * nt + t)),
                   kvt_spec, kvt_spec, kvt_spec,
                   pl.BlockSpec((1, KV_ROW, tm), lambda b, t: (b, 0, 0)),
                   pl.BlockSpec((w_ngt.shape[0], tm), lambda b, t: (0, b * nt + t))],
        out_shape=[jax.ShapeDtypeStruct((R, KV_W), F32), jax.ShapeDtypeStruct((R, 2 * KV_W), BF16),
                   jax.ShapeDtypeStruct((nq, R), BF16), kvt_shape, kvt_shape, kvt_shape,
                   jax.ShapeDtypeStruct((B, KV_ROW, tm), F32),
                   jax.ShapeDtypeStruct((w_ngt.shape[0], R), F32)],
        compiler_params=_cparams(2), name="proj_prompt",
    )(x, g_pre.reshape(1, D), w_k, w_t, w_ngt)


def _cmp_prep_kernel(kc_ref, vct_ref, km_ref, vmt_ref):
    hb = kc_ref.shape[0]
    nb = 2 * hb
    T = hb * SEL_BLOCK
    ev, od = _block_means_even_odd(kc_ref[...])
    km_ref[0:hb, :] = ev
    km_ref[hb:nb, :] = od
    ti = lax.broadcasted_iota(I32, (T, nb), 0)
    ci = lax.broadcasted_iota(I32, (T, nb), 1)
    blk = jnp.where(ci < hb, 2 * ci, 2 * (ci - hb) + 1)
    pool = jnp.where((ti // CMP_BLOCK) == blk, 1.0 / CMP_BLOCK, 0.0).astype(BF16)
    vmt_ref[...] = _dot(vct_ref[0].astype(BF16), pool)


def _cmp_prep(kc, kvt, B, T):
    nb = T // CMP_BLOCK
    hb = T // SEL_BLOCK
    return pl.pallas_call(
        _cmp_prep_kernel, grid=(B,),
        in_specs=[pl.BlockSpec((hb, SEL_BLOCK, KV_W), lambda b: (b, 0, 0)),
                  pl.BlockSpec((1, KV_W, T), lambda b: (b, 1, 0))],
        out_specs=[pl.BlockSpec((nb, KV_W), lambda b: (b, 0)), pl.BlockSpec((KV_W, nb), lambda b: (b, 0))],
        out_shape=[jax.ShapeDtypeStruct((B * nb, KV_W), F32), jax.ShapeDtypeStruct((B * KV_W, nb), F32)],
        compiler_params=_cparams(1), name="cmp_prep",
    )(kc.reshape(B * hb, SEL_BLOCK, KV_W), kvt)


def _nsa_prompt_kernel(qt_ref, km_ref, vmt_ref, ks_ref, vst_ref, kw_ref, vwt_ref, ngt_ref, tri_ref, o_ref,
                       *, tq, tk):
    g = pl.program_id(1)
    qi = pl.program_id(2)
    par = g % 2
    L = GROUP * tq
    nblk = km_ref.shape[0]
    nsel = nblk // 2

    qt = qt_ref[...]
    qs = jnp.concatenate([qt[r * NSA_HD:(r + 1) * NSA_HD, :] for r in range(GROUP)], axis=1)
    zq = jnp.zeros_like(qs)
    qp = jnp.concatenate([jnp.where(par == 0, qs, zq), jnp.where(par == 1, qs, zq)], axis=0)

    lane = lax.broadcasted_iota(I32, (1, L), 1)
    qpos = qi * tq + (lane % tq)

    row = lax.broadcasted_iota(I32, (nblk, 1), 0)
    cblk = jnp.where(row < nsel, 2 * row, 2 * (row - nsel) + 1)
    vis = ((cblk + 1) * CMP_BLOCK - 1) <= qpos
    p_c = _softmax2_rows(_dot(km_ref[...].astype(BF16), qp), vis)
    o_c = _dot(vmt_ref[...].astype(BF16), p_c.astype(BF16))
    imp = p_c[:, 0:tq]
    for r in range(1, GROUP):
        imp = imp + p_c[:, r * tq:(r + 1) * tq]
    imp = imp[0:nsel] + imp[nsel:nblk]
    sblk = lax.broadcasted_iota(I32, (nsel, 1), 0)
    cur = qpos[:, 0:tq] // SEL_BLOCK
    forced = (sblk == 0) | (sblk == cur)
    score = jnp.where(sblk <= cur, imp + jnp.where(forced, FORCE_BONUS, 0.0), NEG)
    keep = (_topk_rows(score, min(SEL_TOPK, nsel)) > 0.5) & (score > NEG / 2)
    bias = jnp.where(keep, 0.0, NEG)
    bias = jnp.concatenate([bias] * GROUP, axis=1).astype(BF16)
    qa = jnp.concatenate([qp, bias, jnp.zeros((128 - nsel, L), BF16)], axis=0)

    NR = NSA_HD + 16
    init = (jnp.full((1, L), M_INIT, F32), jnp.zeros((NR, L), F32))
    sub = tq // tk
    spt = tk // SEL_BLOCK
    wt = WINDOW // tk
    ones = jnp.ones((16, tk), BF16)
    kblk = lax.broadcasted_iota(I32, (tk, 1), 0) // SEL_BLOCK
    lane128 = lax.broadcasted_iota(I32, (1, 128), 1)

    def tile(k_ref, vt_ref, kt, with_bias):
        start = pl.multiple_of(kt * tk, tk)
        kb = k_ref[pl.ds(start, tk), :]
        if with_bias:
            hot = jnp.where(lane128 == kt * spt + kblk, 1.0, 0.0).astype(BF16)
            s = _dot(jnp.concatenate([kb, hot], axis=1), qa)
        else:
            s = _dot(kb, qp)
        return s, jnp.concatenate([vt_ref[0, :, pl.ds(start, tk)].astype(BF16), ones], axis=0)

    def finish(acc):
        return acc[0:NSA_HD] / jnp.maximum(acc[NSA_HD:NSA_HD + 1], 1e-30)

    def branch(k_ref, vt_ref, with_bias, first, n_prev, far_mask):
        carry = init
        for j in range(sub):
            s, v_a = tile(k_ref, vt_ref, qi * sub + j, with_bias)
            carry = _online2(carry, s + tri_ref[j], v_a)

        def one(kt, c):
            s, v_a = tile(k_ref, vt_ref, kt, with_bias)
            if far_mask:
                j = kt - (qi * sub - wt)
                s = s + tri_ref[jnp.where((j >= 0) & (j < sub), sub + 1 + j, sub)]
            return _online2(c, s, v_a)

        carry = lax.fori_loop(0, n_prev // 2, lambda i, c: one(first + 2 * i + 1, one(first + 2 * i, c)), carry)
        if sub % 2:
            carry = lax.cond(n_prev % 2 == 1, lambda c: one(first + n_prev - 1, c), lambda c: c, carry)
        return finish(carry[1])

    o_s = branch(ks_ref, vst_ref, True, 0, qi * sub, False)
    n_w = jnp.minimum(qi * sub, wt)
    o_w = branch(kw_ref, vwt_ref, False, qi * sub - n_w, n_w, True)

    gts = jax.nn.sigmoid(ngt_ref[...])
    grow = lambda i: jnp.concatenate([gts[i * GROUP + r:i * GROUP + r + 1, :] for r in range(GROUP)], axis=1)
    o = grow(0) * o_c + grow(1) * o_s + grow(2) * o_w
    for pr in range(GROUP // 2):
        two = jnp.concatenate([o[:, (2 * pr) * tq:(2 * pr + 1) * tq], o[:, (2 * pr + 1) * tq:(2 * pr + 2) * tq]], axis=0)
        o_ref[:, pr * 128:(pr + 1) * 128] = two.T


def _nsa_prompt(qt, kmean, vmt, ksw, kvt_s, kvt_w, ngt, B, T, tq=512, tk=256):
    tq = min(tq, T)
    nq = T // tq
    nblk = T // CMP_BLOCK
    nsel = nblk // 2
    L = GROUP * tq
    GW = GROUP * NSA_HD
    sub = tq // tk
    assert tq % tk == 0 and tk % SEL_BLOCK == 0 and WINDOW % tq == 0 and nsel <= 128
    vrow = pl.BlockSpec((1, NSA_HD, T), lambda b, g, q: (b, KVH + g, 0))
    kslab = lambda i: pl.BlockSpec((T, 128), lambda b, g, q: (b, (i - 1) * (KVH // 2) + g // 2))
    qc = (jnp.arange(L) % tq)[None, :]
    kr = [jnp.arange(tk)[:, None] + j * tk for j in range(sub)]
    tri = jnp.stack([jnp.where(k <= qc, 0.0, NEG) for k in kr] + [jnp.zeros((tk, L))]
                    + [jnp.where(k > qc, 0.0, NEG) for k in kr]).astype(F32)
    return pl.pallas_call(
        functools.partial(_nsa_prompt_kernel, tq=tq, tk=tk), grid=(B, KVH, nq),
        in_specs=[pl.BlockSpec((GW, tq), lambda b, g, q: (g, b * nq + q)),
                  pl.BlockSpec((nblk, 128), lambda b, g, q: (b, g // 2)),
                  pl.BlockSpec((NSA_HD, nblk), lambda b, g, q: (b * KVH + g, 0)),
                  kslab(1), vrow, kslab(2), vrow,
                  pl.BlockSpec((16, tq), lambda b, g, q: (g, b * nq + q)),
                  pl.BlockSpec((2 * sub + 1, tk, L), lambda b, g, q: (0, 0, 0), pipeline_mode=pl.Buffered(1))],
        out_specs=pl.BlockSpec((tq, GW), lambda b, g, q: (b * nq + q, g)),
        out_shape=jax.ShapeDtypeStruct((B * T, NSA_HEADS * NSA_HD), F32),
        compiler_params=_cparams(3), name="nsa_prompt",
    )(qt, kmean, vmt, ksw, kvt_s, ksw, kvt_w, ngt, tri)


def _q_rows(qx_ref):
    T = qx_ref.shape[0]
    parts = [qx_ref[:, (g * GROUP + r) * KV_W:(g * GROUP + r + 1) * KV_W] for r in range(GROUP) for g in range(KVH)]
    return jnp.concatenate(parts, axis=0).astype(BF16)


def _heads_to_row(o, T):
    parts = []
    for g in range(KVH):
        for r in range(GROUP):
            r0 = (r * KVH + g) * T
            parts.append(o[r0:r0 + T, g * NSA_HD:(g + 1) * NSA_HD])
    return jnp.concatenate(parts, axis=1)


def _softmax_lanes(s, mask):
    s = jnp.where(mask, s, NEG)
    e = jnp.where(mask, jnp.exp(s - jnp.max(s, axis=1, keepdims=True)), 0.0)
    return e / jnp.maximum(jnp.sum(e, axis=1, keepdims=True), 1e-30)


def _topk_lanes(score, jid, cand, k):
    sel = jnp.zeros(score.shape, F32)
    s = score
    for _ in range(k):
        m = jnp.max(s, axis=1, keepdims=True)
        first = jnp.min(jnp.where((s == m) & cand, jid, 3.0e38), axis=1, keepdims=True)
        hit = (jid == first) & cand
        sel = jnp.where(hit, 1.0, sel)
        s = jnp.where(hit, -jnp.inf, s)
    return sel


def _online_lanes(carry, s, v_t):
    m, l, acc = carry
    m_new = jnp.maximum(m, jnp.max(s, axis=1, keepdims=True))
    alpha = jnp.exp(m - m_new)
    p = jnp.exp(s - m_new)
    return (m_new, alpha * l + jnp.sum(p, axis=1, keepdims=True),
            alpha * acc + _dot(p.astype(BF16), v_t, NT))


def _s_cmp_kernel(pt_ref, *refs, n_pg, past):
    pages = refs[:n_pg]
    qx_ref, pool_ref, oc_ref, bias_ref, km_scr = refs[n_pg:]
    c = pl.program_id(1)
    nch = pl.num_programs(1)
    T = qx_ref.shape[0]
    per = n_pg * (PAGE // CMP_BLOCK)
    hs = per // 2
    x = jnp.concatenate([pg[0].astype(BF16) for pg in pages], axis=1)
    km_scr[c] = _dot(x, pool_ref[...])

    @pl.when(c == nch - 1)
    def _():
        n_steps = km_scr.shape[0]
        nblk = n_steps * per
        npast = nblk // 2
        gl = KVH * T
        kvm = jnp.concatenate([km_scr[i] for i in range(n_steps)], axis=1)
        qr = _q_rows(qx_ref)
        row = lax.broadcasted_iota(I32, (NSA_HEADS * T, 1), 0)
        qpos = past + (row % T)
        lane = lax.broadcasted_iota(I32, (1, nblk), 1)
        jsel = (lane // per) * hs + (lane % hs)
        odd = (lane // hs) % 2
        vis = ((2 * jsel + odd + 1) * CMP_BLOCK - 1) <= qpos
        p = _softmax_lanes(_dot(qr, kvm[0:KV_W].astype(BF16)), vis)
        o_c = _dot(p.astype(BF16), kvm[KV_W:KV_ROW].astype(BF16), NT)
        oc_ref[...] = _heads_to_row(o_c, T)
        imp = p[0:gl] + p[gl:2 * gl] + p[2 * gl:3 * gl] + p[3 * gl:4 * gl]
        imp = imp + pltpu.roll(imp, nblk - hs, 1)
        imp = jnp.concatenate([imp, jnp.zeros((gl, 128), F32)], axis=1)
        lane2 = lax.broadcasted_iota(I32, (1, nblk + 128), 1)
        jid = jnp.where(lane2 < nblk, (lane2 // per) * hs + (lane2 % hs), npast + lane2 - nblk)
        cand = ((lane2 < nblk) & (((lane2 // hs) % 2) == 0)) | (lane2 == nblk)
        cur = (past + (lax.broadcasted_iota(I32, (gl, 1), 0) % T)) // SEL_BLOCK
        forced = (jid == 0) | (jid == cur)
        score = jnp.where(cand, jnp.where(jid <= cur, imp + jnp.where(forced, FORCE_BONUS, 0.0), NEG), -jnp.inf)
        keep = (_topk_lanes(score, jid.astype(F32), cand, SEL_TOPK) > 0.5) & (score > NEG / 2)
        bias = jnp.where(keep, 0.0, NEG)
        bias = jnp.concatenate([bias] * GROUP, axis=0)
        zpad = jnp.zeros((NSA_HEADS * T, 128 - hs), F32)
        for i in range(n_steps):
            bias_ref[0, i] = jnp.concatenate([bias[:, i * per:i * per + hs], zpad], axis=1)
        bias_ref[0, n_steps] = bias[:, nblk:nblk + 128]


def _pool_matrix(n_pg):
    rows = n_pg * PAGE
    per = rows // CMP_BLOCK
    blk = jnp.arange(rows) // CMP_BLOCK
    col = (blk % 2) * (per // 2) + blk // 2
    return jnp.where(col[:, None] == jnp.arange(per)[None, :], 1.0 / CMP_BLOCK, 0.0).astype(BF16)


def _s_cmp(page_table, pool_t, qx, DB, T, n_pg=32):
    n_pages = page_table.shape[1]
    past = n_pages * PAGE
    n_pg = min(n_pg, n_pages)
    nch = n_pages // n_pg
    per = n_pg * (PAGE // CMP_BLOCK)
    L = NSA_HEADS * T
    page_spec = lambda i: pl.BlockSpec((1, KV_ROW, PAGE), lambda b, c, pt: (pt[b, c * n_pg + i], 0, 0))
    gs = pltpu.PrefetchScalarGridSpec(
        num_scalar_prefetch=1, grid=(DB, nch),
        in_specs=[page_spec(i) for i in range(n_pg)] + [
            pl.BlockSpec((T, NSA_HEADS * KV_W), lambda b, c, pt: (b, 0)),
            pl.BlockSpec((n_pg * PAGE, per), lambda b, c, pt: (0, 0))],
        out_specs=[pl.BlockSpec((T, NSA_HEADS * NSA_HD), lambda b, c, pt: (b, 0)),
                   pl.BlockSpec((1, nch + 1, L, 128), lambda b, c, pt: (b, 0, 0, 0))],
        scratch_shapes=[pltpu.VMEM((nch, KV_ROW, per), F32)])
    return pl.pallas_call(
        functools.partial(_s_cmp_kernel, n_pg=n_pg, past=past), grid_spec=gs,
        out_shape=[jax.ShapeDtypeStruct((DB * T, NSA_HEADS * NSA_HD), F32),
                   jax.ShapeDtypeStruct((DB, nch + 1, L, 128), F32)],
        compiler_params=_cparams(2), name="sample_cmp",
    )(page_table, *([pool_t] * n_pg), qx, _pool_matrix(n_pg))


def _s_attn_kernel(pt_ref, *refs, n_pg, past):
    pages = refs[:n_pg]
    (qx_ref, bias_ref, exp_ref, win_ref, ksn_ref, kwn_ref, oc_ref, gr_ref,
     o_ref, wout_ref, m_scr, l_scr, acc_scr) = refs[n_pg:]
    c = pl.program_id(1)
    n_steps = pl.num_programs(1)
    T = qx_ref.shape[0]
    L = NSA_HEADS * T
    qr = _q_rows(qx_ref)

    @pl.when(c == 0)
    def _():
        m_scr[...] = jnp.full(m_scr.shape, M_INIT, F32)
        l_scr[...] = jnp.zeros_like(l_scr)
        acc_scr[...] = jnp.zeros_like(acc_scr)

    k_t = jnp.concatenate([pg[0, 0:KV_W, :].astype(BF16) for pg in pages], axis=1)
    v_t = jnp.concatenate([pg[0, KV_W:KV_ROW, :].astype(BF16) for pg in pages], axis=1)
    s = _dot(qr, k_t) + _dot(bias_ref[0, c].astype(BF16), exp_ref[...])
    m, l, acc = _online_lanes((m_scr[...], l_scr[...], acc_scr[...]), s, v_t)
    m_scr[...] = m
    l_scr[...] = l
    acc_scr[...] = acc

    @pl.when(c == n_steps - 1)
    def _():
        row = lax.broadcasted_iota(I32, (L, 1), 0)
        tq = row % T
        pad = 128
        jl = lax.broadcasted_iota(I32, (1, pad), 1)
        new_ok = (jl < T) & (jl <= tq)
        zpad = jnp.zeros((pad - T, KV_ROW), F32)

        def new_rows(ref):
            t = jnp.concatenate([ref[...], zpad], axis=0).T
            return t[0:KV_W].astype(BF16), t[KV_W:KV_ROW].astype(BF16), t

        k_t, v_t, _ = new_rows(ksn_ref)
        s = jnp.where(new_ok, _dot(qr, k_t), NEG) + bias_ref[0, n_steps][:, 0:1]
        m_s, l_s, a_s = _online_lanes((m_scr[...], l_scr[...], acc_scr[...]), s, v_t)
        o_s = a_s / jnp.maximum(l_s, 1e-30)
        wb = win_ref.shape[2]
        win = win_ref[0]
        il = lax.broadcasted_iota(I32, (1, wb), 1)
        d = (past + tq) - (past - wb + il)
        init = (jnp.full((L, 1), M_INIT, F32), jnp.zeros((L, 1), F32), jnp.zeros((L, KV_W), F32))
        k_t, v_t, new_t = new_rows(kwn_ref)
        carry = _online_lanes(init, jnp.where(new_ok, _dot(qr, k_t), NEG), v_t)
        s = jnp.where((d >= 0) & (d < WINDOW), _dot(qr, win[0:KV_W].astype(BF16)), NEG)
        m_w, l_w, a_w = _online_lanes(carry, s, win[KV_W:KV_ROW].astype(BF16))
        o_w = a_w / jnp.maximum(l_w, 1e-30)
        W = NSA_HEADS * NSA_HD
        gts = jax.nn.sigmoid(gr_ref[...])
        o_ref[...] = (gts[:, 0:W] * oc_ref[...] + gts[:, W:2 * W] * _heads_to_row(o_s, T)
                      + gts[:, 2 * W:3 * W] * _heads_to_row(o_w, T))
        shifted = pltpu.roll(win, wb - T, 1)
        wout_ref[0, :, 0:wb - 128] = shifted[:, 0:wb - 128]
        tail = jnp.where(jl >= 128 - T, pltpu.roll(new_t, 128 - T, 1), shifted[:, wb - 128:wb])
        wout_ref[0, :, wb - 128:wb] = tail


def _s_attn(page_table, pool_t, qx, bias, win_t, kvs_new, kvw_new, o_c, g_rep, DB, T, n_pg=32):
    n_pages = page_table.shape[1]
    past = n_pages * PAGE
    n_pg = min(n_pg, n_pages)
    nch = n_pages // n_pg
    L = NSA_HEADS * T
    W = NSA_HEADS * NSA_HD
    keys = n_pg * PAGE
    expand = jnp.where(jnp.arange(128)[:, None] == (jnp.arange(keys) // SEL_BLOCK)[None, :], 1.0, 0.0).astype(BF16)
    page_spec = lambda i: pl.BlockSpec((1, KV_ROW, PAGE), lambda b, c, pt: (pt[b, c * n_pg + i], 0, 0))
    rowb = lambda w: pl.BlockSpec((T, w), lambda b, c, pt: (b, 0))
    per_b = lambda a: pl.BlockSpec((1,) + a.shape[1:], lambda b, c, pt: (b,) + (0,) * (a.ndim - 1))
    gs = pltpu.PrefetchScalarGridSpec(
        num_scalar_prefetch=1, grid=(DB, nch),
        in_specs=[page_spec(i) for i in range(n_pg)] + [
            rowb(NSA_HEADS * KV_W), per_b(bias), pl.BlockSpec((128, keys), lambda b, c, pt: (0, 0)),
            per_b(win_t), rowb(KV_ROW), rowb(KV_ROW), rowb(W), rowb(3 * W)],
        out_specs=[rowb(W), per_b(win_t)],
        scratch_shapes=[pltpu.VMEM((L, 1), F32), pltpu.VMEM((L, 1), F32), pltpu.VMEM((L, KV_W), F32)])
    return pl.pallas_call(
        functools.partial(_s_attn_kernel, n_pg=n_pg, past=past), grid_spec=gs,
        out_shape=[jax.ShapeDtypeStruct((DB * T, W), F32), jax.ShapeDtypeStruct(win_t.shape, F32)],
        compiler_params=_cparams(2), name="sample_attn",
    )(page_table, *([pool_t] * n_pg), qx, bias, expand, win_t, kvs_new, kvw_new, o_c, g_rep)


def _prep_weights(w_in):
    o_q = 4 * HG_HEADS * HG_DK
    o_kv = o_q + NSA_HEADS * NSA_HD
    o_ng = o_kv + 6 * KV_W
    o_mg = o_ng + NSA_HEADS * 3
    w_hg = w_in[:, :o_q].astype(BF16)
    w_q = w_in[:, o_q:o_kv] * ATT_SCALE
    w_kv = w_in[:, o_kv:o_ng].astype(BF16)
    w_ng = w_in[:, o_ng:o_mg]
    w_mg = w_in[:, o_mg:].astype(BF16)
    w_k = jnp.concatenate([w_in[:, o_kv + 2 * i * KV_W:o_kv + (2 * i + 1) * KV_W] for i in range(3)], axis=1)
    w_k = w_k.astype(BF16)
    w_t = jnp.concatenate([w_q * LOG2E, w_in[:, o_kv:o_ng]], axis=1).T.astype(BF16)
    ng4 = w_ng.reshape(D_MODEL, KVH, GROUP, 3).transpose(0, 1, 3, 2)
    ng4 = jnp.pad(ng4.reshape(D_MODEL, KVH, 3 * GROUP), ((0, 0), (0, 0), (0, 16 - 3 * GROUP)))
    w_ngt = ng4.reshape(D_MODEL, KVH * 16).T.astype(BF16)
    wq5 = w_q.reshape(D_MODEL, KVH, GROUP, 1, NSA_HD)
    own = jnp.arange(KVH).reshape(1, KVH, 1, 1, 1) == jnp.arange(KVH).reshape(1, 1, 1, KVH, 1)
    w_qx = jnp.where(own, wq5, 0.0).reshape(D_MODEL, NSA_HEADS * KV_W).astype(BF16)
    w_gr = jnp.repeat(w_ng.reshape(D_MODEL, NSA_HEADS, 3).transpose(0, 2, 1), NSA_HD, axis=2)
    w_gr = w_gr.reshape(D_MODEL, 3 * NSA_HEADS * NSA_HD).astype(BF16)
    return w_hg, w_kv, w_k, w_mg, w_t, w_ngt, w_qx, w_gr


def kernel(x_prompt, x_sample, cache_cmp_kv, cache_sel_kv, page_table, state_win_kv, state_hgrn, state_conv, w_in, hg_lb_logits, hg_norm_g, w_out, w_up, conv_w, conv_b, w_down, g_pre_mix, g_post_mix, g_pre_ffn, g_post_ffn):
    B, T, D = x_prompt.shape
    DB, TS, _ = x_sample.shape
    depth = w_in.shape[0]
    assert depth == 1 and D == D_MODEL and T % 256 == 0 and TS == 8 and TS < CMP_BLOCK
    C2 = 2 * D_FF
    l = 0

    lbs = jnp.cumsum(jax.nn.softmax(hg_lb_logits.astype(F32), axis=0), axis=0)[l]
    w_hg, w_kv, w_k, w_mg, w_t, w_ngt, w_qx, w_gr = _prep_weights(w_in[l])
    w_out_b, w_up_b, w_dn_b = w_out[l].astype(BF16), w_up[l].astype(BF16), w_down[l].astype(BF16)
    fmaj = lambda a: a.transpose(0, 2, 3, 4, 1).reshape(a.shape[0], KV_ROW, a.shape[1])
    rows6 = lambda a: a.reshape(a.shape[0], 2, KVH, NSA_HD, a.shape[2]).transpose(0, 4, 1, 2, 3)[None]

    outs = {}
    for name, x, nb, tl in (("p", x_prompt, B, T), ("s", x_sample, DB, TS)):
        R = nb * tl
        x2 = x.reshape(R, D)
        s0 = jnp.zeros((nb, HG_HEADS, HG_DK, HG_DK), F32) if name == "p" else state_hgrn[l]
        o_hg, s_hg = _hgrn(x2, g_pre_mix[l], w_hg, lbs, hg_norm_g[l], s0, nb, tl)
        if name == "p":
            kc, ksw, qt, kvt_c, kvt_s, kvt_w, win_t, ngt = _proj_prompt(x2, g_pre_mix[l], w_k, w_t, w_ngt, nb, tl)
            kmean, vmt = _cmp_prep(kc, kvt_c, nb, tl)
            o_nsa = _nsa_prompt(qt, kmean, vmt, ksw, kvt_s, kvt_w, ngt, nb, tl)
            kv_out = (rows6(kvt_c), rows6(kvt_s), rows6(win_t))
            cbuf0 = jnp.zeros((nb, CONV_W - 1, C2), F32)
        else:
            h = _rmsnorm_bf16(x2, g_pre_mix[l])
            kv = [_mm(h, w_kv[:, i * KV_ROW:(i + 1) * KV_ROW], F32, name="proj_kv") for i in range(3)]
            qx = _mm(h, w_qx, F32, name="proj_q_pad")
            g_rep = _mm(h, w_gr, F32, name="proj_gate_rep")
            o_c, bias = _s_cmp(page_table, fmaj(cache_cmp_kv[l]), qx, nb, tl)
            o_nsa, win_t = _s_attn(page_table, fmaj(cache_sel_kv[l]), qx, bias, fmaj(state_win_kv[l]),
                                   kv[1], kv[2], o_c, g_rep, nb, tl)
            kv6 = lambda a: a.reshape(1, nb, tl, 2, KVH, NSA_HD)
            kv_out = (kv6(kv[0]), kv6(kv[1]), rows6(win_t))
            cbuf0 = state_conv[l]
        x1 = _merge(x2, g_pre_mix[l], w_mg, o_hg, o_nsa, w_out_b, g_post_mix[l])
        x3, cbuf = _ffn(x1, nb, g_pre_ffn[l], w_up_b, conv_w[l], conv_b[l], w_dn_b, g_post_ffn[l], cbuf0, tm=1024)
        outs[name] = (x3.reshape(nb, tl, D),) + kv_out + (s_hg[None], cbuf[None])
    p, s = outs["p"], outs["s"]
    return (p[0], s[0], p[1], s[1], p[2], s[2], p[3], s[3], p[4], s[4], p[5], s[5])
```
